```python
import jax, jax.numpy as jnp
from jax import lax
import numpy as np

D_MODEL = 1024
BATCH = 4
SEQ = 4096
DEPTH = 2

HEAD_DIM = 64
N_Q_HEADS = 8
N_KV_HEADS = 2
MIX_WIDTH = N_Q_HEADS * HEAD_DIM
KV_WIDTH = N_KV_HEADS * HEAD_DIM
N_MIXERS = 3
D_FF = 4 * D_MODEL
ROPE_THETA = 10000.0
EPS = 1e-6
NEG_INF = -1e30
Q_BLOCK = 128

IDX_HEADS = 4
IDX_DIM = 64
DSA_TOPK_MAX = 256

SWA_WINDOW = 128

CMP_LEN = 32
CMP_STRIDE = 16
CMP_HIDDEN = 256
SLC_LEN = 64
SLC_TOPN = 16
SLC_Q_BLOCK = 64
NSA_WINDOW = 512
FORCE_SCORE = 1e9

IN_SIZES = (
    MIX_WIDTH, KV_WIDTH, KV_WIDTH, IDX_HEADS * IDX_DIM, IDX_DIM, IDX_HEADS,
    MIX_WIDTH, KV_WIDTH, KV_WIDTH,
    MIX_WIDTH, KV_WIDTH, KV_WIDTH, KV_WIDTH, KV_WIDTH, KV_WIDTH, KV_WIDTH, 3 * N_Q_HEADS,
    N_MIXERS * D_MODEL,
)
D_IN = sum(IN_SIZES)
IN_OFFSETS = tuple(int(o) for o in np.cumsum(IN_SIZES)[:-1])

kernel_name = "hybrid_dsa_swa_nsa_block"


def rms_norm(x, g):
    xf = x.astype(jnp.float32)
    y = xf * lax.rsqrt(jnp.mean(xf * xf, axis=-1, keepdims=True) + EPS)
    return (y * g.astype(jnp.float32)).astype(x.dtype)


def rope_tables(seq_len, dim):
    inv_freq = ROPE_THETA ** (-jnp.arange(0, dim, 2, dtype=jnp.float32) / dim)
    ang = jnp.arange(seq_len, dtype=jnp.float32)[:, None] * inv_freq[None, :]
    return jnp.cos(ang), jnp.sin(ang)


def apply_rope(x, cos, sin):
    c = cos[None, :, None, :].astype(x.dtype)
    s = sin[None, :, None, :].astype(x.dtype)
    x1, x2 = jnp.split(x, 2, axis=-1)
    return jnp.concatenate([x1 * c - x2 * s, x2 * c + x1 * s], axis=-1)


def masked_softmax(s, mask):
    p = jax.nn.softmax(jnp.where(mask, s.astype(jnp.float32), NEG_INF), axis=-1)
    return p * mask.astype(jnp.float32)


def to_blocks(t, n_blocks):
    return jnp.swapaxes(t.reshape(t.shape[0], n_blocks, -1, *t.shape[2:]), 0, 1)


def from_blocks(t):
    t = jnp.swapaxes(t, 0, 1)
    return t.reshape(t.shape[0], -1, *t.shape[3:])


def banded_attention(q, k, v, window, sinks):
    B, T, H, D = q.shape
    Hkv = k.shape[2]
    G = H // Hkv
    nb = T // Q_BLOCK
    n_prev = (window + Q_BLOCK - 2) // Q_BLOCK
    pad = n_prev * Q_BLOCK
    width = pad + Q_BLOCK
    kp = jnp.pad(k, ((0, 0), (pad, 0), (0, 0), (0, 0)))
    vp = jnp.pad(v, ((0, 0), (pad, 0), (0, 0), (0, 0)))

    def band(t):
        return jnp.concatenate(
            [t[:, j * Q_BLOCK: j * Q_BLOCK + T].reshape(B, nb, Q_BLOCK, Hkv, D) for j in range(n_prev + 1)], axis=2)

    kb, vb = band(kp), band(vp)
    qb = q.reshape(B, nb, Q_BLOCK, Hkv, G, D)
    r = jnp.arange(Q_BLOCK)[:, None]
    c = jnp.arange(width)[None, :]
    rel = r + pad - c
    kpos = jnp.arange(nb)[:, None, None] * Q_BLOCK - pad + c[None]
    mask = (rel >= 0)[None] & (rel < window)[None] & (kpos >= 0)
    s = jnp.einsum('bnqhgd,bnchd->bnhgqc', qb, kb).astype(jnp.float32) * (D ** -0.5)
    s = jnp.where(mask[None, :, None, None], s, NEG_INF)
    if sinks is not None:
        sink = jnp.broadcast_to(sinks.astype(jnp.float32).reshape(1, 1, Hkv, G, 1, 1), s.shape[:-1] + (1,))
        p = jax.nn.softmax(jnp.concatenate([s, sink], axis=-1), axis=-1)[..., :-1]
    else:
        p = jax.nn.softmax(s, axis=-1)
    o = jnp.einsum('bnhgqc,bnchd->bnqhgd', p.astype(v.dtype), vb)
    return o.reshape(B, T, H, D)


def dsa_attention(q, k, v, iq, ik, iw):
    B, T, H, D = q.shape
    Hkv = k.shape[2]
    G = H // Hkv
    k_top = min(DSA_TOPK_MAX, T // 4)
    nb = T // Q_BLOCK
    kpos = jnp.arange(T)
    gather = jax.vmap(lambda table, idx: table[idx])

    def one_block(inp):
        i, qb, iqb, iwb = inp
        qpos = i * Q_BLOCK + jnp.arange(Q_BLOCK)
        rel = jax.nn.relu(jnp.einsum('bqhd,bsd->bqhs', iqb, ik).astype(jnp.float32) * (IDX_DIM ** -0.5))
        score = jnp.einsum('bqh,bqhs->bqs', iwb.astype(jnp.float32), rel)
        score = jnp.where((kpos[None, :] <= qpos[:, None])[None], score, NEG_INF)
        _, idx = lax.top_k(score, k_top)
        kg, vg = gather(k, idx), gather(v, idx)
        s = jnp.einsum('bqhgd,bqkhd->bqhgk', qb.reshape(B, Q_BLOCK, Hkv, G, D), kg).astype(jnp.float32) * (D ** -0.5)
        p = masked_softmax(s, (idx <= qpos[None, :, None])[:, :, None, None, :])
        o = jnp.einsum('bqhgk,bqkhd->bqhgd', p.astype(v.dtype), vg)
        return o.reshape(B, Q_BLOCK, H, D)

    out = lax.map(one_block, (jnp.arange(nb), to_blocks(q, nb), to_blocks(iq, nb), to_blocks(iw, nb)))
    return from_blocks(out)


def nsa_attention(q, kc, vc, ks, vs, kw, vw, gates, pe_k, pe_v, w_ck1, w_ck2, w_cv1, w_cv2):
    B, T, H, D = q.shape
    Hkv = kc.shape[2]
    G = H // Hkv
    scale = D ** -0.5
    tpos = jnp.arange(T)
    qg = q.reshape(B, T, Hkv, G, D)

    n_cmp = (T - CMP_LEN) // CMP_STRIDE + 1
    cstart = jnp.arange(n_cmp) * CMP_STRIDE
    win = cstart[:, None] + jnp.arange(CMP_LEN)[None, :]

    def compress(t, pe, w1, w2):
        blk = t[:, win] + pe[None, None, :, None, :]
        blk = jnp.swapaxes(blk, 2, 3).reshape(B, n_cmp, Hkv, CMP_LEN * D)
        return jax.nn.relu(blk @ w1) @ w2

    k_cmp = compress(kc, pe_k, w_ck1, w_ck2)
    v_cmp = compress(vc, pe_v, w_cv1, w_cv2)
    cmask = (cstart + CMP_LEN - 1)[None, :] <= tpos[:, None]
    s_cmp = jnp.einsum('bthgd,bchd->bthgc', qg, k_cmp).astype(jnp.float32) * scale
    p_cmp = masked_softmax(s_cmp, cmask[None, :, None, None, :])
    o_cmp = jnp.einsum('bthgc,bchd->bthgd', p_cmp.astype(vc.dtype), v_cmp)

    n_sel = T // SLC_LEN
    n_top = min(SLC_TOPN, n_sel)
    sstart = jnp.arange(n_sel) * SLC_LEN
    overlap = ((cstart[:, None] < sstart[None, :] + SLC_LEN)
               & (cstart[:, None] + CMP_LEN > sstart[None, :])).astype(jnp.float32)
    imp = jnp.einsum('bthgc,cj->bthj', p_cmp, overlap)
    cur = (tpos // SLC_LEN)[:, None]
    jb = jnp.arange(n_sel)[None, :]
    forced = (jb == 0) | (jb == cur) | (jb == cur - 1)
    imp = jnp.where(forced[None, :, None, :], FORCE_SCORE, imp)
    imp = jnp.where((sstart[None, :] <= tpos[:, None])[None, :, None, :], imp, NEG_INF)
    _, sel = lax.top_k(imp, n_top)
    ks_blk = jnp.moveaxis(ks.reshape(B, n_sel, SLC_LEN, Hkv, D), 3, 1)
    vs_blk = jnp.moveaxis(vs.reshape(B, n_sel, SLC_LEN, Hkv, D), 3, 1)
    gather = jax.vmap(jax.vmap(lambda table, idx: table[idx]))
    n_qb = T // SLC_Q_BLOCK
    n_keys = n_top * SLC_LEN

    def one_block(inp):
        i, qb, selb = inp
        qpos = i * SLC_Q_BLOCK + jnp.arange(SLC_Q_BLOCK)
        st = jnp.swapaxes(selb, 1, 2)
        kg = gather(ks_blk, st).reshape(B, Hkv, SLC_Q_BLOCK, n_keys, D)
        vg = gather(vs_blk, st).reshape(B, Hkv, SLC_Q_BLOCK, n_keys, D)
        kpos = (st[..., None] * SLC_LEN + jnp.arange(SLC_LEN)).reshape(B, Hkv, SLC_Q_BLOCK, n_keys)
        valid = jnp.swapaxes(kpos <= qpos[None, None, :, None], 1, 2)
        s = jnp.einsum('bqhgd,bhqkd->bqhgk', qb, kg).astype(jnp.float32) * scale
        p = masked_softmax(s, valid[:, :, :, None, :])
        return jnp.einsum('bqhgk,bhqkd->bqhgd', p.astype(vs.dtype), vg)

    o_slc = from_blocks(lax.map(one_block, (jnp.arange(n_qb), to_blocks(qg, n_qb), to_blocks(sel, n_qb))))

    o_win = banded_attention(q, kw, vw, NSA_WINDOW, None)

    g = jax.nn.sigmoid(gates.reshape(B, T, H, 3))
    return (g[..., 0:1] * o_cmp.reshape(B, T, H, D)
            + g[..., 1:2] * o_slc.reshape(B, T, H, D)
            + g[..., 2:3] * o_win)


def setup_inputs(seed: int = 0) -> dict:
    key = jax.random.key(seed)
    ks = jax.random.split(key, 18)

    def nrm(k, shape, scale):
        return jax.random.normal(k, shape, jnp.float32) * scale

    return {
        "x": nrm(ks[0], (BATCH, SEQ, D_MODEL), 1.0),
        "norm_mix": 1.0 + nrm(ks[1], (DEPTH, D_MODEL), 0.1),
        "w_in": nrm(ks[2], (DEPTH, D_MODEL, D_IN), D_MODEL ** -0.5),
        "q_norm": 1.0 + nrm(ks[3], (DEPTH, N_MIXERS, HEAD_DIM), 0.1),
        "k_norm": 1.0 + nrm(ks[4], (DEPTH, N_MIXERS, HEAD_DIM), 0.1),
        "sinks": nrm(ks[5], (DEPTH, N_Q_HEADS), 1.0),
        "cmp_pe_k": nrm(ks[6], (DEPTH, CMP_LEN, HEAD_DIM), 0.1),
        "cmp_pe_v": nrm(ks[7], (DEPTH, CMP_LEN, HEAD_DIM), 0.1),
        "w_ck1": nrm(ks[8], (DEPTH, CMP_LEN * HEAD_DIM, CMP_HIDDEN), (CMP_LEN * HEAD_DIM) ** -0.5),
        "w_ck2": nrm(ks[9], (DEPTH, CMP_HIDDEN, HEAD_DIM), (CMP_HIDDEN / 2) ** -0.5),
        "w_cv1": nrm(ks[10], (DEPTH, CMP_LEN * HEAD_DIM, CMP_HIDDEN), (CMP_LEN * HEAD_DIM) ** -0.5),
        "w_cv2": nrm(ks[11], (DEPTH, CMP_HIDDEN, HEAD_DIM), (CMP_HIDDEN / 2) ** -0.5),
        "w_branch": nrm(ks[12], (DEPTH, N_MIXERS, MIX_WIDTH, D_MODEL), MIX_WIDTH ** -0.5),
        "w_out": nrm(ks[13], (DEPTH, D_MODEL, D_MODEL), D_MODEL ** -0.5),
        "norm_mlp": 1.0 + nrm(ks[14], (DEPTH, D_MODEL), 0.1),
        "w_up": nrm(ks[15], (DEPTH, D_MODEL, D_FF), D_MODEL ** -0.5),
        "w_down": nrm(ks[16], (DEPTH, D_FF, D_MODEL), D_FF ** -0.5),
    }


def reference(x, norm_mix, w_in, q_norm, k_norm, sinks, cmp_pe_k, cmp_pe_v,
              w_ck1, w_ck2, w_cv1, w_cv2, w_branch, w_out, norm_mlp, w_up, w_down):
    B, T, _ = x.shape
    cos, sin = rope_tables(T, HEAD_DIM)

    def heads(t, n):
        return t.reshape(B, T, n, -1)

    def qk(t, n, gain):
        return apply_rope(rms_norm(heads(t, n), gain), cos, sin)

    for l in range(DEPTH):
        h = rms_norm(x, norm_mix[l])
        z = h @ w_in[l]
        (qa, ka, va, iq, ik, iw, qb, kb, vb,
         qc, kc, vc, ksl, vsl, kwn, vwn, gc, gm) = jnp.split(z, IN_OFFSETS, axis=-1)

        o_a = dsa_attention(qk(qa, N_Q_HEADS, q_norm[l, 0]), qk(ka, N_KV_HEADS, k_norm[l, 0]),
                            heads(va, N_KV_HEADS),
                            apply_rope(heads(iq, IDX_HEADS), cos, sin),
                            apply_rope(heads(ik, 1), cos, sin)[:, :, 0],
                            iw * (IDX_HEADS ** -0.5))
        o_b = banded_attention(qk(qb, N_Q_HEADS, q_norm[l, 1]), qk(kb, N_KV_HEADS, k_norm[l, 1]),
                               heads(vb, N_KV_HEADS), SWA_WINDOW, sinks[l])
        kn = k_norm[l, 2]
        o_c = nsa_attention(qk(qc, N_Q_HEADS, q_norm[l, 2]),
                            qk(kc, N_KV_HEADS, kn), heads(vc, N_KV_HEADS),
                            qk(ksl, N_KV_HEADS, kn), heads(vsl, N_KV_HEADS),
                            qk(kwn, N_KV_HEADS, kn), heads(vwn, N_KV_HEADS),
                            gc, cmp_pe_k[l], cmp_pe_v[l], w_ck1[l], w_ck2[l], w_cv1[l], w_cv2[l])

        o_all = jnp.stack([o_a, o_b, o_c], axis=2).reshape(B, T, N_MIXERS, MIX_WIDTH)
        y = jnp.einsum('btnc,ncd->btnd', o_all, w_branch[l])
        g = jax.nn.sigmoid(gm.reshape(B, T, N_MIXERS, D_MODEL))
        x = x + jnp.sum(g * y, axis=2) @ w_out[l]

        h2 = rms_norm(x, norm_mlp[l])
        x = x + jnp.square(jax.nn.relu(h2 @ w_up[l])) @ w_down[l]
    return x
```

```python
import functools

import numpy as np
import jax
import jax.numpy as jnp
from jax import lax
from jax.experimental import pallas as pl
from jax.experimental.pallas import tpu as pltpu

F32 = jnp.float32
BF16 = jnp.bfloat16

D_MODEL = 1024
HEAD_DIM = 64
HALF = HEAD_DIM // 2
N_Q_HEADS = 8
N_KV_HEADS = 2
GROUP = N_Q_HEADS // N_KV_HEADS
MIX_WIDTH = N_Q_HEADS * HEAD_DIM
KV_WIDTH = N_KV_HEADS * HEAD_DIM
N_MIXERS = 3
D_FF = 4 * D_MODEL
ROPE_THETA = 10000.0
EPS = 1e-6
NEG_INF = -1e30
MASK_VAL = -2e30
M_INIT = -1e30
BIG = 3e38
Q_BLOCK = 128
KEY_CHUNK = 512
IDX_HEADS = 4
IDX_DIM = 64
DSA_TOPK_MAX = 256
SWA_WINDOW = 128
CMP_LEN = 32
CMP_STRIDE = 16
CMP_HIDDEN = 256
SLC_LEN = 64
SLC_TOPN = 16
NSA_WINDOW = 512
FORCE_SCORE = 1e9
LANE = 128
VMEM_LIMIT = 56 * 1024 * 1024

IN_SIZES = (
    MIX_WIDTH, KV_WIDTH, KV_WIDTH, IDX_HEADS * IDX_DIM, IDX_DIM, IDX_HEADS,
    MIX_WIDTH, KV_WIDTH, KV_WIDTH,
    MIX_WIDTH, KV_WIDTH, KV_WIDTH, KV_WIDTH, KV_WIDTH, KV_WIDTH, KV_WIDTH, 3 * N_Q_HEADS,
    N_MIXERS * D_MODEL,
)
_OFF = np.concatenate([[0], np.cumsum(IN_SIZES)]).astype(int)
(S_QA, S_KA, S_VA, S_IQ, S_IK, S_IW, S_QB, S_KB, S_VB,
 S_QC, S_KC, S_VC, S_KSL, S_VSL, S_KWN, S_VWN, S_GC, S_GM) = [
    (int(_OFF[i]), int(_OFF[i + 1])) for i in range(len(IN_SIZES))]

R_Q = 0
R_K = R_Q + N_MIXERS * MIX_WIDTH
R_V = R_K + 5 * KV_WIDTH
R_IQ = R_V + 5 * KV_WIDTH
R_IK = R_IQ + IDX_HEADS * IDX_DIM
R_MISC = R_IK + IDX_DIM
MISC_ROWS = 32
GATE_ROW0 = 8
R_END = R_MISC + MISC_ROWS
PROJ_TM = 512


def _cparams(sem):
    return pltpu.CompilerParams(dimension_semantics=sem, vmem_limit_bytes=VMEM_LIMIT)


def _lane_tile(a, n):
    return a if n == 1 else jnp.concatenate([a] * n, axis=1)


def _proj_kernel(x_ref, g_ref, w_ref, gq_ref, gk_ref, cos_ref, sin_ref,
                 q_ref, k_ref, v_ref, iq_ref, ik_ref, misc_ref):
    tm = x_ref.shape[1]
    rep = tm // LANE
    x = x_ref[...]
    ss = jnp.sum(x * x, axis=0, keepdims=True)
    h = (x * lax.rsqrt(ss * (1.0 / D_MODEL) + EPS) * _lane_tile(g_ref[...], rep)).astype(BF16)
    cos = cos_ref[...]
    sin = sin_ref[...]

    def rope(z):
        z1, z2 = z[:HALF], z[HALF:]
        return jnp.concatenate([z1 * cos - z2 * sin, z2 * cos + z1 * sin], axis=0)

    def head_norm(z, gain):
        ssq = jnp.sum(z * z, axis=0, keepdims=True)
        return z * lax.rsqrt(ssq * (1.0 / HEAD_DIM) + EPS) * _lane_tile(gain, rep)

    for m in range(N_MIXERS):
        z = jnp.dot(w_ref[R_Q + m * MIX_WIDTH:R_Q + (m + 1) * MIX_WIDTH, :], h,
                    preferred_element_type=F32)
        for hh in range(N_Q_HEADS):
            r0 = m * MIX_WIDTH + hh * HEAD_DIM
            zh = head_norm(z[hh * HEAD_DIM:(hh + 1) * HEAD_DIM], gq_ref[r0:r0 + HEAD_DIM, :])
            q_ref[r0:r0 + HEAD_DIM, :] = (rope(zh) * (HEAD_DIM ** -0.5)).astype(BF16)

    z = jnp.dot(w_ref[R_K:R_V, :], h, preferred_element_type=F32)
    for hh in range(5 * N_KV_HEADS):
        r0 = hh * HEAD_DIM
        zh = head_norm(z[r0:r0 + HEAD_DIM], gk_ref[r0:r0 + HEAD_DIM, :])
        k_ref[r0:r0 + HEAD_DIM, :] = rope(zh).astype(BF16)

    z = jnp.dot(w_ref[R_V:R_IQ, :], h, preferred_element_type=F32)
    for j in range(rep):
        v_ref[j] = z[:, j * LANE:(j + 1) * LANE].astype(BF16)

    z = jnp.dot(w_ref[R_IQ:R_END, :], h, preferred_element_type=F32)
    for hh in range(IDX_HEADS):
        r0 = hh * IDX_DIM
        iq_ref[r0:r0 + IDX_DIM, :] = rope(z[r0:r0 + IDX_DIM])
    ik_ref[...] = rope(z[R_IK - R_IQ:R_MISC - R_IQ])
    misc_ref[...] = z[R_MISC - R_IQ:R_END - R_IQ]


def _project(xT, g_b, wT, gq_b, gk_b, cosT, sinT, T):
    N = xT.shape[1]
    tm = PROJ_TM
    tpb = T // tm
    rep = tm // LANE
    col = lambda i: (0, i)
    const = lambda i: (0, 0)
    return pl.pallas_call(
        _proj_kernel,
        grid=(N // tm,),
        in_specs=[
            pl.BlockSpec((D_MODEL, tm), col),
            pl.BlockSpec((D_MODEL, LANE), const),
            pl.BlockSpec((R_END, D_MODEL), const),
            pl.BlockSpec((N_MIXERS * MIX_WIDTH, LANE), const),
            pl.BlockSpec((5 * KV_WIDTH, LANE), const),
            pl.BlockSpec((HALF, tm), lambda i: (0, i % tpb)),
            pl.BlockSpec((HALF, tm), lambda i: (0, i % tpb)),
        ],
        out_specs=[
            pl.BlockSpec((N_MIXERS * MIX_WIDTH, tm), col),
            pl.BlockSpec((5 * KV_WIDTH, tm), col),
            pl.BlockSpec((rep, 5 * KV_WIDTH, LANE), lambda i: (i, 0, 0)),
            pl.BlockSpec((IDX_HEADS * IDX_DIM, tm), col),
            pl.BlockSpec((IDX_DIM, tm), col),
            pl.BlockSpec((MISC_ROWS, tm), col),
        ],
        out_shape=[
            jax.ShapeDtypeStruct((N_MIXERS * MIX_WIDTH, N), BF16),
            jax.ShapeDtypeStruct((5 * KV_WIDTH, N), BF16),
            jax.ShapeDtypeStruct((N // LANE, 5 * KV_WIDTH, LANE), BF16),
            jax.ShapeDtypeStruct((IDX_HEADS * IDX_DIM, N), F32),
            jax.ShapeDtypeStruct((IDX_DIM, N), F32),
            jax.ShapeDtypeStruct((MISC_ROWS, N), F32),
        ],
        compiler_params=_cparams(("parallel",)),
        name="in_projection",
    )(xT, g_b, wT, gq_b, gk_b, cosT, sinT)


def _padded_queries(q):
    zeros = jnp.zeros((HEAD_DIM, Q_BLOCK), q.dtype)
    out = []
    for kv in range(N_KV_HEADS):
        cols = []
        for g in range(GROUP):
            hq = kv * GROUP + g
            qh = q[hq * HEAD_DIM:(hq + 1) * HEAD_DIM, :]
            cols.append(jnp.concatenate([qh, zeros] if kv == 0 else [zeros, qh], axis=0))
        out.append(jnp.concatenate(cols, axis=1))
    return out


def _v_chunk(v_ref, blk0, nblk):
    return jnp.concatenate([v_ref[blk0 + j] for j in range(nblk)], axis=1)


def _online_init(m_ref, l_ref, acc_ref):
    m_ref[...] = jnp.full(m_ref.shape, M_INIT, F32)
    l_ref[...] = jnp.zeros(l_ref.shape, F32)
    acc_ref[...] = jnp.zeros(acc_ref.shape, F32)


def _online_step(kv, s, vc, m_ref, l_ref, acc_ref):
    m_old = m_ref[kv]
    m_new = jnp.maximum(m_old, jnp.max(s, axis=0, keepdims=True))
    alpha = jnp.exp(m_old - m_new)
    p = jnp.exp(s - m_new)
    l_ref[kv] = alpha * l_ref[kv] + jnp.sum(p, axis=0, keepdims=True)
    acc_ref[kv] = alpha * acc_ref[kv] + jnp.dot(vc, p.astype(BF16), preferred_element_type=F32)
    m_ref[kv] = m_new


def _online_finish(kv, l_ref, acc_ref):
    l = l_ref[kv]
    inv = jnp.where(l > 0.0, 1.0 / l, 0.0)
    return acc_ref[kv][kv * HEAD_DIM:(kv + 1) * HEAD_DIM, :] * inv


def _dsa_kernel(iq_ref, misc_ref, ik_ref, q_ref, k_ref, v_ref, ltri_ref, o_ref,
                sc_ref, m_ref, l_ref, acc_ref, *, k_top):
    C = KEY_CHUNK
    i = pl.program_id(1)
    nch = i // (C // Q_BLOCK) + 1
    qpos = i * Q_BLOCK + lax.broadcasted_iota(jnp.int32, (1, Q_BLOCK), 1)
    row = lax.broadcasted_iota(jnp.int32, (C, Q_BLOCK), 0)

    iq = iq_ref[...].astype(BF16)
    iq_cat = jnp.concatenate([iq[h * IDX_DIM:(h + 1) * IDX_DIM] for h in range(IDX_HEADS)], axis=1)
    w = misc_ref[0:IDX_HEADS, :] * ((IDX_HEADS ** -0.5) * (IDX_DIM ** -0.5))

    def score_chunk(c, carry):
        lo, hi = carry
        off = pl.multiple_of(c * C, C)
        ikc = ik_ref[pl.ds(off, C), :].astype(BF16)
        d = jnp.dot(ikc, iq_cat, preferred_element_type=F32)
        s = jnp.maximum(d[:, 0:Q_BLOCK], 0.0) * w[0:1, :]
        for h in range(1, IDX_HEADS):
            s = s + jnp.maximum(d[:, h * Q_BLOCK:(h + 1) * Q_BLOCK], 0.0) * w[h:h + 1, :]
        causal = (off + row) <= qpos
        sc_ref[pl.ds(off, C), :] = jnp.where(causal, s, NEG_INF)
        lo = jnp.minimum(lo, jnp.min(jnp.where(causal, s, BIG), axis=0, keepdims=True))
        hi = jnp.maximum(hi, jnp.max(jnp.where(causal, s, -BIG), axis=0, keepdims=True))
        return lo, hi

    lo0, hi0 = lax.fori_loop(0, nch, score_chunk,
                             (jnp.full((1, Q_BLOCK), BIG, F32), jnp.full((1, Q_BLOCK), -BIG, F32)))

    kf = float(k_top)
    select = i * Q_BLOCK >= k_top

    def stats(mid):
        def body(c, carry):
            cnt, a, b = carry
            s = sc_ref[pl.ds(pl.multiple_of(c * C, C), C), :]
            gt = s > mid
            cnt = cnt + jnp.sum(jnp.where(gt, 1.0, 0.0), axis=0, keepdims=True)
            a = jnp.minimum(a, jnp.min(jnp.where(gt, s, BIG), axis=0, keepdims=True))
            b = jnp.maximum(b, jnp.max(jnp.where(gt, -BIG, s), axis=0, keepdims=True))
            return cnt, a, b
        z = jnp.zeros((1, Q_BLOCK), F32)
        return lax.fori_loop(0, nch, body, (z, z + BIG, z - BIG))

    def cond(st):
        lo, hi, _, _, _ = st
        return jnp.logical_and(select, jnp.max(hi - lo) > 0.0)

    def body(st):
        lo, hi, clo, chi, it = st
        interp = (clo - kf + 0.5) / (clo - chi)
        use_half = (it % 2).astype(F32)
        frac = interp * (1.0 - use_half) + 0.5 * use_half
        mid = jnp.maximum(lo + (hi - lo) * frac, lo)
        mid = jnp.where(mid >= hi, lo, mid)
        cnt, a, b = stats(mid)
        up = cnt >= kf
        return (jnp.where(up, a, lo), jnp.where(up, hi, b),
                jnp.where(up, cnt, clo), jnp.where(up, chi, cnt), it + 1)

    n_causal = (qpos + 1).astype(F32)
    lo, _, _, chi, _ = lax.while_loop(
        cond, body, (lo0, hi0, n_causal, jnp.zeros((1, Q_BLOCK), F32), jnp.int32(0)))
    thr = jnp.where(select, lo, NEG_INF)
    need = jnp.where(select, kf - chi, 0.0)

    qpad = _padded_queries(q_ref[...])
    _online_init(m_ref, l_ref, acc_ref)

    def att_chunk(c, tie_carry):
        off = pl.multiple_of(c * C, C)
        s_idx = sc_ref[pl.ds(off, C), :]
        eqf = jnp.where(s_idx == thr, 1.0, 0.0)
        before = jnp.dot(ltri_ref[...], eqf.astype(BF16), preferred_element_type=F32) + tie_carry
        self_ = jnp.where(s_idx > thr, 1.0, jnp.where(before < need, eqf, 0.0))
        self_ = jnp.where((off + row) <= qpos, self_, 0.0)
        sel = _lane_tile(self_, GROUP) > 0.5
        kc = k_ref[pl.ds(off, C), :]
        vc = _v_chunk(v_ref, c * (C // LANE), C // LANE)
        for kv in range(N_KV_HEADS):
            s = jnp.dot(kc, qpad[kv], preferred_element_type=F32)
            _online_step(kv, jnp.where(sel, s, MASK_VAL), vc, m_ref, l_ref, acc_ref)
        return tie_carry + jnp.sum(eqf, axis=0, keepdims=True)

    lax.fori_loop(0, nch, att_chunk, jnp.zeros((1, Q_BLOCK), F32))

    for kv in range(N_KV_HEADS):
        o = _online_finish(kv, l_ref, acc_ref)
        for g in range(GROUP):
            hq = kv * GROUP + g
            o_ref[hq * HEAD_DIM:(hq + 1) * HEAD_DIM, :] = o[:, g * Q_BLOCK:(g + 1) * Q_BLOCK].astype(BF16)


def _dsa(iqT, miscT, ik_rows, qT, k_rows, vT128, ltri, B, T):
    nq = T // Q_BLOCK
    k_top = min(DSA_TOPK_MAX, T // 4)
    N = B * T
    return pl.pallas_call(
        functools.partial(_dsa_kernel, k_top=k_top),
        grid=(B, nq),
        in_specs=[
            pl.BlockSpec((IDX_HEADS * IDX_DIM, Q_BLOCK), lambda b, i: (0, b * nq + i)),
            pl.BlockSpec((MISC_ROWS, Q_BLOCK), lambda b, i: (0, b * nq + i)),
            pl.BlockSpec((T, IDX_DIM), lambda b, i: (b, 0)),
            pl.BlockSpec((MIX_WIDTH, Q_BLOCK), lambda b, i: (0, b * nq + i)),
            pl.BlockSpec((T, KV_WIDTH), lambda b, i: (b, 0)),
            pl.BlockSpec((nq, KV_WIDTH, LANE), lambda b, i: (b, 0, 0)),
            pl.BlockSpec((KEY_CHUNK, KEY_CHUNK), lambda b, i: (0, 0)),
        ],
        out_specs=pl.BlockSpec((MIX_WIDTH, Q_BLOCK), lambda b, i: (0, b * nq + i)),
        out_shape=jax.ShapeDtypeStruct((MIX_WIDTH, N), BF16),
        scratch_shapes=[
            pltpu.VMEM((T, Q_BLOCK), F32),
            pltpu.VMEM((N_KV_HEADS, 1, GROUP * Q_BLOCK), F32),
            pltpu.VMEM((N_KV_HEADS, 1, GROUP * Q_BLOCK), F32),
            pltpu.VMEM((N_KV_HEADS, KV_WIDTH, GROUP * Q_BLOCK), F32),
        ],
        compiler_params=_cparams(("parallel", "arbitrary")),
        name="dsa_attention",
    )(iqT, miscT, ik_rows, qT, k_rows, vT128, ltri)


def _band_kernel(q_ref, k_ref, v_ref, sink_ref, o_ref, *, n_prev, window, use_sink):
    i = pl.program_id(1)
    blk0 = jnp.maximum(i - n_prev, 0)
    nblk = n_prev + 1
    wb = nblk * Q_BLOCK
    kb = k_ref[pl.ds(pl.multiple_of(blk0 * Q_BLOCK, Q_BLOCK), wb), :]
    vb = _v_chunk(v_ref, blk0, nblk)
    qpad = _padded_queries(q_ref[...])
    lanes = GROUP * Q_BLOCK
    kpos = blk0 * Q_BLOCK + lax.broadcasted_iota(jnp.int32, (wb, lanes), 0)
    qpos = i * Q_BLOCK + (lax.broadcasted_iota(jnp.int32, (wb, lanes), 1) & (Q_BLOCK - 1))
    rel = qpos - kpos
    valid = (rel >= 0) & (rel < window)
    for kv in range(N_KV_HEADS):
        s = jnp.where(valid, jnp.dot(kb, qpad[kv], preferred_element_type=F32), MASK_VAL)
        m = jnp.max(s, axis=0, keepdims=True)
        if use_sink:
            sink = sink_ref[kv:kv + 1, :]
            m = jnp.maximum(m, sink)
        p = jnp.exp(s - m)
        l = jnp.sum(p, axis=0, keepdims=True)
        if use_sink:
            l = l + jnp.exp(sink - m)
        o = jnp.dot(vb, p.astype(BF16), preferred_element_type=F32)[kv * HEAD_DIM:(kv + 1) * HEAD_DIM, :]
        o = o * (1.0 / l)
        for g in range(GROUP):
            hq = kv * GROUP + g
            o_ref[hq * HEAD_DIM:(hq + 1) * HEAD_DIM, :] = o[:, g * Q_BLOCK:(g + 1) * Q_BLOCK].astype(o_ref.dtype)


def _band(qT, q_mixer, k_rows, kv_index, vT128, sink_lanes, window, use_sink, out_dtype, B, T, name):
    nq = T // Q_BLOCK
    N = B * T
    n_prev = (window + Q_BLOCK - 2) // Q_BLOCK
    return pl.pallas_call(
        functools.partial(_band_kernel, n_prev=n_prev, window=window, use_sink=use_sink),
        grid=(B, nq),
        in_specs=[
            pl.BlockSpec((MIX_WIDTH, Q_BLOCK), lambda b, i: (q_mixer, b * nq + i)),
            pl.BlockSpec((T, KV_WIDTH), lambda b, i: (b, kv_index)),
            pl.BlockSpec((nq, KV_WIDTH, LANE), lambda b, i: (b, kv_index, 0)),
            pl.BlockSpec((8, GROUP * Q_BLOCK), lambda b, i: (0, 0)),
        ],
        out_specs=pl.BlockSpec((MIX_WIDTH, Q_BLOCK), lambda b, i: (0, b * nq + i)),
        out_shape=jax.ShapeDtypeStruct((MIX_WIDTH, N), out_dtype),
        compiler_params=_cparams(("parallel", "arbitrary")),
        name=name,
    )(qT, k_rows, vT128, sink_lanes)


def _cmp_kernel(xa_ref, xb_ref, pe_ref, w1_ref, w2_ref, o_ref):
    half = (CMP_LEN // 2) * HEAD_DIM
    xa = (xa_ref[...].astype(F32) + pe_ref[0:1, :]).astype(BF16)
    xb = (xb_ref[...].astype(F32) + pe_ref[1:2, :]).astype(BF16)
    hid = (jnp.dot(xa, w1_ref[0:half, :], preferred_element_type=F32)
           + jnp.dot(xb, w1_ref[half:2 * half, :], preferred_element_type=F32))
    hid = jnp.maximum(hid, 0.0).astype(BF16)
    o_ref[...] = jnp.dot(hid, w2_ref[...], preferred_element_type=F32)


def _compress(chunks, pe, w1, w2, name):
    G, nc, half = chunks.shape
    nxt = jnp.concatenate([chunks[:, 1:], jnp.zeros((G, 1, half), chunks.dtype)], axis=1)
    pe2 = jnp.zeros((8, half), F32).at[0:2].set(pe.reshape(2, half))
    return pl.pallas_call(
        _cmp_kernel,
        grid=(G,),
        in_specs=[
            pl.BlockSpec((None, nc, half), lambda g: (g, 0, 0)),
            pl.BlockSpec((None, nc, half), lambda g: (g, 0, 0)),
            pl.BlockSpec((8, half), lambda g: (0, 0)),
            pl.BlockSpec((2 * half, CMP_HIDDEN), lambda g: (0, 0)),
            pl.BlockSpec((CMP_HIDDEN, HEAD_DIM), lambda g: (0, 0)),
        ],
        out_specs=pl.BlockSpec((None, nc, HEAD_DIM), lambda g: (g, 0, 0)),
        out_shape=jax.ShapeDtypeStruct((G, nc, HEAD_DIM), F32),
        compiler_params=_cparams(("parallel",)),
        name=name,
    )(chunks, nxt, pe2, w1.astype(BF16), w2.astype(BF16))


def _nsa_kernel(q_ref, kc_ref, vc_ref, ks_ref, vs_ref, misc_ref, ow_ref, ov_ref, e_ref, o_ref,
                imp_ref, m_ref, l_ref, acc_ref, *, n_top):
    C = KEY_CHUNK
    i = pl.program_id(1)
    nch = i // (C // Q_BLOCK) + 1
    ncmp = kc_ref.shape[0]
    nsel = ov_ref.shape[0]
    lanes = GROUP * Q_BLOCK
    t = i * Q_BLOCK + lax.broadcasted_iota(jnp.int32, (1, Q_BLOCK), 1)
    t4 = i * Q_BLOCK + (lax.broadcasted_iota(jnp.int32, (1, lanes), 1) & (Q_BLOCK - 1))
    row = lax.broadcasted_iota(jnp.int32, (C, Q_BLOCK), 0)
    crow = lax.broadcasted_iota(jnp.int32, (ncmp, lanes), 0)
    jrow = lax.broadcasted_iota(jnp.int32, (nsel, Q_BLOCK), 0)
    cur = t // SLC_LEN
    qpad = _padded_queries(q_ref[...])
    kc = kc_ref[...]
    vcm = vc_ref[...]
    ov = ov_ref[...]

    for kv in range(N_KV_HEADS):
        s = jnp.dot(kc, qpad[kv], preferred_element_type=F32)
        s = jnp.where(crow * CMP_STRIDE + (CMP_LEN - 1) <= t4, s, MASK_VAL)
        m = jnp.maximum(jnp.max(s, axis=0, keepdims=True), M_INIT)
        p = jnp.exp(s - m)
        l = jnp.sum(p, axis=0, keepdims=True)
        p = p * jnp.where(l > 0.0, 1.0 / l, 0.0)
        o_cmp = jnp.dot(vcm, p.astype(BF16), preferred_element_type=F32)[kv * HEAD_DIM:(kv + 1) * HEAD_DIM, :]

        pg = p[:, 0:Q_BLOCK]
        for g in range(1, GROUP):
            pg = pg + p[:, g * Q_BLOCK:(g + 1) * Q_BLOCK]
        p_hi = pg.astype(BF16)
        p_lo = (pg - p_hi.astype(F32)).astype(BF16)
        imp = (jnp.dot(ov, p_hi, preferred_element_type=F32)
               + jnp.dot(ov, p_lo, preferred_element_type=F32))
        forced = (jrow == 0) | (jrow == cur) | (jrow == cur - 1)
        imp = jnp.where(forced, FORCE_SCORE, imp)
        imp = jnp.where(jrow * SLC_LEN <= t, imp, NEG_INF)
        imp_ref[...] = imp

        def rank_body(ii, rank):
            r = imp_ref[pl.ds(ii, 1), :]
            ahead = jnp.where(r > imp, 1.0, jnp.where((r == imp) & (ii < jrow), 1.0, 0.0))
            return rank + ahead
        rank = lax.fori_loop(0, nsel, rank_body, jnp.zeros((nsel, Q_BLOCK), F32))
        selb = jnp.where(rank < float(n_top), 1.0, 0.0).astype(BF16)

        _online_init(m_ref, l_ref, acc_ref)

        def slc_chunk(c, carry):
            off = pl.multiple_of(c * C, C)
            mk = jnp.dot(e_ref[pl.ds(off, C), :], selb, preferred_element_type=F32)
            mk = jnp.where((off + row) <= t, mk, 0.0)
            sel = _lane_tile(mk, GROUP) > 0.5
            ksc = ks_ref[pl.ds(off, C), :]
            vsc = _v_chunk(vs_ref, c * (C // LANE), C // LANE)
            s2 = jnp.dot(ksc, qpad[kv], preferred_element_type=F32)
            _online_step(kv, jnp.where(sel, s2, MASK_VAL), vsc, m_ref, l_ref, acc_ref)
            return carry
        lax.fori_loop(0, nch, slc_chunk, 0)
        o_slc = _online_finish(kv, l_ref, acc_ref)

        for g in range(GROUP):
            hq = kv * GROUP + g
            gr = GATE_ROW0 + hq * 3
            gate = [1.0 / (1.0 + jnp.exp(-misc_ref[gr + br:gr + br + 1, :])) for br in range(3)]
            out = (gate[0] * o_cmp[:, g * Q_BLOCK:(g + 1) * Q_BLOCK]
                   + gate[1] * o_slc[:, g * Q_BLOCK:(g + 1) * Q_BLOCK]
                   + gate[2] * ow_ref[hq * HEAD_DIM:(hq + 1) * HEAD_DIM, :])
            o_ref[hq * HEAD_DIM:(hq + 1) * HEAD_DIM, :] = out.astype(BF16)


def _nsa(qT, kcmp_rows, vcmpT, k_rows, vT128, miscT, o_winT, ovT, emat, B, T):
    nq = T // Q_BLOCK
    N = B * T
    ncmp = T // CMP_STRIDE
    nsel = T // SLC_LEN
    n_top = min(SLC_TOPN, nsel)
    return pl.pallas_call(
        functools.partial(_nsa_kernel, n_top=n_top),
        grid=(B, nq),
        in_specs=[
            pl.BlockSpec((MIX_WIDTH, Q_BLOCK), lambda b, i: (2, b * nq + i)),
            pl.BlockSpec((None, ncmp, KV_WIDTH), lambda b, i: (b, 0, 0)),
            pl.BlockSpec((None, KV_WIDTH, ncmp), lambda b, i: (b, 0, 0)),
            pl.BlockSpec((T, KV_WIDTH), lambda b, i: (b, 3)),
            pl.BlockSpec((nq, KV_WIDTH, LANE), lambda b, i: (b, 3, 0)),
            pl.BlockSpec((MISC_ROWS, Q_BLOCK), lambda b, i: (0, b * nq + i)),
            pl.BlockSpec((MIX_WIDTH, Q_BLOCK), lambda b, i: (0, b * nq + i)),
            pl.BlockSpec((nsel, ncmp), lambda b, i: (0, 0)),
            pl.BlockSpec((T, nsel), lambda b, i: (0, 0)),
        ],
        out_specs=pl.BlockSpec((MIX_WIDTH, Q_BLOCK), lambda b, i: (0, b * nq + i)),
        out_shape=jax.ShapeDtypeStruct((MIX_WIDTH, N), BF16),
        scratch_shapes=[
            pltpu.VMEM((nsel, Q_BLOCK), F32),
            pltpu.VMEM((N_KV_HEADS, 1, GROUP * Q_BLOCK), F32),
            pltpu.VMEM((N_KV_HEADS, 1, GROUP * Q_BLOCK), F32),
            pltpu.VMEM((N_KV_HEADS, KV_WIDTH, GROUP * Q_BLOCK), F32),
        ],
        compiler_params=_cparams(("parallel", "arbitrary")),
        name="nsa_attention",
    )(qT, kcmp_rows, vcmpT, k_rows, vT128, miscT, o_winT, ovT, emat)


def _merge_kernel(x_ref, g_ref, wg_ref, oa_ref, ob_ref, oc_ref, wb_ref, wo_ref, out_ref):
    tm = x_ref.shape[1]
    x = x_ref[...]
    ss = jnp.sum(x * x, axis=0, keepdims=True)
    h = (x * lax.rsqrt(ss * (1.0 / D_MODEL) + EPS) * _lane_tile(g_ref[...], tm // LANE)).astype(BF16)
    acc = None
    for n, o_ref in enumerate((oa_ref, ob_ref, oc_ref)):
        gm = jnp.dot(wg_ref[n * D_MODEL:(n + 1) * D_MODEL, :], h, preferred_element_type=F32)
        y = jnp.dot(wb_ref[n], o_ref[...], preferred_element_type=F32)
        term = y * (1.0 / (1.0 + jnp.exp(-gm)))
        acc = term if acc is None else acc + term
    out_ref[...] = x + jnp.dot(wo_ref[...], acc.astype(BF16), preferred_element_type=F32)


def _merge(xT, g_b, wgT, oa, ob, oc, wbT, woT):
    N = xT.shape[1]
    tm = 256
    col = lambda i: (0, i)
    return pl.pallas_call(
        _merge_kernel,
        grid=(N // tm,),
        in_specs=[
            pl.BlockSpec((D_MODEL, tm), col),
            pl.BlockSpec((D_MODEL, LANE), lambda i: (0, 0)),
            pl.BlockSpec((N_MIXERS * D_MODEL, D_MODEL), lambda i: (0, 0)),
            pl.BlockSpec((MIX_WIDTH, tm), col),
            pl.BlockSpec((MIX_WIDTH, tm), col),
            pl.BlockSpec((MIX_WIDTH, tm), col),
            pl.BlockSpec((N_MIXERS, D_MODEL, MIX_WIDTH), lambda i: (0, 0, 0)),
            pl.BlockSpec((D_MODEL, D_MODEL), lambda i: (0, 0)),
        ],
        out_specs=pl.BlockSpec((D_MODEL, tm), col),
        out_shape=jax.ShapeDtypeStruct((D_MODEL, N), F32),
        compiler_params=_cparams(("parallel",)),
        name="merge_out_projection",
    )(xT, g_b, wgT, oa, ob, oc, wbT, woT)


def _mlp_kernel(x_ref, g_ref, wu_ref, wd_ref, out_ref, h_ref, acc_ref):
    f = pl.program_id(1)
    tm = x_ref.shape[1]

    @pl.when(f == 0)
    def _():
        x = x_ref[...]
        ss = jnp.sum(x * x, axis=0, keepdims=True)
        h_ref[...] = (x * lax.rsqrt(ss * (1.0 / D_MODEL) + EPS)
                      * _lane_tile(g_ref[...], tm // LANE)).astype(BF16)
        acc_ref[...] = jnp.zeros(acc_ref.shape, F32)

    u = jnp.maximum(jnp.dot(wu_ref[...], h_ref[...], preferred_element_type=F32), 0.0)
    acc_ref[...] += jnp.dot(wd_ref[...], (u * u).astype(BF16), preferred_element_type=F32)

    @pl.when(f == pl.num_programs(1) - 1)
    def _():
        out_ref[...] = x_ref[...] + acc_ref[...]


def _mlp(xT, g_b, wuT, wdT):
    N = xT.shape[1]
    tm, tf = 512, 1024
    return pl.pallas_call(
        _mlp_kernel,
        grid=(N // tm, D_FF // tf),
        in_specs=[
            pl.BlockSpec((D_MODEL, tm), lambda i, f: (0, i)),
            pl.BlockSpec((D_MODEL, LANE), lambda i, f: (0, 0)),
            pl.BlockSpec((tf, D_MODEL), lambda i, f: (f, 0)),
            pl.BlockSpec((D_MODEL, tf), lambda i, f: (0, f)),
        ],
        out_specs=pl.BlockSpec((D_MODEL, tm), lambda i, f: (0, i)),
        out_shape=jax.ShapeDtypeStruct((D_MODEL, N), F32),
        scratch_shapes=[pltpu.VMEM((D_MODEL, tm), BF16), pltpu.VMEM((D_MODEL, tm), F32)],
        compiler_params=_cparams(("parallel", "arbitrary")),
        name="relu2_mlp",
    )(xT, g_b, wuT, wdT)


def _lane_bcast(v):
    return jnp.broadcast_to(v.astype(F32)[:, None], (v.shape[0], LANE))


def _to_chunks(rows, B, T):
    half_tokens = CMP_LEN // 2
    t = rows.reshape(B, T, N_KV_HEADS, HEAD_DIM).transpose(0, 2, 1, 3)
    return t.reshape(B * N_KV_HEADS, T // half_tokens, half_tokens * HEAD_DIM)


def kernel(x, norm_mix, w_in, q_norm, k_norm, sinks, cmp_pe_k, cmp_pe_v, w_ck1, w_ck2, w_cv1, w_cv2,
           w_branch, w_out, norm_mlp, w_up, w_down):
    B, T, D = x.shape
    N = B * T
    depth = w_in.shape[0]
    nq = T // Q_BLOCK
    ncmp = T // CMP_STRIDE
    nsel = T // SLC_LEN

    inv_freq = ROPE_THETA ** (-jnp.arange(0, HEAD_DIM, 2, dtype=F32) / HEAD_DIM)
    ang = jnp.arange(T, dtype=F32)[:, None] * inv_freq[None, :]
    cosT, sinT = jnp.cos(ang).T, jnp.sin(ang).T

    ci = np.arange(KEY_CHUNK)
    ltri = jnp.asarray(ci[None, :] < ci[:, None], BF16)
    emat = jnp.asarray((np.arange(T)[:, None] // SLC_LEN) == np.arange(nsel)[None, :], BF16)
    cstart = np.arange(ncmp) * CMP_STRIDE
    sstart = np.arange(nsel) * SLC_LEN
    n_cmp_valid = (T - CMP_LEN) // CMP_STRIDE + 1
    ov = ((cstart[None, :] < sstart[:, None] + SLC_LEN) & (cstart[None, :] + CMP_LEN > sstart[:, None])
          & (np.arange(ncmp)[None, :] < n_cmp_valid))
    ovT = jnp.asarray(ov, BF16)

    xT = x.reshape(N, D).T

    for l in range(depth):
        w = w_in[l]
        sl = lambda s: w[:, s[0]:s[1]]
        wT = jnp.concatenate(
            [sl(S_QA), sl(S_QB), sl(S_QC),
             sl(S_KA), sl(S_KB), sl(S_KC), sl(S_KSL), sl(S_KWN),
             sl(S_VA), sl(S_VB), sl(S_VC), sl(S_VSL), sl(S_VWN),
             sl(S_IQ), sl(S_IK), sl(S_IW), jnp.zeros((D, GATE_ROW0 - IDX_HEADS), w.dtype), sl(S_GC)],
            axis=1).T.astype(BF16)
        wgT = sl(S_GM).T.astype(BF16)
        gq_b = _lane_bcast(jnp.tile(q_norm[l][:, None, :], (1, N_Q_HEADS, 1)).reshape(-1))
        kg = k_norm[l]
        gk_b = _lane_bcast(jnp.tile(jnp.stack([kg[0], kg[1], kg[2], kg[2], kg[2]])[:, None, :],
                                    (1, N_KV_HEADS, 1)).reshape(-1))
        sink_lanes = jnp.zeros((8, GROUP * Q_BLOCK), F32).at[0:N_KV_HEADS].set(
            jnp.repeat(sinks[l].astype(F32).reshape(N_KV_HEADS, GROUP), Q_BLOCK, axis=1))

        qT, kT, vT128, iqT, ikT, miscT = _project(
            xT, _lane_bcast(norm_mix[l]), wT, gq_b, gk_b, cosT, sinT, T)
        k_rows = kT.T
        ik_rows = ikT.T

        o_a = _dsa(iqT, miscT, ik_rows, qT, k_rows, vT128, ltri, B, T)
        o_b = _band(qT, 1, k_rows, 1, vT128, sink_lanes, SWA_WINDOW, True, BF16, B, T, "swa_attention")
        o_w = _band(qT, 2, k_rows, 4, vT128, sink_lanes, NSA_WINDOW, False, F32, B, T, "nsa_window_attention")

        kc_chunks = _to_chunks(k_rows[:, 2 * KV_WIDTH:3 * KV_WIDTH].reshape(B, T, KV_WIDTH), B, T)
        vc_rows = vT128[:, 2 * KV_WIDTH:3 * KV_WIDTH, :].reshape(B, nq, KV_WIDTH, LANE)
        vc_rows = vc_rows.transpose(0, 1, 3, 2).reshape(B, T, KV_WIDTH)
        vc_chunks = _to_chunks(vc_rows, B, T)
        k_cmp = _compress(kc_chunks, cmp_pe_k[l], w_ck1[l], w_ck2[l], "compress_k")
        v_cmp = _compress(vc_chunks, cmp_pe_v[l], w_cv1[l], w_cv2[l], "compress_v")
        kcmp_rows = k_cmp.reshape(B, N_KV_HEADS, ncmp, HEAD_DIM).transpose(0, 2, 1, 3)
        kcmp_rows = kcmp_rows.reshape(B, ncmp, KV_WIDTH).astype(BF16)
        vcmpT = v_cmp.reshape(B, N_KV_HEADS, ncmp, HEAD_DIM).transpose(0, 1, 3, 2)
        vcmpT = vcmpT.reshape(B, KV_WIDTH, ncmp).astype(BF16)

        o_c = _nsa(qT, kcmp_rows, vcmpT, k_rows, vT128, miscT, o_w, ovT, emat, B, T)

        wbT = w_branch[l].transpose(0, 2, 1).astype(BF16)
        xT = _merge(xT, _lane_bcast(norm_mix[l]), wgT, o_a, o_b, o_c, wbT, w_out[l].T.astype(BF16))
        xT = _mlp(xT, _lane_bcast(norm_mlp[l]), w_up[l].T.astype(BF16), w_down[l].T.astype(BF16))

    return xT.T.reshape(B, T, D)
```

```python
import functools

import numpy as np
import jax
import jax.numpy as jnp
from jax import lax
from jax.experimental import pallas as pl
from jax.experimental.pallas import tpu as pltpu

F32 = jnp.float32
BF16 = jnp.bfloat16

D_MODEL = 1024
HEAD_DIM = 64
HALF = HEAD_DIM // 2
N_Q_HEADS = 8
N_KV_HEADS = 2
GROUP = N_Q_HEADS // N_KV_HEADS
MIX_WIDTH = N_Q_HEADS * HEAD_DIM
KV_WIDTH = N_KV_HEADS * HEAD_DIM
N_MIXERS = 3
D_FF = 4 * D_MODEL
ROPE_THETA = 10000.0
EPS = 1e-6
NEG_INF = -1e30
MASK_VAL = -2e30
M_INIT = -1e30
BIG = 3e38
Q_BLOCK = 128
KEY_CHUNK = 512
IDX_HEADS = 4
IDX_DIM = 64
DSA_TOPK_MAX = 256
SWA_WINDOW = 128
CMP_LEN = 32
CMP_STRIDE = 16
CMP_HIDDEN = 256
SLC_LEN = 64
SLC_TOPN = 16
NSA_WINDOW = 512
FORCE_SCORE = 1e9
LOG2E = 1.4426950408889634
Q_SCALE = (HEAD_DIM ** -0.5) * LOG2E
LANE = 128
VMEM_LIMIT = 56 * 1024 * 1024

IN_SIZES = (
    MIX_WIDTH, KV_WIDTH, KV_WIDTH, IDX_HEADS * IDX_DIM, IDX_DIM, IDX_HEADS,
    MIX_WIDTH, KV_WIDTH, KV_WIDTH,
    MIX_WIDTH, KV_WIDTH, KV_WIDTH, KV_WIDTH, KV_WIDTH, KV_WIDTH, KV_WIDTH, 3 * N_Q_HEADS,
    N_MIXERS * D_MODEL,
)
_OFF = np.concatenate([[0], np.cumsum(IN_SIZES)]).astype(int)
(S_QA, S_KA, S_VA, S_IQ, S_IK, S_IW, S_QB, S_KB, S_VB,
 S_QC, S_KC, S_VC, S_KSL, S_VSL, S_KWN, S_VWN, S_GC, S_GM) = [
    (int(_OFF[i]), int(_OFF[i + 1])) for i in range(len(IN_SIZES))]

R_Q = 0
R_K = R_Q + N_MIXERS * MIX_WIDTH
R_V = R_K + 5 * KV_WIDTH
R_IQ = R_V + 5 * KV_WIDTH
R_IK = R_IQ + IDX_HEADS * IDX_DIM
R_MISC = R_IK + IDX_DIM
MISC_ROWS = 32
GATE_ROW0 = 8
R_END = R_MISC + MISC_ROWS
PROJ_TM = 512


def _cparams(sem):
    return pltpu.CompilerParams(dimension_semantics=sem, vmem_limit_bytes=VMEM_LIMIT)


def _lane_tile(a, n):
    return a if n == 1 else jnp.concatenate([a] * n, axis=1)


def _proj_kernel(x_ref, g_ref, w_ref, gq_ref, gk_ref, cos_ref, sin_ref,
                 q_ref, k_ref, v_ref, iq_ref, ik_ref, misc_ref):
    tm = x_ref.shape[1]
    rep = tm // LANE
    x = x_ref[...]
    ss = jnp.sum(x * x, axis=0, keepdims=True)
    h = (x * lax.rsqrt(ss * (1.0 / D_MODEL) + EPS) * _lane_tile(g_ref[...], rep)).astype(BF16)
    cos = cos_ref[...]
    sin = sin_ref[...]

    def rope(z):
        z1, z2 = z[:HALF], z[HALF:]
        return jnp.concatenate([z1 * cos - z2 * sin, z2 * cos + z1 * sin], axis=0)

    def head_norm(z, gain):
        ssq = jnp.sum(z * z, axis=0, keepdims=True)
        return z * lax.rsqrt(ssq * (1.0 / HEAD_DIM) + EPS) * _lane_tile(gain, rep)

    for m in range(N_MIXERS):
        z = jnp.dot(w_ref[R_Q + m * MIX_WIDTH:R_Q + (m + 1) * MIX_WIDTH, :], h,
                    preferred_element_type=F32)
        for hh in range(N_Q_HEADS):
            r0 = m * MIX_WIDTH + hh * HEAD_DIM
            zh = head_norm(z[hh * HEAD_DIM:(hh + 1) * HEAD_DIM], gq_ref[r0:r0 + HEAD_DIM, :])
            q_ref[r0:r0 + HEAD_DIM, :] = (rope(zh) * Q_SCALE).astype(BF16)

    z = jnp.dot(w_ref[R_K:R_V, :], h, preferred_element_type=F32)
    for hh in range(5 * N_KV_HEADS):
        r0 = hh * HEAD_DIM
        zh = head_norm(z[r0:r0 + HEAD_DIM], gk_ref[r0:r0 + HEAD_DIM, :])
        k_ref[r0:r0 + HEAD_DIM, :] = rope(zh).astype(BF16)

    z = jnp.dot(w_ref[R_V:R_IQ, :], h, preferred_element_type=F32)
    for j in range(rep):
        v_ref[j] = z[:, j * LANE:(j + 1) * LANE].astype(BF16)

    z = jnp.dot(w_ref[R_IQ:R_END, :], h, preferred_element_type=F32)
    for hh in range(IDX_HEADS):
        r0 = hh * IDX_DIM
        iq_ref[r0:r0 + IDX_DIM, :] = rope(z[r0:r0 + IDX_DIM])
    ik_ref[...] = rope(z[R_IK - R_IQ:R_MISC - R_IQ])
    misc_ref[...] = z[R_MISC - R_IQ:R_END - R_IQ]


def _project(xT, g_b, wT, gq_b, gk_b, cosT, sinT, T):
    N = xT.shape[1]
    tm = PROJ_TM
    tpb = T // tm
    rep = tm // LANE
    col = lambda i: (0, i)
    const = lambda i: (0, 0)
    return pl.pallas_call(
        _proj_kernel,
        grid=(N // tm,),
        in_specs=[
            pl.BlockSpec((D_MODEL, tm), col),
            pl.BlockSpec((D_MODEL, LANE), const),
            pl.BlockSpec((R_END, D_MODEL), const),
            pl.BlockSpec((N_MIXERS * MIX_WIDTH, LANE), const),
            pl.BlockSpec((5 * KV_WIDTH, LANE), const),
            pl.BlockSpec((HALF, tm), lambda i: (0, i % tpb)),
            pl.BlockSpec((HALF, tm), lambda i: (0, i % tpb)),
        ],
        out_specs=[
            pl.BlockSpec((N_MIXERS * MIX_WIDTH, tm), col),
            pl.BlockSpec((5 * KV_WIDTH, tm), col),
            pl.BlockSpec((rep, 5 * KV_WIDTH, LANE), lambda i: (i, 0, 0)),
            pl.BlockSpec((IDX_HEADS * IDX_DIM, tm), col),
            pl.BlockSpec((IDX_DIM, tm), col),
            pl.BlockSpec((MISC_ROWS, tm), col),
        ],
        out_shape=[
            jax.ShapeDtypeStruct((N_MIXERS * MIX_WIDTH, N), BF16),
            jax.ShapeDtypeStruct((5 * KV_WIDTH, N), BF16),
            jax.ShapeDtypeStruct((N // LANE, 5 * KV_WIDTH, LANE), BF16),
            jax.ShapeDtypeStruct((IDX_HEADS * IDX_DIM, N), F32),
            jax.ShapeDtypeStruct((IDX_DIM, N), F32),
            jax.ShapeDtypeStruct((MISC_ROWS, N), F32),
        ],
        compiler_params=_cparams(("parallel",)),
        name="in_projection",
    )(xT, g_b, wT, gq_b, gk_b, cosT, sinT)


def _padded_queries(q):
    zeros = jnp.zeros((HEAD_DIM, Q_BLOCK), q.dtype)
    out = []
    for kv in range(N_KV_HEADS):
        cols = []
        for g in range(GROUP):
            hq = kv * GROUP + g
            qh = q[hq * HEAD_DIM:(hq + 1) * HEAD_DIM, :]
            cols.append(jnp.concatenate([qh, zeros] if kv == 0 else [zeros, qh], axis=0))
        out.append(jnp.concatenate(cols, axis=1))
    return out


def _v_chunk(v_ref, blk0, nblk):
    return jnp.concatenate([v_ref[blk0 + j] for j in range(nblk)], axis=1)


def _masked_attention(nch, sel_fn, k_ref, v_ref, qall, s_ref, acc_ref):
    C = KEY_CHUNK
    lanes = N_KV_HEADS * GROUP * Q_BLOCK

    def qk_chunk(c, mx):
        off = pl.multiple_of(c * C, C)
        s = jnp.dot(k_ref[pl.ds(off, C), :], qall, preferred_element_type=F32)
        s = s + sel_fn(c, off)
        s_ref[pl.ds(off, C), :] = s
        return jnp.maximum(mx, jnp.max(s, axis=0, keepdims=True))

    mx = lax.fori_loop(0, nch, qk_chunk, jnp.full((1, lanes), M_INIT, F32))
    acc_ref[...] = jnp.zeros(acc_ref.shape, F32)

    def pv_chunk(c, l):
        off = pl.multiple_of(c * C, C)
        p = jnp.exp2(s_ref[pl.ds(off, C), :] - mx)
        vc = _v_chunk(v_ref, c * (C // LANE), C // LANE)
        acc_ref[...] += jnp.dot(vc, p.astype(BF16), preferred_element_type=F32)
        return l + jnp.sum(p, axis=0, keepdims=True)

    l = lax.fori_loop(0, nch, pv_chunk, jnp.zeros((1, lanes), F32))
    inv = jnp.where(l > 0.0, 1.0 / l, 0.0)
    half = GROUP * Q_BLOCK
    return [acc_ref[kv * HEAD_DIM:(kv + 1) * HEAD_DIM, kv * half:(kv + 1) * half]
            * inv[:, kv * half:(kv + 1) * half] for kv in range(N_KV_HEADS)]


def _dsa_kernel(iq_ref, misc_ref, ik_ref, q_ref, k_ref, v_ref, ltri_ref, o_ref,
                sc_ref, s_ref, acc_ref, *, k_top):
    C = KEY_CHUNK
    i = pl.program_id(1)
    nch = i // (C // Q_BLOCK) + 1
    qpos = i * Q_BLOCK + lax.broadcasted_iota(jnp.int32, (1, Q_BLOCK), 1)
    row = lax.broadcasted_iota(jnp.int32, (C, Q_BLOCK), 0)

    iq = iq_ref[...].astype(BF16)
    iq_cat = jnp.concatenate([iq[h * IDX_DIM:(h + 1) * IDX_DIM] for h in range(IDX_HEADS)], axis=1)
    w = misc_ref[0:IDX_HEADS, :] * ((IDX_HEADS ** -0.5) * (IDX_DIM ** -0.5))

    def score_chunk(c, carry):
        lo, hi = carry
        off = pl.multiple_of(c * C, C)
        ikc = ik_ref[pl.ds(off, C), :].astype(BF16)
        d = jnp.dot(ikc, iq_cat, preferred_element_type=F32)
        s = jnp.maximum(d[:, 0:Q_BLOCK], 0.0) * w[0:1, :]
        for h in range(1, IDX_HEADS):
            s = s + jnp.maximum(d[:, h * Q_BLOCK:(h + 1) * Q_BLOCK], 0.0) * w[h:h + 1, :]
        causal = (off + row) <= qpos
        sc_ref[pl.ds(off, C), :] = jnp.where(causal, s, NEG_INF)
        lo = jnp.minimum(lo, jnp.min(jnp.where(causal, s, BIG), axis=0, keepdims=True))
        hi = jnp.maximum(hi, jnp.max(jnp.where(causal, s, -BIG), axis=0, keepdims=True))
        return lo, hi

    lo0, hi0 = lax.fori_loop(0, nch, score_chunk,
                             (jnp.full((1, Q_BLOCK), BIG, F32), jnp.full((1, Q_BLOCK), -BIG, F32)))

    kf = float(k_top)
    select = i * Q_BLOCK >= k_top

    def stats(mid):
        def body(c, carry):
            cnt, a, b = carry
            s = sc_ref[pl.ds(pl.multiple_of(c * C, C), C), :]
            gt = s > mid
            cnt = cnt + jnp.sum(jnp.where(gt, 1.0, 0.0), axis=0, keepdims=True)
            a = jnp.minimum(a, jnp.min(jnp.where(gt, s, BIG), axis=0, keepdims=True))
            b = jnp.maximum(b, jnp.max(jnp.where(gt, -BIG, s), axis=0, keepdims=True))
            return cnt, a, b
        z = jnp.zeros((1, Q_BLOCK), F32)
        return lax.fori_loop(0, nch, body, (z, z + BIG, z - BIG))

    def cond(st):
        lo, hi, _, _, _ = st
        return jnp.logical_and(select, jnp.max(hi - lo) > 0.0)

    def body(st):
        lo, hi, clo, chi, it = st
        interp = (clo - kf + 0.5) / (clo - chi)
        use_half = (it % 2).astype(F32)
        frac = interp * (1.0 - use_half) + 0.5 * use_half
        mid = jnp.maximum(lo + (hi - lo) * frac, lo)
        mid = jnp.where(mid >= hi, lo, mid)
        cnt, a, b = stats(mid)
        up = cnt >= kf
        return (jnp.where(up, a, lo), jnp.where(up, hi, b),
                jnp.where(up, cnt, clo), jnp.where(up, chi, cnt), it + 1)

    n_causal = (qpos + 1).astype(F32)
    lo, _, clo, chi, _ = lax.while_loop(
        cond, body, (lo0, hi0, n_causal, jnp.zeros((1, Q_BLOCK), F32), jnp.int32(0)))
    thr = jnp.where(select, lo, 0.5 * NEG_INF)
    need = jnp.where(select, kf - chi, 0.0)
    ties = jnp.logical_and(select, jnp.max((clo - chi) - need) > 0.0)

    @pl.when(ties)
    def _():
        def mask_chunk(c, tie_carry):
            off = pl.multiple_of(c * C, C)
            s_idx = sc_ref[pl.ds(off, C), :]
            eqf = jnp.where(s_idx == thr, 1.0, 0.0)
            before = jnp.dot(ltri_ref[...], eqf.astype(BF16), preferred_element_type=F32) + tie_carry
            sc_ref[pl.ds(off, C), :] = jnp.where(s_idx > thr, 1.0, jnp.where(before < need, eqf, 0.0))
            return tie_carry + jnp.sum(eqf, axis=0, keepdims=True)

        lax.fori_loop(0, nch, mask_chunk, jnp.zeros((1, Q_BLOCK), F32))

    thr_sel = jnp.where(ties, 0.5, thr)
    qall = jnp.concatenate(_padded_queries(q_ref[...]), axis=1)

    def sel_fn(c, off):
        bias = jnp.where(sc_ref[pl.ds(off, C), :] >= thr_sel, 0.0, MASK_VAL)
        return _lane_tile(bias, N_KV_HEADS * GROUP)

    outs = _masked_attention(nch, sel_fn, k_ref, v_ref, qall, s_ref, acc_ref)
    for kv in range(N_KV_HEADS):
        o = outs[kv]
        for g in range(GROUP):
            hq = kv * GROUP + g
            o_ref[hq * HEAD_DIM:(hq + 1) * HEAD_DIM, :] = o[:, g * Q_BLOCK:(g + 1) * Q_BLOCK].astype(BF16)


def _dsa(iqT, miscT, ik_rows, qT, k_rows, vT128, ltri, B, T):
    nq = T // Q_BLOCK
    k_top = min(DSA_TOPK_MAX, T // 4)
    N = B * T
    return pl.pallas_call(
        functools.partial(_dsa_kernel, k_top=k_top),
        grid=(B, nq),
        in_specs=[
            pl.BlockSpec((IDX_HEADS * IDX_DIM, Q_BLOCK), lambda b, i: (0, b * nq + i)),
            pl.BlockSpec((MISC_ROWS, Q_BLOCK), lambda b, i: (0, b * nq + i)),
            pl.BlockSpec((T, IDX_DIM), lambda b, i: (b, 0)),
            pl.BlockSpec((MIX_WIDTH, Q_BLOCK), lambda b, i: (0, b * nq + i)),
            pl.BlockSpec((T, KV_WIDTH), lambda b, i: (b, 0)),
            pl.BlockSpec((nq, KV_WIDTH, LANE), lambda b, i: (b, 0, 0)),
            pl.BlockSpec((KEY_CHUNK, KEY_CHUNK), lambda b, i: (0, 0)),
        ],
        out_specs=pl.BlockSpec((MIX_WIDTH, Q_BLOCK), lambda b, i: (0, b * nq + i)),
        out_shape=jax.ShapeDtypeStruct((MIX_WIDTH, N), BF16),
        scratch_shapes=[
            pltpu.VMEM((T, Q_BLOCK), F32),
            pltpu.VMEM((T, N_KV_HEADS * GROUP * Q_BLOCK), F32),
            pltpu.VMEM((KV_WIDTH, N_KV_HEADS * GROUP * Q_BLOCK), F32),
        ],
        compiler_params=_cparams(("parallel", "arbitrary")),
        name="dsa_attention",
    )(iqT, miscT, ik_rows, qT, k_rows, vT128, ltri)


def _band_kernel(q_ref, k_ref, v_ref, sink_ref, o_ref, *, n_prev, window, use_sink):
    i = pl.program_id(1)
    blk0 = jnp.maximum(i - n_prev, 0)
    nblk = n_prev + 1
    wb = nblk * Q_BLOCK
    kb = k_ref[pl.ds(pl.multiple_of(blk0 * Q_BLOCK, Q_BLOCK), wb), :]
    vb = _v_chunk(v_ref, blk0, nblk)
    qpad = _padded_queries(q_ref[...])
    lanes = GROUP * Q_BLOCK
    kpos = blk0 * Q_BLOCK + lax.broadcasted_iota(jnp.int32, (wb, lanes), 0)
    qpos = i * Q_BLOCK + (lax.broadcasted_iota(jnp.int32, (wb, lanes), 1) & (Q_BLOCK - 1))
    rel = qpos - kpos
    valid = (rel >= 0) & (rel < window)
    for kv in range(N_KV_HEADS):
        s = jnp.where(valid, jnp.dot(kb, qpad[kv], preferred_element_type=F32), MASK_VAL)
        m = jnp.max(s, axis=0, keepdims=True)
        if use_sink:
            sink = sink_ref[kv:kv + 1, :]
            m = jnp.maximum(m, sink)
        p = jnp.exp2(s - m)
        l = jnp.sum(p, axis=0, keepdims=True)
        if use_sink:
            l = l + jnp.exp2(sink - m)
        o = jnp.dot(vb, p.astype(BF16), preferred_element_type=F32)[kv * HEAD_DIM:(kv + 1) * HEAD_DIM, :]
        o = o * (1.0 / l)
        for g in range(GROUP):
            hq = kv * GROUP + g
            o_ref[hq * HEAD_DIM:(hq + 1) * HEAD_DIM, :] = o[:, g * Q_BLOCK:(g + 1) * Q_BLOCK].astype(o_ref.dtype)


def _band(qT, q_mixer, k_rows, kv_index, vT128, sink_lanes, window, use_sink, out_dtype, B, T, name):
    nq = T // Q_BLOCK
    N = B * T
    n_prev = (window + Q_BLOCK - 2) // Q_BLOCK
    return pl.pallas_call(
        functools.partial(_band_kernel, n_prev=n_prev, window=window, use_sink=use_sink),
        grid=(B, nq),
        in_specs=[
            pl.BlockSpec((MIX_WIDTH, Q_BLOCK), lambda b, i: (q_mixer, b * nq + i)),
            pl.BlockSpec((T, KV_WIDTH), lambda b, i: (b, kv_index)),
            pl.BlockSpec((nq, KV_WIDTH, LANE), lambda b, i: (b, kv_index, 0)),
            pl.BlockSpec((8, GROUP * Q_BLOCK), lambda b, i: (0, 0)),
        ],
        out_specs=pl.BlockSpec((MIX_WIDTH, Q_BLOCK), lambda b, i: (0, b * nq + i)),
        out_shape=jax.ShapeDtypeStruct((MIX_WIDTH, N), out_dtype),
        compiler_params=_cparams(("parallel", "arbitrary")),
        name=name,
    )(qT, k_rows, vT128, sink_lanes)


def _cmp_kernel(xa_ref, xb_ref, pe_ref, w1_ref, w2_ref, o_ref):
    half = (CMP_LEN // 2) * HEAD_DIM
    xa = (xa_ref[...].astype(F32) + pe_ref[0:1, :]).astype(BF16)
    xb = (xb_ref[...].astype(F32) + pe_ref[1:2, :]).astype(BF16)
    hid = (jnp.dot(xa, w1_ref[0:half, :], preferred_element_type=F32)
           + jnp.dot(xb, w1_ref[half:2 * half, :], preferred_element_type=F32))
    hid = jnp.maximum(hid, 0.0).astype(BF16)
    o_ref[...] = jnp.dot(hid, w2_ref[...], preferred_element_type=F32)


def _compress(chunks, pe, w1, w2, name):
    G, nc, half = chunks.shape
    nxt = jnp.concatenate([chunks[:, 1:], jnp.zeros((G, 1, half), chunks.dtype)], axis=1)
    pe2 = jnp.zeros((8, half), F32).at[0:2].set(pe.reshape(2, half))
    return pl.pallas_call(
        _cmp_kernel,
        grid=(G,),
        in_specs=[
            pl.BlockSpec((None, nc, half), lambda g: (g, 0, 0)),
            pl.BlockSpec((None, nc, half), lambda g: (g, 0, 0)),
            pl.BlockSpec((8, half), lambda g: (0, 0)),
            pl.BlockSpec((2 * half, CMP_HIDDEN), lambda g: (0, 0)),
            pl.BlockSpec((CMP_HIDDEN, HEAD_DIM), lambda g: (0, 0)),
        ],
        out_specs=pl.BlockSpec((None, nc, HEAD_DIM), lambda g: (g, 0, 0)),
        out_shape=jax.ShapeDtypeStruct((G, nc, HEAD_DIM), F32),
        compiler_params=_cparams(("parallel",)),
        name=name,
    )(chunks, nxt, pe2, w1.astype(BF16), w2.astype(BF16))


def _nsa_kernel(q_ref, kc_ref, vc_ref, ks_ref, vs_ref, misc_ref, ow_ref, ov_ref, e_ref, o_ref,
                imp_ref, s_ref, acc_ref, *, n_top):
    C = KEY_CHUNK
    i = pl.program_id(1)
    nch = i // (C // Q_BLOCK) + 1
    ncmp = kc_ref.shape[0]
    nsel = ov_ref.shape[0]
    lanes = GROUP * Q_BLOCK
    t = i * Q_BLOCK + lax.broadcasted_iota(jnp.int32, (1, Q_BLOCK), 1)
    t4 = i * Q_BLOCK + (lax.broadcasted_iota(jnp.int32, (1, lanes), 1) & (Q_BLOCK - 1))
    row = lax.broadcasted_iota(jnp.int32, (C, Q_BLOCK), 0)
    crow = lax.broadcasted_iota(jnp.int32, (ncmp, lanes), 0)
    jrow = lax.broadcasted_iota(jnp.int32, (nsel, Q_BLOCK), 0)
    cur = t // SLC_LEN
    qpad = _padded_queries(q_ref[...])
    kc = kc_ref[...]
    vcm = vc_ref[...]
    ov = ov_ref[...]

    o_cmps, selbs = [], []
    for kv in range(N_KV_HEADS):
        s = jnp.dot(kc, qpad[kv], preferred_element_type=F32)
        s = jnp.where(crow * CMP_STRIDE + (CMP_LEN - 1) <= t4, s, MASK_VAL)
        m = jnp.maximum(jnp.max(s, axis=0, keepdims=True), M_INIT)
        p = jnp.exp2(s - m)
        l = jnp.sum(p, axis=0, keepdims=True)
        p = p * jnp.where(l > 0.0, 1.0 / l, 0.0)
        o_cmp = jnp.dot(vcm, p.astype(BF16), preferred_element_type=F32)[kv * HEAD_DIM:(kv + 1) * HEAD_DIM, :]

        pg = p[:, 0:Q_BLOCK]
        for g in range(1, GROUP):
            pg = pg + p[:, g * Q_BLOCK:(g + 1) * Q_BLOCK]
        p_hi = pg.astype(BF16)
        p_lo = (pg - p_hi.astype(F32)).astype(BF16)
        imp = (jnp.dot(ov, p_hi, preferred_element_type=F32)
               + jnp.dot(ov, p_lo, preferred_element_type=F32))
        forced = (jrow == 0) | (jrow == cur) | (jrow == cur - 1)
        imp = jnp.where(forced, FORCE_SCORE, imp)
        imp = jnp.where(jrow * SLC_LEN <= t, imp, NEG_INF)
        imp_ref[...] = imp

        def rank_body(ii, rank):
            r = imp_ref[pl.ds(ii, 1), :]
            ahead = jnp.where(r > imp, 1.0, jnp.where((r == imp) & (ii < jrow), 1.0, 0.0))
            return rank + ahead
        rank = lax.fori_loop(0, nsel, rank_body, jnp.zeros((nsel, Q_BLOCK), F32), unroll=8)
        selbs.append(jnp.where(rank < float(n_top), 1.0, 0.0).astype(BF16))
        o_cmps.append(o_cmp)

    qall = jnp.concatenate(qpad, axis=1)
    selb_all = jnp.concatenate(selbs, axis=1)

    def sel_fn(c, off):
        mk = jnp.dot(e_ref[pl.ds(off, C), :], selb_all, preferred_element_type=F32)
        causal = (off + row) <= t
        parts = []
        for kv in range(N_KV_HEADS):
            keep = jnp.where(causal, mk[:, kv * Q_BLOCK:(kv + 1) * Q_BLOCK], 0.0) > 0.5
            parts.append(_lane_tile(jnp.where(keep, 0.0, MASK_VAL), GROUP))
        return jnp.concatenate(parts, axis=1)

    o_slcs = _masked_attention(nch, sel_fn, ks_ref, vs_ref, qall, s_ref, acc_ref)

    for kv in range(N_KV_HEADS):
        for g in range(GROUP):
            hq = kv * GROUP + g
            gr = GATE_ROW0 + hq * 3
            gate = [1.0 / (1.0 + jnp.exp(-misc_ref[gr + br:gr + br + 1, :])) for br in range(3)]
            out = (gate[0] * o_cmps[kv][:, g * Q_BLOCK:(g + 1) * Q_BLOCK]
                   + gate[1] * o_slcs[kv][:, g * Q_BLOCK:(g + 1) * Q_BLOCK]
                   + gate[2] * ow_ref[hq * HEAD_DIM:(hq + 1) * HEAD_DIM, :])
            o_ref[hq * HEAD_DIM:(hq + 1) * HEAD_DIM, :] = out.astype(BF16)


def _nsa(qT, kcmp_rows, vcmpT, k_rows, vT128, miscT, o_winT, ovT, emat, B, T):
    nq = T // Q_BLOCK
    N = B * T
    ncmp = T // CMP_STRIDE
    nsel = T // SLC_LEN
    n_top = min(SLC_TOPN, nsel)
    return pl.pallas_call(
        functools.partial(_nsa_kernel, n_top=n_top),
        grid=(B, nq),
        in_specs=[
            pl.BlockSpec((MIX_WIDTH, Q_BLOCK), lambda b, i: (2, b * nq + i)),
            pl.BlockSpec((None, ncmp, KV_WIDTH), lambda b, i: (b, 0, 0)),
            pl.BlockSpec((None, KV_WIDTH, ncmp), lambda b, i: (b, 0, 0)),
            pl.BlockSpec((T, KV_WIDTH), lambda b, i: (b, 3)),
            pl.BlockSpec((nq, KV_WIDTH, LANE), lambda b, i: (b, 3, 0)),
            pl.BlockSpec((MISC_ROWS, Q_BLOCK), lambda b, i: (0, b * nq + i)),
            pl.BlockSpec((MIX_WIDTH, Q_BLOCK), lambda b, i: (0, b * nq + i)),
            pl.BlockSpec((nsel, ncmp), lambda b, i: (0, 0)),
            pl.BlockSpec((T, nsel), lambda b, i: (0, 0)),
        ],
        out_specs=pl.BlockSpec((MIX_WIDTH, Q_BLOCK), lambda b, i: (0, b * nq + i)),
        out_shape=jax.ShapeDtypeStruct((MIX_WIDTH, N), BF16),
        scratch_shapes=[
            pltpu.VMEM((nsel, Q_BLOCK), F32),
            pltpu.VMEM((T, N_KV_HEADS * GROUP * Q_BLOCK), F32),
            pltpu.VMEM((KV_WIDTH, N_KV_HEADS * GROUP * Q_BLOCK), F32),
        ],
        compiler_params=_cparams(("parallel", "arbitrary")),
        name="nsa_attention",
    )(qT, kcmp_rows, vcmpT, k_rows, vT128, miscT, o_winT, ovT, emat)


def _merge_kernel(x_ref, g_ref, wg_ref, oa_ref, ob_ref, oc_ref, wb_ref, wo_ref, out_ref):
    tm = x_ref.shape[1]
    x = x_ref[...]
    ss = jnp.sum(x * x, axis=0, keepdims=True)
    h = (x * lax.rsqrt(ss * (1.0 / D_MODEL) + EPS) * _lane_tile(g_ref[...], tm // LANE)).astype(BF16)
    acc = None
    for n, o_ref in enumerate((oa_ref, ob_ref, oc_ref)):
        gm = jnp.dot(wg_ref[n * D_MODEL:(n + 1) * D_MODEL, :], h, preferred_element_type=F32)
        y = jnp.dot(wb_ref[n], o_ref[...], preferred_element_type=F32)
        term = y * (1.0 / (1.0 + jnp.exp(-gm)))
        acc = term if acc is None else acc + term
    out_ref[...] = x + jnp.dot(wo_ref[...], acc.astype(BF16), preferred_element_type=F32)


def _merge(xT, g_b, wgT, oa, ob, oc, wbT, woT):
    N = xT.shape[1]
    tm = 256
    col = lambda i: (0, i)
    return pl.pallas_call(
        _merge_kernel,
        grid=(N // tm,),
        in_specs=[
            pl.BlockSpec((D_MODEL, tm), col),
            pl.BlockSpec((D_MODEL, LANE), lambda i: (0, 0)),
            pl.BlockSpec((N_MIXERS * D_MODEL, D_MODEL), lambda i: (0, 0)),
            pl.BlockSpec((MIX_WIDTH, tm), col),
            pl.BlockSpec((MIX_WIDTH, tm), col),
            pl.BlockSpec((MIX_WIDTH, tm), col),
            pl.BlockSpec((N_MIXERS, D_MODEL, MIX_WIDTH), lambda i: (0, 0, 0)),
            pl.BlockSpec((D_MODEL, D_MODEL), lambda i: (0, 0)),
        ],
        out_specs=pl.BlockSpec((D_MODEL, tm), col),
        out_shape=jax.ShapeDtypeStruct((D_MODEL, N), F32),
        compiler_params=_cparams(("parallel",)),
        name="merge_out_projection",
    )(xT, g_b, wgT, oa, ob, oc, wbT, woT)


def _mlp_kernel(x_ref, g_ref, wu_ref, wd_ref, out_ref, h_ref, acc_ref):
    f = pl.program_id(1)
    tm = x_ref.shape[1]

    @pl.when(f == 0)
    def _():
        x = x_ref[...]
        ss = jnp.sum(x * x, axis=0, keepdims=True)
        h_ref[...] = (x * lax.rsqrt(ss * (1.0 / D_MODEL) + EPS)
                      * _lane_tile(g_ref[...], tm // LANE)).astype(BF16)
        acc_ref[...] = jnp.zeros(acc_ref.shape, F32)

    u = jnp.maximum(jnp.dot(wu_ref[...], h_ref[...], preferred_element_type=F32), 0.0)
    acc_ref[...] += jnp.dot(wd_ref[...], (u * u).astype(BF16), preferred_element_type=F32)

    @pl.when(f == pl.num_programs(1) - 1)
    def _():
        out_ref[...] = x_ref[...] + acc_ref[...]


def _mlp(xT, g_b, wuT, wdT):
    N = xT.shape[1]
    tm, tf = 512, 1024
    return pl.pallas_call(
        _mlp_kernel,
        grid=(N // tm, D_FF // tf),
        in_specs=[
            pl.BlockSpec((D_MODEL, tm), lambda i, f: (0, i)),
            pl.BlockSpec((D_MODEL, LANE), lambda i, f: (0, 0)),
            pl.BlockSpec((tf, D_MODEL), lambda i, f: (f, 0)),
            pl.BlockSpec((D_MODEL, tf), lambda i, f: (0, f)),
        ],
        out_specs=pl.BlockSpec((D_MODEL, tm), lambda i, f: (0, i)),
        out_shape=jax.ShapeDtypeStruct((D_MODEL, N), F32),
        scratch_shapes=[pltpu.VMEM((D_MODEL, tm), BF16), pltpu.VMEM((D_MODEL, tm), F32)],
        compiler_params=_cparams(("parallel", "arbitrary")),
        name="relu2_mlp",
    )(xT, g_b, wuT, wdT)


def _lane_bcast(v):
    return jnp.broadcast_to(v.astype(F32)[:, None], (v.shape[0], LANE))


def _to_chunks(rows, B, T):
    half_tokens = CMP_LEN // 2
    t = rows.reshape(B, T, N_KV_HEADS, HEAD_DIM).transpose(0, 2, 1, 3)
    return t.reshape(B * N_KV_HEADS, T // half_tokens, half_tokens * HEAD_DIM)


def kernel(x, norm_mix, w_in, q_norm, k_norm, sinks, cmp_pe_k, cmp_pe_v, w_ck1, w_ck2, w_cv1, w_cv2,
           w_branch, w_out, norm_mlp, w_up, w_down):
    B, T, D = x.shape
    N = B * T
    depth = w_in.shape[0]
    nq = T // Q_BLOCK
    ncmp = T // CMP_STRIDE
    nsel = T // SLC_LEN

    inv_freq = ROPE_THETA ** (-jnp.arange(0, HEAD_DIM, 2, dtype=F32) / HEAD_DIM)
    ang = jnp.arange(T, dtype=F32)[:, None] * inv_freq[None, :]
    cosT, sinT = jnp.cos(ang).T, jnp.sin(ang).T

    ci = np.arange(KEY_CHUNK)
    ltri = jnp.asarray(ci[None, :] < ci[:, None], BF16)
    emat = jnp.asarray((np.arange(T)[:, None] // SLC_LEN) == np.arange(nsel)[None, :], BF16)
    cstart = np.arange(ncmp) * CMP_STRIDE
    sstart = np.arange(nsel) * SLC_LEN
    n_cmp_valid = (T - CMP_LEN) // CMP_STRIDE + 1
    ov = ((cstart[None, :] < sstart[:, None] + SLC_LEN) & (cstart[None, :] + CMP_LEN > sstart[:, None])
          & (np.arange(ncmp)[None, :] < n_cmp_valid))
    ovT = jnp.asarray(ov, BF16)

    xT = x.reshape(N, D).T

    for l in range(depth):
        w = w_in[l]
        sl = lambda s: w[:, s[0]:s[1]]
        wT = jnp.concatenate(
            [sl(S_QA), sl(S_QB), sl(S_QC),
             sl(S_KA), sl(S_KB), sl(S_KC), sl(S_KSL), sl(S_KWN),
             sl(S_VA), sl(S_VB), sl(S_VC), sl(S_VSL), sl(S_VWN),
             sl(S_IQ), sl(S_IK), sl(S_IW), jnp.zeros((D, GATE_ROW0 - IDX_HEADS), w.dtype), sl(S_GC)],
            axis=1).T.astype(BF16)
        wgT = sl(S_GM).T.astype(BF16)
        gq_b = _lane_bcast(jnp.tile(q_norm[l][:, None, :], (1, N_Q_HEADS, 1)).reshape(-1))
        kg = k_norm[l]
        gk_b = _lane_bcast(jnp.tile(jnp.stack([kg[0], kg[1], kg[2], kg[2], kg[2]])[:, None, :],
                                    (1, N_KV_HEADS, 1)).reshape(-1))
        sink_lanes = jnp.zeros((8, GROUP * Q_BLOCK), F32).at[0:N_KV_HEADS].set(
            jnp.repeat(sinks[l].astype(F32).reshape(N_KV_HEADS, GROUP) * LOG2E, Q_BLOCK, axis=1))

        qT, kT, vT128, iqT, ikT, miscT = _project(
            xT, _lane_bcast(norm_mix[l]), wT, gq_b, gk_b, cosT, sinT, T)
        k_rows = kT.T
        ik_rows = ikT.T

        o_a = _dsa(iqT, miscT, ik_rows, qT, k_rows, vT128, ltri, B, T)
        o_b = _band(qT, 1, k_rows, 1, vT128, sink_lanes, SWA_WINDOW, True, BF16, B, T, "swa_attention")
        o_w = _band(qT, 2, k_rows, 4, vT128, sink_lanes, NSA_WINDOW, False, F32, B, T, "nsa_window_attention")

        kc_chunks = _to_chunks(k_rows[:, 2 * KV_WIDTH:3 * KV_WIDTH].reshape(B, T, KV_WIDTH), B, T)
        vc_rows = vT128[:, 2 * KV_WIDTH:3 * KV_WIDTH, :].reshape(B, nq, KV_WIDTH, LANE)
        vc_rows = vc_rows.transpose(0, 1, 3, 2).reshape(B, T, KV_WIDTH)
        vc_chunks = _to_chunks(vc_rows, B, T)
        k_cmp = _compress(kc_chunks, cmp_pe_k[l], w_ck1[l], w_ck2[l], "compress_k")
        v_cmp = _compress(vc_chunks, cmp_pe_v[l], w_cv1[l], w_cv2[l], "compress_v")
        kcmp_rows = k_cmp.reshape(B, N_KV_HEADS, ncmp, HEAD_DIM).transpose(0, 2, 1, 3)
        kcmp_rows = kcmp_rows.reshape(B, ncmp, KV_WIDTH).astype(BF16)
        vcmpT = v_cmp.reshape(B, N_KV_HEADS, ncmp, HEAD_DIM).transpose(0, 1, 3, 2)
        vcmpT = vcmpT.reshape(B, KV_WIDTH, ncmp).astype(BF16)

        o_c = _nsa(qT, kcmp_rows, vcmpT, k_rows, vT128, miscT, o_w, ovT, emat, B, T)

        wbT = w_branch[l].transpose(0, 2, 1).astype(BF16)
        xT = _merge(xT, _lane_bcast(norm_mix[l]), wgT, o_a, o_b, o_c, wbT, w_out[l].T.astype(BF16))
        xT = _mlp(xT, _lane_bcast(norm_mlp[l]), w_up[l].T.astype(BF16), w_down[l].T.astype(BF16))

    return xT.T.reshape(B, T, D)
```

```python
import functools

import numpy as np
import jax
import jax.numpy as jnp
from jax import lax
from jax.experimental import pallas as pl
from jax.experimental.pallas import tpu as pltpu

F32 = jnp.float32
BF16 = jnp.bfloat16

D_MODEL = 1024
HEAD_DIM = 64
HALF = HEAD_DIM // 2
N_Q_HEADS = 8
N_KV_HEADS = 2
GROUP = N_Q_HEADS // N_KV_HEADS
MIX_WIDTH = N_Q_HEADS * HEAD_DIM
KV_WIDTH = N_KV_HEADS * HEAD_DIM
N_MIXERS = 3
D_FF = 4 * D_MODEL
ROPE_THETA = 10000.0
EPS = 1e-6
NEG_INF = -1e30
MASK_VAL = -2e30
M_INIT = -1e30
BIG = 3e38
Q_BLOCK = 128
KEY_CHUNK = 512
BISECT_FIXED_STEPS = 12
ONES_ROWS = 16
IDX_HEADS = 4
IDX_DIM = 64
DSA_TOPK_MAX = 256
SWA_WINDOW = 128
CMP_LEN = 32
CMP_STRIDE = 16
CMP_HIDDEN = 256
SLC_LEN = 64
SLC_TOPN = 16
NSA_WINDOW = 512
FORCE_SCORE = 1e9
LOG2E = 1.4426950408889634
Q_SCALE = (HEAD_DIM ** -0.5) * LOG2E
LANE = 128
VMEM_LIMIT = 56 * 1024 * 1024

IN_SIZES = (
    MIX_WIDTH, KV_WIDTH, KV_WIDTH, IDX_HEADS * IDX_DIM, IDX_DIM, IDX_HEADS,
    MIX_WIDTH, KV_WIDTH, KV_WIDTH,
    MIX_WIDTH, KV_WIDTH, KV_WIDTH, KV_WIDTH, KV_WIDTH, KV_WIDTH, KV_WIDTH, 3 * N_Q_HEADS,
    N_MIXERS * D_MODEL,
)
_OFF = np.concatenate([[0], np.cumsum(IN_SIZES)]).astype(int)
(S_QA, S_KA, S_VA, S_IQ, S_IK, S_IW, S_QB, S_KB, S_VB,
 S_QC, S_KC, S_VC, S_KSL, S_VSL, S_KWN, S_VWN, S_GC, S_GM) = [
    (int(_OFF[i]), int(_OFF[i + 1])) for i in range(len(IN_SIZES))]

R_Q = 0
R_K = R_Q + N_MIXERS * MIX_WIDTH
R_V = R_K + 5 * KV_WIDTH
R_IQ = R_V + 5 * KV_WIDTH
R_IK = R_IQ + IDX_HEADS * IDX_DIM
R_MISC = R_IK + IDX_DIM
MISC_ROWS = 32
GATE_ROW0 = 8
R_END = R_MISC + MISC_ROWS
PROJ_TM = 512


def _cparams(sem):
    return pltpu.CompilerParams(dimension_semantics=sem, vmem_limit_bytes=VMEM_LIMIT)


def _lane_tile(a, n):
    return a if n == 1 else jnp.concatenate([a] * n, axis=1)


def _proj_kernel(x_ref, g_ref, w_ref, gq_ref, gk_ref, cos_ref, sin_ref,
                 q_ref, k_ref, v_ref, iq_ref, ik_ref, misc_ref):
    tm = x_ref.shape[1]
    rep = tm // LANE
    x = x_ref[...]
    ss = jnp.sum(x * x, axis=0, keepdims=True)
    h = (x * lax.rsqrt(ss * (1.0 / D_MODEL) + EPS) * _lane_tile(g_ref[...], rep)).astype(BF16)
    cos = cos_ref[...]
    sin = sin_ref[...]

    def rope(z):
        z1, z2 = z[:HALF], z[HALF:]
        return jnp.concatenate([z1 * cos - z2 * sin, z2 * cos + z1 * sin], axis=0)

    def head_norm(z, gain):
        ssq = jnp.sum(z * z, axis=0, keepdims=True)
        return z * lax.rsqrt(ssq * (1.0 / HEAD_DIM) + EPS) * _lane_tile(gain, rep)

    for m in range(N_MIXERS):
        z = jnp.dot(w_ref[R_Q + m * MIX_WIDTH:R_Q + (m + 1) * MIX_WIDTH, :], h,
                    preferred_element_type=F32)
        for hh in range(N_Q_HEADS):
            r0 = m * MIX_WIDTH + hh * HEAD_DIM
            zh = head_norm(z[hh * HEAD_DIM:(hh + 1) * HEAD_DIM], gq_ref[r0:r0 + HEAD_DIM, :])
            q_ref[r0:r0 + HEAD_DIM, :] = (rope(zh) * Q_SCALE).astype(BF16)

    z = jnp.dot(w_ref[R_K:R_V, :], h, preferred_element_type=F32)
    for hh in range(5 * N_KV_HEADS):
        r0 = hh * HEAD_DIM
        zh = head_norm(z[r0:r0 + HEAD_DIM], gk_ref[r0:r0 + HEAD_DIM, :])
        k_ref[r0:r0 + HEAD_DIM, :] = rope(zh).astype(BF16)

    z = jnp.dot(w_ref[R_V:R_IQ, :], h, preferred_element_type=F32)
    for j in range(rep):
        v_ref[j] = z[:, j * LANE:(j + 1) * LANE].astype(BF16)

    z = jnp.dot(w_ref[R_IQ:R_END, :], h, preferred_element_type=F32)
    for hh in range(IDX_HEADS):
        r0 = hh * IDX_DIM
        iq_ref[r0:r0 + IDX_DIM, :] = rope(z[r0:r0 + IDX_DIM])
    ik_ref[...] = rope(z[R_IK - R_IQ:R_MISC - R_IQ])
    misc_ref[...] = z[R_MISC - R_IQ:R_END - R_IQ]


def _project(xT, g_b, wT, gq_b, gk_b, cosT, sinT, T):
    N = xT.shape[1]
    tm = PROJ_TM
    tpb = T // tm
    rep = tm // LANE
    col = lambda i: (0, i)
    const = lambda i: (0, 0)
    return pl.pallas_call(
        _proj_kernel,
        grid=(N // tm,),
        in_specs=[
            pl.BlockSpec((D_MODEL, tm), col),
            pl.BlockSpec((D_MODEL, LANE), const),
            pl.BlockSpec((R_END, D_MODEL), const),
            pl.BlockSpec((N_MIXERS * MIX_WIDTH, LANE), const),
            pl.BlockSpec((5 * KV_WIDTH, LANE), const),
            pl.BlockSpec((HALF, tm), lambda i: (0, i % tpb)),
            pl.BlockSpec((HALF, tm), lambda i: (0, i % tpb)),
        ],
        out_specs=[
            pl.BlockSpec((N_MIXERS * MIX_WIDTH, tm), col),
            pl.BlockSpec((5 * KV_WIDTH, tm), col),
            pl.BlockSpec((rep, 5 * KV_WIDTH, LANE), lambda i: (i, 0, 0)),
            pl.BlockSpec((IDX_HEADS * IDX_DIM, tm), col),
            pl.BlockSpec((IDX_DIM, tm), col),
            pl.BlockSpec((MISC_ROWS, tm), col),
        ],
        out_shape=[
            jax.ShapeDtypeStruct((N_MIXERS * MIX_WIDTH, N), BF16),
            jax.ShapeDtypeStruct((5 * KV_WIDTH, N), BF16),
            jax.ShapeDtypeStruct((N // LANE, 5 * KV_WIDTH, LANE), BF16),
            jax.ShapeDtypeStruct((IDX_HEADS * IDX_DIM, N), F32),
            jax.ShapeDtypeStruct((IDX_DIM, N), F32),
            jax.ShapeDtypeStruct((MISC_ROWS, N), F32),
        ],
        compiler_params=_cparams(("parallel",)),
        name="in_projection",
    )(xT, g_b, wT, gq_b, gk_b, cosT, sinT)


def _padded_queries(q):
    zeros = jnp.zeros((HEAD_DIM, Q_BLOCK), q.dtype)
    out = []
    for kv in range(N_KV_HEADS):
        cols = []
        for g in range(GROUP):
            hq = kv * GROUP + g
            qh = q[hq * HEAD_DIM:(hq + 1) * HEAD_DIM, :]
            cols.append(jnp.concatenate([qh, zeros] if kv == 0 else [zeros, qh], axis=0))
        out.append(jnp.concatenate(cols, axis=1))
    return out


def _v_chunk(v_ref, blk0, nblk):
    return jnp.concatenate([v_ref[blk0 + j] for j in range(nblk)], axis=1)


def _masked_attention(nch, sel_fn, k_ref, v_ref, qall, s_ref, acc_ref):
    C = KEY_CHUNK
    lanes = N_KV_HEADS * GROUP * Q_BLOCK

    def qk_chunk(c, mx):
        off = pl.multiple_of(c * C, C)
        s = jnp.dot(k_ref[pl.ds(off, C), :], qall, preferred_element_type=F32) + sel_fn(c, off)
        s_ref[pl.ds(off, C), :] = s
        return jnp.maximum(mx, jnp.max(s, axis=0, keepdims=True))

    mx = lax.fori_loop(0, nch, qk_chunk, jnp.full((1, lanes), M_INIT, F32))
    acc_ref[...] = jnp.zeros(acc_ref.shape, F32)
    ones = jnp.ones((ONES_ROWS, C), BF16)

    def pv_chunk(c, carry):
        off = pl.multiple_of(c * C, C)
        p = jnp.exp2(s_ref[pl.ds(off, C), :] - mx).astype(BF16)
        vc = jnp.concatenate([_v_chunk(v_ref, c * (C // LANE), C // LANE), ones], axis=0)
        acc_ref[...] += jnp.dot(vc, p, preferred_element_type=F32)
        return carry

    lax.fori_loop(0, nch, pv_chunk, 0)
    l = acc_ref[KV_WIDTH:KV_WIDTH + 1, :]
    inv = jnp.where(l > 0.0, 1.0 / l, 0.0)
    half = GROUP * Q_BLOCK
    return [acc_ref[kv * HEAD_DIM:(kv + 1) * HEAD_DIM, kv * half:(kv + 1) * half]
            * inv[:, kv * half:(kv + 1) * half] for kv in range(N_KV_HEADS)]


def _dsa_kernel(iq_ref, misc_ref, ik_ref, q_ref, k_ref, v_ref, ltri_ref, o_ref,
                sc_ref, s_ref, acc_ref, *, k_top):
    C = KEY_CHUNK
    i = pl.program_id(1)
    nch = i // (C // Q_BLOCK) + 1
    qpos = i * Q_BLOCK + lax.broadcasted_iota(jnp.int32, (1, Q_BLOCK), 1)
    row = lax.broadcasted_iota(jnp.int32, (C, Q_BLOCK), 0)

    iq = iq_ref[...].astype(BF16)
    iq_cat = jnp.concatenate([iq[h * IDX_DIM:(h + 1) * IDX_DIM] for h in range(IDX_HEADS)], axis=1)
    w = misc_ref[0:IDX_HEADS, :] * ((IDX_HEADS ** -0.5) * (IDX_DIM ** -0.5))

    def fold(x):
        return x.reshape(C // 8, 8, Q_BLOCK)

    part = lambda v: jnp.full((8, Q_BLOCK), v, F32)

    def score_chunk(c, carry):
        lo, hi = carry
        off = pl.multiple_of(c * C, C)
        ikc = ik_ref[pl.ds(off, C), :].astype(BF16)
        d = jnp.dot(ikc, iq_cat, preferred_element_type=F32)
        s = jnp.maximum(d[:, 0:Q_BLOCK], 0.0) * w[0:1, :]
        for h in range(1, IDX_HEADS):
            s = s + jnp.maximum(d[:, h * Q_BLOCK:(h + 1) * Q_BLOCK], 0.0) * w[h:h + 1, :]
        causal = (off + row) <= qpos
        sc_ref[pl.ds(off, C), :] = jnp.where(causal, s, NEG_INF)
        lo = jnp.minimum(lo, jnp.min(fold(jnp.where(causal, s, BIG)), axis=0))
        hi = jnp.maximum(hi, jnp.max(fold(jnp.where(causal, s, -BIG)), axis=0))
        return lo, hi

    lo0, hi0 = lax.fori_loop(0, nch, score_chunk, (part(BIG), part(-BIG)))
    lo0 = jnp.min(lo0, axis=0, keepdims=True)
    hi0 = jnp.max(hi0, axis=0, keepdims=True)

    kf = float(k_top)
    select = i * Q_BLOCK >= k_top

    def stats(mid):
        mid8 = jnp.broadcast_to(mid, (8, Q_BLOCK))[None]

        def body(c, carry):
            cnt, a, b = carry
            for j in range(C // 64):
                off = pl.multiple_of(c * C + j * 64, 64)
                s = sc_ref[pl.ds(off, 64), :].reshape(8, 8, Q_BLOCK)
                gt = s > mid8
                cnt = cnt + jnp.sum(jnp.where(gt, 1.0, 0.0), axis=0)
                a = jnp.minimum(a, jnp.min(jnp.where(gt, s, BIG), axis=0))
                b = jnp.maximum(b, jnp.max(jnp.where(gt, -BIG, s), axis=0))
            return cnt, a, b

        cnt, a, b = lax.fori_loop(0, nch, body, (part(0.0), part(BIG), part(-BIG)))
        return (jnp.sum(cnt, axis=0, keepdims=True), jnp.min(a, axis=0, keepdims=True),
                jnp.max(b, axis=0, keepdims=True))

    def step(st):
        lo, hi, clo, chi = st
        mid = jnp.maximum(lo + (hi - lo) * 0.5, lo)
        mid = jnp.where(mid >= hi, lo, mid)
        cnt, a, b = stats(mid)
        up = cnt >= kf
        return (jnp.where(up, a, lo), jnp.where(up, hi, b),
                jnp.where(up, cnt, clo), jnp.where(up, chi, cnt))

    def cond(st):
        return jnp.logical_and(select, jnp.max(st[1] - st[0]) > 0.0)

    n_causal = (qpos + 1).astype(F32)
    st = (lo0, hi0, n_causal, jnp.zeros((1, Q_BLOCK), F32))
    st = lax.fori_loop(0, jnp.where(select, BISECT_FIXED_STEPS, 0), lambda _, s: step(s), st)
    lo, _, clo, chi = lax.while_loop(cond, lambda s: step(step(s)), st)
    thr = jnp.where(select, lo, 0.5 * NEG_INF)
    need = jnp.where(select, kf - chi, 0.0)
    ties = jnp.logical_and(select, jnp.max((clo - chi) - need) > 0.0)

    @pl.when(ties)
    def _():
        def mask_chunk(c, tie_carry):
            off = pl.multiple_of(c * C, C)
            s_idx = sc_ref[pl.ds(off, C), :]
            eqf = jnp.where(s_idx == thr, 1.0, 0.0)
            before = jnp.dot(ltri_ref[...], eqf.astype(BF16), preferred_element_type=F32) + tie_carry
            sc_ref[pl.ds(off, C), :] = jnp.where(s_idx > thr, 1.0, jnp.where(before < need, eqf, 0.0))
            return tie_carry + jnp.sum(eqf, axis=0, keepdims=True)

        lax.fori_loop(0, nch, mask_chunk, jnp.zeros((1, Q_BLOCK), F32))

    thr_sel = jnp.where(ties, 0.5, thr)
    qall = jnp.concatenate(_padded_queries(q_ref[...]), axis=1)

    def sel_fn(c, off):
        bias = jnp.where(sc_ref[pl.ds(off, C), :] >= thr_sel, 0.0, MASK_VAL)
        return _lane_tile(bias, N_KV_HEADS * GROUP)

    outs = _masked_attention(nch, sel_fn, k_ref, v_ref, qall, s_ref, acc_ref)
    for kv in range(N_KV_HEADS):
        o = outs[kv]
        for g in range(GROUP):
            hq = kv * GROUP + g
            o_ref[hq * HEAD_DIM:(hq + 1) * HEAD_DIM, :] = o[:, g * Q_BLOCK:(g + 1) * Q_BLOCK].astype(BF16)


def _dsa(iqT, miscT, ik_rows, qT, k_rows, vT128, ltri, B, T):
    nq = T // Q_BLOCK
    k_top = min(DSA_TOPK_MAX, T // 4)
    N = B * T
    return pl.pallas_call(
        functools.partial(_dsa_kernel, k_top=k_top),
        grid=(B, nq),
        in_specs=[
            pl.BlockSpec((IDX_HEADS * IDX_DIM, Q_BLOCK), lambda b, i: (0, b * nq + i)),
            pl.BlockSpec((MISC_ROWS, Q_BLOCK), lambda b, i: (0, b * nq + i)),
            pl.BlockSpec((T, IDX_DIM), lambda b, i: (b, 0)),
            pl.BlockSpec((MIX_WIDTH, Q_BLOCK), lambda b, i: (0, b * nq + i)),
            pl.BlockSpec((T, KV_WIDTH), lambda b, i: (b, 0)),
            pl.BlockSpec((nq, KV_WIDTH, LANE), lambda b, i: (b, 0, 0)),
            pl.BlockSpec((KEY_CHUNK, KEY_CHUNK), lambda b, i: (0, 0)),
        ],
        out_specs=pl.BlockSpec((MIX_WIDTH, Q_BLOCK), lambda b, i: (0, b * nq + i)),
        out_shape=jax.ShapeDtypeStruct((MIX_WIDTH, N), BF16),
        scratch_shapes=[
            pltpu.VMEM((T, Q_BLOCK), F32),
            pltpu.VMEM((T, N_KV_HEADS * GROUP * Q_BLOCK), F32),
            pltpu.VMEM((KV_WIDTH + ONES_ROWS, N_KV_HEADS * GROUP * Q_BLOCK), F32),
        ],
        compiler_params=_cparams(("parallel", "arbitrary")),
        name="dsa_attention",
    )(iqT, miscT, ik_rows, qT, k_rows, vT128, ltri)


def _band_kernel(q_ref, k_ref, v_ref, sink_ref, o_ref, *, n_prev, window, use_sink):
    i = pl.program_id(1)
    blk0 = jnp.maximum(i - n_prev, 0)
    nblk = n_prev + 1
    wb = nblk * Q_BLOCK
    kb = k_ref[pl.ds(pl.multiple_of(blk0 * Q_BLOCK, Q_BLOCK), wb), :]
    vb = _v_chunk(v_ref, blk0, nblk)
    qpad = _padded_queries(q_ref[...])
    lanes = GROUP * Q_BLOCK
    kpos = blk0 * Q_BLOCK + lax.broadcasted_iota(jnp.int32, (wb, lanes), 0)
    qpos = i * Q_BLOCK + (lax.broadcasted_iota(jnp.int32, (wb, lanes), 1) & (Q_BLOCK - 1))
    rel = qpos - kpos
    valid = (rel >= 0) & (rel < window)
    for kv in range(N_KV_HEADS):
        s = jnp.where(valid, jnp.dot(kb, qpad[kv], preferred_element_type=F32), MASK_VAL)
        m = jnp.max(s, axis=0, keepdims=True)
        if use_sink:
            sink = sink_ref[kv:kv + 1, :]
            m = jnp.maximum(m, sink)
        p = jnp.exp2(s - m)
        l = jnp.sum(p, axis=0, keepdims=True)
        if use_sink:
            l = l + jnp.exp2(sink - m)
        o = jnp.dot(vb, p.astype(BF16), preferred_element_type=F32)[kv * HEAD_DIM:(kv + 1) * HEAD_DIM, :]
        o = o * (1.0 / l)
        for g in range(GROUP):
            hq = kv * GROUP + g
            o_ref[hq * HEAD_DIM:(hq + 1) * HEAD_DIM, :] = o[:, g * Q_BLOCK:(g + 1) * Q_BLOCK].astype(o_ref.dtype)


def _band(qT, q_mixer, k_rows, kv_index, vT128, sink_lanes, window, use_sink, out_dtype, B, T, name):
    nq = T // Q_BLOCK
    N = B * T
    n_prev = (window + Q_BLOCK - 2) // Q_BLOCK
    return pl.pallas_call(
        functools.partial(_band_kernel, n_prev=n_prev, window=window, use_sink=use_sink),
        grid=(B, nq),
        in_specs=[
            pl.BlockSpec((MIX_WIDTH, Q_BLOCK), lambda b, i: (q_mixer, b * nq + i)),
            pl.BlockSpec((T, KV_WIDTH), lambda b, i: (b, kv_index)),
            pl.BlockSpec((nq, KV_WIDTH, LANE), lambda b, i: (b, kv_index, 0)),
            pl.BlockSpec((8, GROUP * Q_BLOCK), lambda b, i: (0, 0)),
        ],
        out_specs=pl.BlockSpec((MIX_WIDTH, Q_BLOCK), lambda b, i: (0, b * nq + i)),
        out_shape=jax.ShapeDtypeStruct((MIX_WIDTH, N), out_dtype),
        compiler_params=_cparams(("parallel", "arbitrary")),
        name=name,
    )(qT, k_rows, vT128, sink_lanes)


def _cmp_kernel(xa_ref, xb_ref, pe_ref, w1_ref, w2_ref, o_ref):
    half = (CMP_LEN // 2) * HEAD_DIM
    xa = (xa_ref[...].astype(F32) + pe_ref[0:1, :]).astype(BF16)
    xb = (xb_ref[...].astype(F32) + pe_ref[1:2, :]).astype(BF16)
    hid = (jnp.dot(xa, w1_ref[0:half, :], preferred_element_type=F32)
           + jnp.dot(xb, w1_ref[half:2 * half, :], preferred_element_type=F32))
    hid = jnp.maximum(hid, 0.0).astype(BF16)
    o_ref[...] = jnp.dot(hid, w2_ref[...], preferred_element_type=F32)


def _compress(chunks, pe, w1, w2, name):
    G, nc, half = chunks.shape
    nxt = jnp.concatenate([chunks[:, 1:], jnp.zeros((G, 1, half), chunks.dtype)], axis=1)
    pe2 = jnp.zeros((8, half), F32).at[0:2].set(pe.reshape(2, half))
    return pl.pallas_call(
        _cmp_kernel,
        grid=(G,),
        in_specs=[
            pl.BlockSpec((None, nc, half), lambda g: (g, 0, 0)),
            pl.BlockSpec((None, nc, half), lambda g: (g, 0, 0)),
            pl.BlockSpec((8, half), lambda g: (0, 0)),
            pl.BlockSpec((2 * half, CMP_HIDDEN), lambda g: (0, 0)),
            pl.BlockSpec((CMP_HIDDEN, HEAD_DIM), lambda g: (0, 0)),
        ],
        out_specs=pl.BlockSpec((None, nc, HEAD_DIM), lambda g: (g, 0, 0)),
        out_shape=jax.ShapeDtypeStruct((G, nc, HEAD_DIM), F32),
        compiler_params=_cparams(("parallel",)),
        name=name,
    )(chunks, nxt, pe2, w1.astype(BF16), w2.astype(BF16))


def _nsa_kernel(q_ref, kc_ref, vc_ref, ks_ref, vs_ref, misc_ref, ow_ref, ov_ref, e_ref, o_ref,
                imp_ref, s_ref, acc_ref, *, n_top):
    C = KEY_CHUNK
    i = pl.program_id(1)
    nch = i // (C // Q_BLOCK) + 1
    ncmp = kc_ref.shape[0]
    nsel = ov_ref.shape[0]
    lanes = GROUP * Q_BLOCK
    t = i * Q_BLOCK + lax.broadcasted_iota(jnp.int32, (1, Q_BLOCK), 1)
    t4 = i * Q_BLOCK + (lax.broadcasted_iota(jnp.int32, (1, lanes), 1) & (Q_BLOCK - 1))
    row = lax.broadcasted_iota(jnp.int32, (C, Q_BLOCK), 0)
    crow = lax.broadcasted_iota(jnp.int32, (ncmp, lanes), 0)
    jrow = lax.broadcasted_iota(jnp.int32, (nsel, Q_BLOCK), 0)
    cur = t // SLC_LEN
    qpad = _padded_queries(q_ref[...])
    kc = kc_ref[...]
    vcm = vc_ref[...]
    ov = ov_ref[...]

    o_cmps, selbs = [], []
    for kv in range(N_KV_HEADS):
        s = jnp.dot(kc, qpad[kv], preferred_element_type=F32)
        s = jnp.where(crow * CMP_STRIDE + (CMP_LEN - 1) <= t4, s, MASK_VAL)
        m = jnp.maximum(jnp.max(s, axis=0, keepdims=True), M_INIT)
        p = jnp.exp2(s - m)
        l = jnp.sum(p, axis=0, keepdims=True)
        p = p * jnp.where(l > 0.0, 1.0 / l, 0.0)
        o_cmp = jnp.dot(vcm, p.astype(BF16), preferred_element_type=F32)[kv * HEAD_DIM:(kv + 1) * HEAD_DIM, :]

        pg = p[:, 0:Q_BLOCK]
        for g in range(1, GROUP):
            pg = pg + p[:, g * Q_BLOCK:(g + 1) * Q_BLOCK]
        p_hi = pg.astype(BF16)
        p_lo = (pg - p_hi.astype(F32)).astype(BF16)
        imp = (jnp.dot(ov, p_hi, preferred_element_type=F32)
               + jnp.dot(ov, p_lo, preferred_element_type=F32))
        forced = (jrow == 0) | (jrow == cur) | (jrow == cur - 1)
        imp = jnp.where(forced, FORCE_SCORE, imp)
        imp = jnp.where(jrow * SLC_LEN <= t, imp, NEG_INF)
        imp_ref[...] = imp

        def rank_body(ii, rank):
            r = imp_ref[pl.ds(ii, 1), :]
            ahead = jnp.where(r > imp, 1.0, jnp.where((r == imp) & (ii < jrow), 1.0, 0.0))
            return rank + ahead
        rank = lax.fori_loop(0, nsel, rank_body, jnp.zeros((nsel, Q_BLOCK), F32), unroll=8)
        selbs.append(jnp.where(rank < float(n_top), 1.0, 0.0).astype(BF16))
        o_cmps.append(o_cmp)

    qall = jnp.concatenate(qpad, axis=1)
    selb_all = jnp.concatenate(selbs, axis=1)

    def sel_fn(c, off):
        mk = jnp.dot(e_ref[pl.ds(off, C), :], selb_all, preferred_element_type=F32)
        causal = (off + row) <= t
        parts = []
        for kv in range(N_KV_HEADS):
            keep = jnp.where(causal, mk[:, kv * Q_BLOCK:(kv + 1) * Q_BLOCK], 0.0) > 0.5
            parts.append(_lane_tile(jnp.where(keep, 0.0, MASK_VAL), GROUP))
        return jnp.concatenate(parts, axis=1)

    o_slcs = _masked_attention(nch, sel_fn, ks_ref, vs_ref, qall, s_ref, acc_ref)

    for kv in range(N_KV_HEADS):
        for g in range(GROUP):
            hq = kv * GROUP + g
            gr = GATE_ROW0 + hq * 3
            gate = [1.0 / (1.0 + jnp.exp(-misc_ref[gr + br:gr + br + 1, :])) for br in range(3)]
            out = (gate[0] * o_cmps[kv][:, g * Q_BLOCK:(g + 1) * Q_BLOCK]
                   + gate[1] * o_slcs[kv][:, g * Q_BLOCK:(g + 1) * Q_BLOCK]
                   + gate[2] * ow_ref[hq * HEAD_DIM:(hq + 1) * HEAD_DIM, :])
            o_ref[hq * HEAD_DIM:(hq + 1) * HEAD_DIM, :] = out.astype(BF16)


def _nsa(qT, kcmp_rows, vcmpT, k_rows, vT128, miscT, o_winT, ovT, emat, B, T):
    nq = T // Q_BLOCK
    N = B * T
    ncmp = T // CMP_STRIDE
    nsel = T // SLC_LEN
    n_top = min(SLC_TOPN, nsel)
    return pl.pallas_call(
        functools.partial(_nsa_kernel, n_top=n_top),
        grid=(B, nq),
        in_specs=[
            pl.BlockSpec((MIX_WIDTH, Q_BLOCK), lambda b, i: (2, b * nq + i)),
            pl.BlockSpec((None, ncmp, KV_WIDTH), lambda b, i: (b, 0, 0)),
            pl.BlockSpec((None, KV_WIDTH, ncmp), lambda b, i: (b, 0, 0)),
            pl.BlockSpec((T, KV_WIDTH), lambda b, i: (b, 3)),
            pl.BlockSpec((nq, KV_WIDTH, LANE), lambda b, i: (b, 3, 0)),
            pl.BlockSpec((MISC_ROWS, Q_BLOCK), lambda b, i: (0, b * nq + i)),
            pl.BlockSpec((MIX_WIDTH, Q_BLOCK), lambda b, i: (0, b * nq + i)),
            pl.BlockSpec((nsel, ncmp), lambda b, i: (0, 0)),
            pl.BlockSpec((T, nsel), lambda b, i: (0, 0)),
        ],
        out_specs=pl.BlockSpec((MIX_WIDTH, Q_BLOCK), lambda b, i: (0, b * nq + i)),
        out_shape=jax.ShapeDtypeStruct((MIX_WIDTH, N), BF16),
        scratch_shapes=[
            pltpu.VMEM((nsel, Q_BLOCK), F32),
            pltpu.VMEM((T, N_KV_HEADS * GROUP * Q_BLOCK), F32),
            pltpu.VMEM((KV_WIDTH + ONES_ROWS, N_KV_HEADS * GROUP * Q_BLOCK), F32),
        ],
        compiler_params=_cparams(("parallel", "arbitrary")),
        name="nsa_attention",
    )(qT, kcmp_rows, vcmpT, k_rows, vT128, miscT, o_winT, ovT, emat)


def _merge_kernel(x_ref, g_ref, wg_ref, oa_ref, ob_ref, oc_ref, wb_ref, wo_ref, out_ref):
    tm = x_ref.shape[1]
    x = x_ref[...]
    ss = jnp.sum(x * x, axis=0, keepdims=True)
    h = (x * lax.rsqrt(ss * (1.0 / D_MODEL) + EPS) * _lane_tile(g_ref[...], tm // LANE)).astype(BF16)
    acc = None
    for n, o_ref in enumerate((oa_ref, ob_ref, oc_ref)):
        gm = jnp.dot(wg_ref[n * D_MODEL:(n + 1) * D_MODEL, :], h, preferred_element_type=F32)
        y = jnp.dot(wb_ref[n], o_ref[...], preferred_element_type=F32)
        term = y * (1.0 / (1.0 + jnp.exp(-gm)))
        acc = term if acc is None else acc + term
    out_ref[...] = x + jnp.dot(wo_ref[...], acc.astype(BF16), preferred_element_type=F32)


def _merge(xT, g_b, wgT, oa, ob, oc, wbT, woT):
    N = xT.shape[1]
    tm = 256
    col = lambda i: (0, i)
    return pl.pallas_call(
        _merge_kernel,
        grid=(N // tm,),
        in_specs=[
            pl.BlockSpec((D_MODEL, tm), col),
            pl.BlockSpec((D_MODEL, LANE), lambda i: (0, 0)),
            pl.BlockSpec((N_MIXERS * D_MODEL, D_MODEL), lambda i: (0, 0)),
            pl.BlockSpec((MIX_WIDTH, tm), col),
            pl.BlockSpec((MIX_WIDTH, tm), col),
            pl.BlockSpec((MIX_WIDTH, tm), col),
            pl.BlockSpec((N_MIXERS, D_MODEL, MIX_WIDTH), lambda i: (0, 0, 0)),
            pl.BlockSpec((D_MODEL, D_MODEL), lambda i: (0, 0)),
        ],
        out_specs=pl.BlockSpec((D_MODEL, tm), col),
        out_shape=jax.ShapeDtypeStruct((D_MODEL, N), F32),
        compiler_params=_cparams(("parallel",)),
        name="merge_out_projection",
    )(xT, g_b, wgT, oa, ob, oc, wbT, woT)


def _mlp_kernel(x_ref, g_ref, wu_ref, wd_ref, out_ref, h_ref, acc_ref):
    f = pl.program_id(1)
    tm = x_ref.shape[1]

    @pl.when(f == 0)
    def _():
        x = x_ref[...]
        ss = jnp.sum(x * x, axis=0, keepdims=True)
        h_ref[...] = (x * lax.rsqrt(ss * (1.0 / D_MODEL) + EPS)
                      * _lane_tile(g_ref[...], tm // LANE)).astype(BF16)
        acc_ref[...] = jnp.zeros(acc_ref.shape, F32)

    u = jnp.maximum(jnp.dot(wu_ref[...], h_ref[...], preferred_element_type=F32), 0.0)
    acc_ref[...] += jnp.dot(wd_ref[...], (u * u).astype(BF16), preferred_element_type=F32)

    @pl.when(f == pl.num_programs(1) - 1)
    def _():
        out_ref[...] = x_ref[...] + acc_ref[...]


def _mlp(xT, g_b, wuT, wdT):
    N = xT.shape[1]
    tm, tf = 512, 1024
    return pl.pallas_call(
        _mlp_kernel,
        grid=(N // tm, D_FF // tf),
        in_specs=[
            pl.BlockSpec((D_MODEL, tm), lambda i, f: (0, i)),
            pl.BlockSpec((D_MODEL, LANE), lambda i, f: (0, 0)),
            pl.BlockSpec((tf, D_MODEL), lambda i, f: (f, 0)),
            pl.BlockSpec((D_MODEL, tf), lambda i, f: (0, f)),
        ],
        out_specs=pl.BlockSpec((D_MODEL, tm), lambda i, f: (0, i)),
        out_shape=jax.ShapeDtypeStruct((D_MODEL, N), F32),
        scratch_shapes=[pltpu.VMEM((D_MODEL, tm), BF16), pltpu.VMEM((D_MODEL, tm), F32)],
        compiler_params=_cparams(("parallel", "arbitrary")),
        name="relu2_mlp",
    )(xT, g_b, wuT, wdT)


def _lane_bcast(v):
    return jnp.broadcast_to(v.astype(F32)[:, None], (v.shape[0], LANE))


def _to_chunks(rows, B, T):
    half_tokens = CMP_LEN // 2
    t = rows.reshape(B, T, N_KV_HEADS, HEAD_DIM).transpose(0, 2, 1, 3)
    return t.reshape(B * N_KV_HEADS, T // half_tokens, half_tokens * HEAD_DIM)


def kernel(x, norm_mix, w_in, q_norm, k_norm, sinks, cmp_pe_k, cmp_pe_v, w_ck1, w_ck2, w_cv1, w_cv2,
           w_branch, w_out, norm_mlp, w_up, w_down):
    B, T, D = x.shape
    N = B * T
    depth = w_in.shape[0]
    nq = T // Q_BLOCK
    ncmp = T // CMP_STRIDE
    nsel = T // SLC_LEN

    inv_freq = ROPE_THETA ** (-jnp.arange(0, HEAD_DIM, 2, dtype=F32) / HEAD_DIM)
    ang = jnp.arange(T, dtype=F32)[:, None] * inv_freq[None, :]
    cosT, sinT = jnp.cos(ang).T, jnp.sin(ang).T

    ci = np.arange(KEY_CHUNK)
    ltri = jnp.asarray(ci[None, :] < ci[:, None], BF16)
    emat = jnp.asarray((np.arange(T)[:, None] // SLC_LEN) == np.arange(nsel)[None, :], BF16)
    cstart = np.arange(ncmp) * CMP_STRIDE
    sstart = np.arange(nsel) * SLC_LEN
    n_cmp_valid = (T - CMP_LEN) // CMP_STRIDE + 1
    ov = ((cstart[None, :] < sstart[:, None] + SLC_LEN) & (cstart[None, :] + CMP_LEN > sstart[:, None])
          & (np.arange(ncmp)[None, :] < n_cmp_valid))
    ovT = jnp.asarray(ov, BF16)

    xT = x.reshape(N, D).T

    for l in range(depth):
        w = w_in[l]
        sl = lambda s: w[:, s[0]:s[1]]
        wT = jnp.concatenate(
            [sl(S_QA), sl(S_QB), sl(S_QC),
             sl(S_KA), sl(S_KB), sl(S_KC), sl(S_KSL), sl(S_KWN),
             sl(S_VA), sl(S_VB), sl(S_VC), sl(S_VSL), sl(S_VWN),
             sl(S_IQ), sl(S_IK), sl(S_IW), jnp.zeros((D, GATE_ROW0 - IDX_HEADS), w.dtype), sl(S_GC)],
            axis=1).T.astype(BF16)
        wgT = sl(S_GM).T.astype(BF16)
        gq_b = _lane_bcast(jnp.tile(q_norm[l][:, None, :], (1, N_Q_HEADS, 1)).reshape(-1))
        kg = k_norm[l]
        gk_b = _lane_bcast(jnp.tile(jnp.stack([kg[0], kg[1], kg[2], kg[2], kg[2]])[:, None, :],
                                    (1, N_KV_HEADS, 1)).reshape(-1))
        sink_lanes = jnp.zeros((8, GROUP * Q_BLOCK), F32).at[0:N_KV_HEADS].set(
            jnp.repeat(sinks[l].astype(F32).reshape(N_KV_HEADS, GROUP) * LOG2E, Q_BLOCK, axis=1))

        qT, kT, vT128, iqT, ikT, miscT = _project(
            xT, _lane_bcast(norm_mix[l]), wT, gq_b, gk_b, cosT, sinT, T)
        k_rows = kT.T
        ik_rows = ikT.T

        o_a = _dsa(iqT, miscT, ik_rows, qT, k_rows, vT128, ltri, B, T)
        o_b = _band(qT, 1, k_rows, 1, vT128, sink_lanes, SWA_WINDOW, True, BF16, B, T, "swa_attention")
        o_w = _band(qT, 2, k_rows, 4, vT128, sink_lanes, NSA_WINDOW, False, F32, B, T, "nsa_window_attention")

        kc_chunks = _to_chunks(k_rows[:, 2 * KV_WIDTH:3 * KV_WIDTH].reshape(B, T, KV_WIDTH), B, T)
        vc_rows = vT128[:, 2 * KV_WIDTH:3 * KV_WIDTH, :].reshape(B, nq, KV_WIDTH, LANE)
        vc_rows = vc_rows.transpose(0, 1, 3, 2).reshape(B, T, KV_WIDTH)
        vc_chunks = _to_chunks(vc_rows, B, T)
        k_cmp = _compress(kc_chunks, cmp_pe_k[l], w_ck1[l], w_ck2[l], "compress_k")
        v_cmp = _compress(vc_chunks, cmp_pe_v[l], w_cv1[l], w_cv2[l], "compress_v")
        kcmp_rows = k_cmp.reshape(B, N_KV_HEADS, ncmp, HEAD_DIM).transpose(0, 2, 1, 3)
        kcmp_rows = kcmp_rows.reshape(B, ncmp, KV_WIDTH).astype(BF16)
        vcmpT = v_cmp.reshape(B, N_KV_HEADS, ncmp, HEAD_DIM).transpose(0, 1, 3, 2)
        vcmpT = vcmpT.reshape(B, KV_WIDTH, ncmp).astype(BF16)

        o_c = _nsa(qT, kcmp_rows, vcmpT, k_rows, vT128, miscT, o_w, ovT, emat, B, T)

        wbT = w_branch[l].transpose(0, 2, 1).astype(BF16)
        xT = _merge(xT, _lane_bcast(norm_mix[l]), wgT, o_a, o_b, o_c, wbT, w_out[l].T.astype(BF16))
        xT = _mlp(xT, _lane_bcast(norm_mlp[l]), w_up[l].T.astype(BF16), w_down[l].T.astype(BF16))

    return xT.T.reshape(B, T, D)
```

```python
import functools

import numpy as np
import jax
import jax.numpy as jnp
from jax import lax
from jax.experimental import pallas as pl
from jax.experimental.pallas import tpu as pltpu

F32 = jnp.float32
BF16 = jnp.bfloat16

D_MODEL = 1024
HEAD_DIM = 64
HALF = HEAD_DIM // 2
N_Q_HEADS = 8
N_KV_HEADS = 2
GROUP = N_Q_HEADS // N_KV_HEADS
MIX_WIDTH = N_Q_HEADS * HEAD_DIM
KV_WIDTH = N_KV_HEADS * HEAD_DIM
N_MIXERS = 3
D_FF = 4 * D_MODEL
ROPE_THETA = 10000.0
EPS = 1e-6
NEG_INF = -1e30
MASK_VAL = -2e30
M_INIT = -1e30
BIG = 3e38
Q_BLOCK = 128
KEY_CHUNK = 512
BISECT_FIXED_STEPS = 12
ONES_ROWS = 16
IDX_HEADS = 4
IDX_DIM = 64
DSA_TOPK_MAX = 256
SWA_WINDOW = 128
CMP_LEN = 32
CMP_STRIDE = 16
CMP_HIDDEN = 256
SLC_LEN = 64
SLC_TOPN = 16
NSA_WINDOW = 512
FORCE_SCORE = 1e9
LOG2E = 1.4426950408889634
Q_SCALE = (HEAD_DIM ** -0.5) * LOG2E
LANE = 128
LANE_SHIFT = 7
ATT_BIG_STEP = 1024
VMEM_LIMIT = 56 * 1024 * 1024

IN_SIZES = (
    MIX_WIDTH, KV_WIDTH, KV_WIDTH, IDX_HEADS * IDX_DIM, IDX_DIM, IDX_HEADS,
    MIX_WIDTH, KV_WIDTH, KV_WIDTH,
    MIX_WIDTH, KV_WIDTH, KV_WIDTH, KV_WIDTH, KV_WIDTH, KV_WIDTH, KV_WIDTH, 3 * N_Q_HEADS,
    N_MIXERS * D_MODEL,
)
_OFF = np.concatenate([[0], np.cumsum(IN_SIZES)]).astype(int)
(S_QA, S_KA, S_VA, S_IQ, S_IK, S_IW, S_QB, S_KB, S_VB,
 S_QC, S_KC, S_VC, S_KSL, S_VSL, S_KWN, S_VWN, S_GC, S_GM) = [
    (int(_OFF[i]), int(_OFF[i + 1])) for i in range(len(IN_SIZES))]

R_Q = 0
R_K = R_Q + N_MIXERS * MIX_WIDTH
R_V = R_K + 5 * KV_WIDTH
R_IQ = R_V + 5 * KV_WIDTH
R_IK = R_IQ + IDX_HEADS * IDX_DIM
R_MISC = R_IK + IDX_DIM
MISC_ROWS = 32
GATE_ROW0 = 8
R_END = R_MISC + MISC_ROWS
PROJ_TM = 512


def _cparams(sem):
    return pltpu.CompilerParams(dimension_semantics=sem, vmem_limit_bytes=VMEM_LIMIT)


def _lane_tile(a, n):
    return a if n == 1 else jnp.concatenate([a] * n, axis=1)


def _proj_kernel(x_ref, g_ref, w_ref, gq_ref, gk_ref, cos_ref, sin_ref,
                 q_ref, k_ref, v_ref, iq_ref, ik_ref, misc_ref):
    tm = x_ref.shape[1]
    rep = tm // LANE
    x = x_ref[...]
    ss = jnp.sum(x * x, axis=0, keepdims=True)
    h = (x * lax.rsqrt(ss * (1.0 / D_MODEL) + EPS) * _lane_tile(g_ref[...], rep)).astype(BF16)
    cos = cos_ref[...]
    sin = sin_ref[...]

    def rope(z):
        z1, z2 = z[:HALF], z[HALF:]
        return jnp.concatenate([z1 * cos - z2 * sin, z2 * cos + z1 * sin], axis=0)

    def head_norm(z, gain):
        ssq = jnp.sum(z * z, axis=0, keepdims=True)
        return z * lax.rsqrt(ssq * (1.0 / HEAD_DIM) + EPS) * _lane_tile(gain, rep)

    for m in range(N_MIXERS):
        z = jnp.dot(w_ref[R_Q + m * MIX_WIDTH:R_Q + (m + 1) * MIX_WIDTH, :], h,
                    preferred_element_type=F32)
        for hh in range(N_Q_HEADS):
            r0 = m * MIX_WIDTH + hh * HEAD_DIM
            zh = head_norm(z[hh * HEAD_DIM:(hh + 1) * HEAD_DIM], gq_ref[r0:r0 + HEAD_DIM, :])
            q_ref[r0:r0 + HEAD_DIM, :] = (rope(zh) * Q_SCALE).astype(BF16)

    z = jnp.dot(w_ref[R_K:R_V, :], h, preferred_element_type=F32)
    for hh in range(5 * N_KV_HEADS):
        r0 = hh * HEAD_DIM
        zh = head_norm(z[r0:r0 + HEAD_DIM], gk_ref[r0:r0 + HEAD_DIM, :])
        k_ref[r0:r0 + HEAD_DIM, :] = rope(zh).astype(BF16)

    z = jnp.dot(w_ref[R_V:R_IQ, :], h, preferred_element_type=F32)
    for j in range(rep):
        v_ref[j] = z[:, j * LANE:(j + 1) * LANE].astype(BF16)

    z = jnp.dot(w_ref[R_IQ:R_END, :], h, preferred_element_type=F32)
    for hh in range(IDX_HEADS):
        r0 = hh * IDX_DIM
        iq_ref[r0:r0 + IDX_DIM, :] = rope(z[r0:r0 + IDX_DIM])
    ik_ref[...] = rope(z[R_IK - R_IQ:R_MISC - R_IQ])
    misc_ref[...] = z[R_MISC - R_IQ:R_END - R_IQ]


def _project(xT, g_b, wT, gq_b, gk_b, cosT, sinT, T):
    N = xT.shape[1]
    tm = PROJ_TM
    tpb = T // tm
    rep = tm // LANE
    col = lambda i: (0, i)
    const = lambda i: (0, 0)
    return pl.pallas_call(
        _proj_kernel,
        grid=(N // tm,),
        in_specs=[
            pl.BlockSpec((D_MODEL, tm), col),
            pl.BlockSpec((D_MODEL, LANE), const),
            pl.BlockSpec((R_END, D_MODEL), const),
            pl.BlockSpec((N_MIXERS * MIX_WIDTH, LANE), const),
            pl.BlockSpec((5 * KV_WIDTH, LANE), const),
            pl.BlockSpec((HALF, tm), lambda i: (0, i % tpb)),
            pl.BlockSpec((HALF, tm), lambda i: (0, i % tpb)),
        ],
        out_specs=[
            pl.BlockSpec((N_MIXERS * MIX_WIDTH, tm), col),
            pl.BlockSpec((5 * KV_WIDTH, tm), col),
            pl.BlockSpec((rep, 5 * KV_WIDTH, LANE), lambda i: (i, 0, 0)),
            pl.BlockSpec((IDX_HEADS * IDX_DIM, tm), col),
            pl.BlockSpec((IDX_DIM, tm), col),
            pl.BlockSpec((MISC_ROWS, tm), col),
        ],
        out_shape=[
            jax.ShapeDtypeStruct((N_MIXERS * MIX_WIDTH, N), BF16),
            jax.ShapeDtypeStruct((5 * KV_WIDTH, N), BF16),
            jax.ShapeDtypeStruct((N // LANE, 5 * KV_WIDTH, LANE), BF16),
            jax.ShapeDtypeStruct((IDX_HEADS * IDX_DIM, N), F32),
            jax.ShapeDtypeStruct((IDX_DIM, N), F32),
            jax.ShapeDtypeStruct((MISC_ROWS, N), F32),
        ],
        compiler_params=_cparams(("parallel",)),
        name="in_projection",
    )(xT, g_b, wT, gq_b, gk_b, cosT, sinT)


def _padded_queries(q):
    zeros = jnp.zeros((HEAD_DIM, Q_BLOCK), q.dtype)
    out = []
    for kv in range(N_KV_HEADS):
        cols = []
        for g in range(GROUP):
            hq = kv * GROUP + g
            qh = q[hq * HEAD_DIM:(hq + 1) * HEAD_DIM, :]
            cols.append(jnp.concatenate([qh, zeros] if kv == 0 else [zeros, qh], axis=0))
        out.append(jnp.concatenate(cols, axis=1))
    return out


def _v_chunk(v_ref, blk0, nblk):
    return jnp.concatenate([v_ref[blk0 + j] for j in range(nblk)], axis=1)


def _masked_attention(nch, sel_fn, k_ref, v_ref, qall, s_ref, acc_ref):
    C = KEY_CHUNK
    CB = ATT_BIG_STEP
    lanes = N_KV_HEADS * GROUP * Q_BLOCK
    nbig = nch // (CB // C)
    nsmall = nch - nbig * (CB // C)
    small0 = nbig * CB

    def walk(fn, carry):
        carry = lax.fori_loop(0, nbig, lambda c, x: fn(pl.multiple_of(c * CB, CB), CB, x), carry)
        return lax.fori_loop(0, nsmall, lambda c, x: fn(pl.multiple_of(small0 + c * C, C), C, x), carry)

    def qk_step(off, size, mx):
        s = jnp.dot(k_ref[pl.ds(off, size), :], qall, preferred_element_type=F32) + sel_fn(off, size)
        s_ref[pl.ds(off, size), :] = s
        return jnp.maximum(mx, jnp.max(s, axis=0, keepdims=True))

    mx = walk(qk_step, jnp.full((1, lanes), M_INIT, F32))
    acc_ref[...] = jnp.zeros(acc_ref.shape, F32)

    def pv_step(off, size, carry):
        p = jnp.exp2(s_ref[pl.ds(off, size), :] - mx).astype(BF16)
        vc = jnp.concatenate([_v_chunk(v_ref, lax.shift_right_logical(off, LANE_SHIFT), size // LANE),
                              jnp.ones((ONES_ROWS, size), BF16)], axis=0)
        acc_ref[...] += jnp.dot(vc, p, preferred_element_type=F32)
        return carry

    walk(pv_step, 0)
    l = acc_ref[KV_WIDTH:KV_WIDTH + 1, :]
    inv = jnp.where(l > 0.0, 1.0 / l, 0.0)
    half = GROUP * Q_BLOCK
    return [acc_ref[kv * HEAD_DIM:(kv + 1) * HEAD_DIM, kv * half:(kv + 1) * half]
            * inv[:, kv * half:(kv + 1) * half] for kv in range(N_KV_HEADS)]


def _dsa_kernel(iq_ref, misc_ref, ik_ref, q_ref, k_ref, v_ref, ltri_ref, o_ref,
                sc_ref, s_ref, acc_ref, *, k_top):
    C = KEY_CHUNK
    i = pl.program_id(1)
    nch = i // (C // Q_BLOCK) + 1
    qpos = i * Q_BLOCK + lax.broadcasted_iota(jnp.int32, (1, Q_BLOCK), 1)
    row = lax.broadcasted_iota(jnp.int32, (C, Q_BLOCK), 0)

    iq = iq_ref[...].astype(BF16)
    iq_cat = jnp.concatenate([iq[h * IDX_DIM:(h + 1) * IDX_DIM] for h in range(IDX_HEADS)], axis=1)
    w = misc_ref[0:IDX_HEADS, :] * ((IDX_HEADS ** -0.5) * (IDX_DIM ** -0.5))

    def fold(x):
        return x.reshape(C // 8, 8, Q_BLOCK)

    part = lambda v: jnp.full((8, Q_BLOCK), v, F32)

    def score_chunk(c, carry):
        lo, hi = carry
        off = pl.multiple_of(c * C, C)
        ikc = ik_ref[pl.ds(off, C), :].astype(BF16)
        d = jnp.dot(ikc, iq_cat, preferred_element_type=F32)
        s = jnp.maximum(d[:, 0:Q_BLOCK], 0.0) * w[0:1, :]
        for h in range(1, IDX_HEADS):
            s = s + jnp.maximum(d[:, h * Q_BLOCK:(h + 1) * Q_BLOCK], 0.0) * w[h:h + 1, :]
        causal = (off + row) <= qpos
        sc_ref[pl.ds(off, C), :] = jnp.where(causal, s, NEG_INF)
        lo = jnp.minimum(lo, jnp.min(fold(jnp.where(causal, s, BIG)), axis=0))
        hi = jnp.maximum(hi, jnp.max(fold(jnp.where(causal, s, -BIG)), axis=0))
        return lo, hi

    lo0, hi0 = lax.fori_loop(0, nch, score_chunk, (part(BIG), part(-BIG)))
    lo0 = jnp.min(lo0, axis=0, keepdims=True)
    hi0 = jnp.max(hi0, axis=0, keepdims=True)

    kf = float(k_top)
    select = i * Q_BLOCK >= k_top

    def stats(mid):
        mid8 = jnp.broadcast_to(mid, (8, Q_BLOCK))[None]

        def body(c, carry):
            cnt, a, b = carry
            for j in range(C // 64):
                off = pl.multiple_of(c * C + j * 64, 64)
                s = sc_ref[pl.ds(off, 64), :].reshape(8, 8, Q_BLOCK)
                gt = s > mid8
                cnt = cnt + jnp.sum(jnp.where(gt, 1.0, 0.0), axis=0)
                a = jnp.minimum(a, jnp.min(jnp.where(gt, s, BIG), axis=0))
                b = jnp.maximum(b, jnp.max(jnp.where(gt, -BIG, s), axis=0))
            return cnt, a, b

        cnt, a, b = lax.fori_loop(0, nch, body, (part(0.0), part(BIG), part(-BIG)))
        return (jnp.sum(cnt, axis=0, keepdims=True), jnp.min(a, axis=0, keepdims=True),
                jnp.max(b, axis=0, keepdims=True))

    def step(st):
        lo, hi, clo, chi = st
        mid = jnp.maximum(lo + (hi - lo) * 0.5, lo)
        mid = jnp.where(mid >= hi, lo, mid)
        cnt, a, b = stats(mid)
        up = cnt >= kf
        return (jnp.where(up, a, lo), jnp.where(up, hi, b),
                jnp.where(up, cnt, clo), jnp.where(up, chi, cnt))

    def cond(st):
        return jnp.logical_and(select, jnp.max(st[1] - st[0]) > 0.0)

    n_causal = (qpos + 1).astype(F32)
    st = (lo0, hi0, n_causal, jnp.zeros((1, Q_BLOCK), F32))
    st = lax.fori_loop(0, jnp.where(select, BISECT_FIXED_STEPS, 0), lambda _, s: step(s), st)
    lo, _, clo, chi = lax.while_loop(cond, lambda s: step(step(s)), st)
    thr = jnp.where(select, lo, 0.5 * NEG_INF)
    need = jnp.where(select, kf - chi, 0.0)
    ties = jnp.logical_and(select, jnp.max((clo - chi) - need) > 0.0)

    @pl.when(ties)
    def _():
        def mask_chunk(c, tie_carry):
            off = pl.multiple_of(c * C, C)
            s_idx = sc_ref[pl.ds(off, C), :]
            eqf = jnp.where(s_idx == thr, 1.0, 0.0)
            before = jnp.dot(ltri_ref[...], eqf.astype(BF16), preferred_element_type=F32) + tie_carry
            sc_ref[pl.ds(off, C), :] = jnp.where(s_idx > thr, 1.0, jnp.where(before < need, eqf, 0.0))
            return tie_carry + jnp.sum(eqf, axis=0, keepdims=True)

        lax.fori_loop(0, nch, mask_chunk, jnp.zeros((1, Q_BLOCK), F32))

    thr_sel = jnp.where(ties, 0.5, thr)
    qall = jnp.concatenate(_padded_queries(q_ref[...]), axis=1)

    def sel_fn(off, size):
        bias = jnp.where(sc_ref[pl.ds(off, size), :] >= thr_sel, 0.0, MASK_VAL)
        return _lane_tile(bias, N_KV_HEADS * GROUP)

    outs = _masked_attention(nch, sel_fn, k_ref, v_ref, qall, s_ref, acc_ref)
    for kv in range(N_KV_HEADS):
        o = outs[kv]
        for g in range(GROUP):
            hq = kv * GROUP + g
            o_ref[hq * HEAD_DIM:(hq + 1) * HEAD_DIM, :] = o[:, g * Q_BLOCK:(g + 1) * Q_BLOCK].astype(BF16)


def _dsa(iqT, miscT, ik_rows, qT, k_rows, vT128, ltri, B, T):
    nq = T // Q_BLOCK
    k_top = min(DSA_TOPK_MAX, T // 4)
    N = B * T
    return pl.pallas_call(
        functools.partial(_dsa_kernel, k_top=k_top),
        grid=(B, nq),
        in_specs=[
            pl.BlockSpec((IDX_HEADS * IDX_DIM, Q_BLOCK), lambda b, i: (0, b * nq + i)),
            pl.BlockSpec((MISC_ROWS, Q_BLOCK), lambda b, i: (0, b * nq + i)),
            pl.BlockSpec((T, IDX_DIM), lambda b, i: (b, 0)),
            pl.BlockSpec((MIX_WIDTH, Q_BLOCK), lambda b, i: (0, b * nq + i)),
            pl.BlockSpec((T, KV_WIDTH), lambda b, i: (b, 0)),
            pl.BlockSpec((nq, KV_WIDTH, LANE), lambda b, i: (b, 0, 0)),
            pl.BlockSpec((KEY_CHUNK, KEY_CHUNK), lambda b, i: (0, 0)),
        ],
        out_specs=pl.BlockSpec((MIX_WIDTH, Q_BLOCK), lambda b, i: (0, b * nq + i)),
        out_shape=jax.ShapeDtypeStruct((MIX_WIDTH, N), BF16),
        scratch_shapes=[
            pltpu.VMEM((T, Q_BLOCK), F32),
            pltpu.VMEM((T, N_KV_HEADS * GROUP * Q_BLOCK), F32),
            pltpu.VMEM((KV_WIDTH + ONES_ROWS, N_KV_HEADS * GROUP * Q_BLOCK), F32),
        ],
        compiler_params=_cparams(("parallel", "arbitrary")),
        name="dsa_attention",
    )(iqT, miscT, ik_rows, qT, k_rows, vT128, ltri)


def _band_kernel(q_ref, k_ref, v_ref, sink_ref, o_ref, *, n_prev, window, use_sink):
    i = pl.program_id(1)
    blk0 = jnp.maximum(i - n_prev, 0)
    nblk = n_prev + 1
    wb = nblk * Q_BLOCK
    kb = k_ref[pl.ds(pl.multiple_of(blk0 * Q_BLOCK, Q_BLOCK), wb), :]
    vb = jnp.concatenate([_v_chunk(v_ref, blk0, nblk), jnp.ones((ONES_ROWS, wb), BF16)], axis=0)
    qall = jnp.concatenate(_padded_queries(q_ref[...]), axis=1)
    rel = ((i - blk0) * Q_BLOCK + lax.broadcasted_iota(jnp.int32, (wb, Q_BLOCK), 1)
           - lax.broadcasted_iota(jnp.int32, (wb, Q_BLOCK), 0))
    bias = jnp.where((rel >= 0) & (rel < window), 0.0, MASK_VAL)
    s = jnp.dot(kb, qall, preferred_element_type=F32) + _lane_tile(bias, N_KV_HEADS * GROUP)
    m = jnp.max(s, axis=0, keepdims=True)
    if use_sink:
        sink = sink_ref[0:1, :]
        m = jnp.maximum(m, sink)
    o = jnp.dot(vb, jnp.exp2(s - m).astype(BF16), preferred_element_type=F32)
    l = o[KV_WIDTH:KV_WIDTH + 1, :]
    if use_sink:
        l = l + jnp.exp2(sink - m)
    inv = 1.0 / l
    half = GROUP * Q_BLOCK
    for kv in range(N_KV_HEADS):
        for g in range(GROUP):
            hq = kv * GROUP + g
            lanes = slice(kv * half + g * Q_BLOCK, kv * half + (g + 1) * Q_BLOCK)
            o_ref[hq * HEAD_DIM:(hq + 1) * HEAD_DIM, :] = (
                o[kv * HEAD_DIM:(kv + 1) * HEAD_DIM, lanes] * inv[:, lanes]).astype(o_ref.dtype)


def _band(qT, q_mixer, k_rows, kv_index, vT128, sink_lanes, window, use_sink, out_dtype, B, T, name):
    nq = T // Q_BLOCK
    N = B * T
    n_prev = (window + Q_BLOCK - 2) // Q_BLOCK
    return pl.pallas_call(
        functools.partial(_band_kernel, n_prev=n_prev, window=window, use_sink=use_sink),
        grid=(B, nq),
        in_specs=[
            pl.BlockSpec((MIX_WIDTH, Q_BLOCK), lambda b, i: (q_mixer, b * nq + i)),
            pl.BlockSpec((T, KV_WIDTH), lambda b, i: (b, kv_index)),
            pl.BlockSpec((nq, KV_WIDTH, LANE), lambda b, i: (b, kv_index, 0)),
            pl.BlockSpec((8, N_Q_HEADS * Q_BLOCK), lambda b, i: (0, 0)),
        ],
        out_specs=pl.BlockSpec((MIX_WIDTH, Q_BLOCK), lambda b, i: (0, b * nq + i)),
        out_shape=jax.ShapeDtypeStruct((MIX_WIDTH, N), out_dtype),
        compiler_params=_cparams(("parallel", "arbitrary")),
        name=name,
    )(qT, k_rows, vT128, sink_lanes)


def _cmp_kernel(xa_ref, xb_ref, pe_ref, w1_ref, w2_ref, o_ref):
    half = (CMP_LEN // 2) * HEAD_DIM
    xa = (xa_ref[...].astype(F32) + pe_ref[0:1, :]).astype(BF16)
    xb = (xb_ref[...].astype(F32) + pe_ref[1:2, :]).astype(BF16)
    hid = (jnp.dot(xa, w1_ref[0:half, :], preferred_element_type=F32)
           + jnp.dot(xb, w1_ref[half:2 * half, :], preferred_element_type=F32))
    hid = jnp.maximum(hid, 0.0).astype(BF16)
    o_ref[...] = jnp.dot(hid, w2_ref[...], preferred_element_type=F32)


def _compress(chunks, pe, w1, w2, name):
    G, nc, half = chunks.shape
    nxt = jnp.concatenate([chunks[:, 1:], jnp.zeros((G, 1, half), chunks.dtype)], axis=1)
    pe2 = jnp.zeros((8, half), F32).at[0:2].set(pe.reshape(2, half))
    return pl.pallas_call(
        _cmp_kernel,
        grid=(G,),
        in_specs=[
            pl.BlockSpec((None, nc, half), lambda g: (g, 0, 0)),
            pl.BlockSpec((None, nc, half), lambda g: (g, 0, 0)),
            pl.BlockSpec((8, half), lambda g: (0, 0)),
            pl.BlockSpec((2 * half, CMP_HIDDEN), lambda g: (0, 0)),
            pl.BlockSpec((CMP_HIDDEN, HEAD_DIM), lambda g: (0, 0)),
        ],
        out_specs=pl.BlockSpec((None, nc, HEAD_DIM), lambda g: (g, 0, 0)),
        out_shape=jax.ShapeDtypeStruct((G, nc, HEAD_DIM), F32),
        compiler_params=_cparams(("parallel",)),
        name=name,
    )(chunks, nxt, pe2, w1.astype(BF16), w2.astype(BF16))


def _nsa_kernel(q_ref, kc_ref, vc_ref, ks_ref, vs_ref, misc_ref, ow_ref, ov_ref, e_ref, o_ref,
                imp_ref, s_ref, acc_ref, *, n_top):
    C = KEY_CHUNK
    i = pl.program_id(1)
    nch = i // (C // Q_BLOCK) + 1
    ncmp = kc_ref.shape[0]
    nsel = ov_ref.shape[0]
    lanes = GROUP * Q_BLOCK
    t = i * Q_BLOCK + lax.broadcasted_iota(jnp.int32, (1, Q_BLOCK), 1)
    t4 = i * Q_BLOCK + (lax.broadcasted_iota(jnp.int32, (1, lanes), 1) & (Q_BLOCK - 1))
    row = lax.broadcasted_iota(jnp.int32, (C, Q_BLOCK), 0)
    crow = lax.broadcasted_iota(jnp.int32, (ncmp, lanes), 0)
    jrow = lax.broadcasted_iota(jnp.int32, (nsel, Q_BLOCK), 0)
    cur = t // SLC_LEN
    qpad = _padded_queries(q_ref[...])
    kc = kc_ref[...]
    vcm = vc_ref[...]
    ov = ov_ref[...]

    o_cmps, selbs = [], []
    for kv in range(N_KV_HEADS):
        s = jnp.dot(kc, qpad[kv], preferred_element_type=F32)
        s = jnp.where(crow * CMP_STRIDE + (CMP_LEN - 1) <= t4, s, MASK_VAL)
        m = jnp.maximum(jnp.max(s, axis=0, keepdims=True), M_INIT)
        p = jnp.exp2(s - m)
        l = jnp.sum(p, axis=0, keepdims=True)
        p = p * jnp.where(l > 0.0, 1.0 / l, 0.0)
        o_cmp = jnp.dot(vcm, p.astype(BF16), preferred_element_type=F32)[kv * HEAD_DIM:(kv + 1) * HEAD_DIM, :]

        pg = p[:, 0:Q_BLOCK]
        for g in range(1, GROUP):
            pg = pg + p[:, g * Q_BLOCK:(g + 1) * Q_BLOCK]
        p_hi = pg.astype(BF16)
        p_lo = (pg - p_hi.astype(F32)).astype(BF16)
        imp = (jnp.dot(ov, p_hi, preferred_element_type=F32)
               + jnp.dot(ov, p_lo, preferred_element_type=F32))
        forced = (jrow == 0) | (jrow == cur) | (jrow == cur - 1)
        imp = jnp.where(forced, FORCE_SCORE, imp)
        imp = jnp.where(jrow * SLC_LEN <= t, imp, NEG_INF)
        imp_ref[...] = imp

        def rank_body(ii, rank):
            r = imp_ref[pl.ds(ii, 1), :]
            ahead = jnp.where(r > imp, 1.0, jnp.where((r == imp) & (ii < jrow), 1.0, 0.0))
            return rank + ahead
        rank = lax.fori_loop(0, nsel, rank_body, jnp.zeros((nsel, Q_BLOCK), F32), unroll=8)
        selbs.append(jnp.where(rank < float(n_top), 1.0, 0.0).astype(BF16))
        o_cmps.append(o_cmp)

    qall = jnp.concatenate(qpad, axis=1)
    selb_all = jnp.concatenate(selbs, axis=1)

    def sel_fn(off, size):
        mk = jnp.dot(e_ref[pl.ds(off, size), :], selb_all, preferred_element_type=F32)
        causal = (off + lax.broadcasted_iota(jnp.int32, (size, Q_BLOCK), 0)) <= t
        parts = []
        for kv in range(N_KV_HEADS):
            keep = jnp.where(causal, mk[:, kv * Q_BLOCK:(kv + 1) * Q_BLOCK], 0.0) > 0.5
            parts.append(_lane_tile(jnp.where(keep, 0.0, MASK_VAL), GROUP))
        return jnp.concatenate(parts, axis=1)

    o_slcs = _masked_attention(nch, sel_fn, ks_ref, vs_ref, qall, s_ref, acc_ref)

    for kv in range(N_KV_HEADS):
        for g in range(GROUP):
            hq = kv * GROUP + g
            gr = GATE_ROW0 + hq * 3
            gate = [1.0 / (1.0 + jnp.exp(-misc_ref[gr + br:gr + br + 1, :])) for br in range(3)]
            out = (gate[0] * o_cmps[kv][:, g * Q_BLOCK:(g + 1) * Q_BLOCK]
                   + gate[1] * o_slcs[kv][:, g * Q_BLOCK:(g + 1) * Q_BLOCK]
                   + gate[2] * ow_ref[hq * HEAD_DIM:(hq + 1) * HEAD_DIM, :])
            o_ref[hq * HEAD_DIM:(hq + 1) * HEAD_DIM, :] = out.astype(BF16)


def _nsa(qT, kcmp_rows, vcmpT, k_rows, vT128, miscT, o_winT, ovT, emat, B, T):
    nq = T // Q_BLOCK
    N = B * T
    ncmp = T // CMP_STRIDE
    nsel = T // SLC_LEN
    n_top = min(SLC_TOPN, nsel)
    return pl.pallas_call(
        functools.partial(_nsa_kernel, n_top=n_top),
        grid=(B, nq),
        in_specs=[
            pl.BlockSpec((MIX_WIDTH, Q_BLOCK), lambda b, i: (2, b * nq + i)),
            pl.BlockSpec((None, ncmp, KV_WIDTH), lambda b, i: (b, 0, 0)),
            pl.BlockSpec((None, KV_WIDTH, ncmp), lambda b, i: (b, 0, 0)),
            pl.BlockSpec((T, KV_WIDTH), lambda b, i: (b, 3)),
            pl.BlockSpec((nq, KV_WIDTH, LANE), lambda b, i: (b, 3, 0)),
            pl.BlockSpec((MISC_ROWS, Q_BLOCK), lambda b, i: (0, b * nq + i)),
            pl.BlockSpec((MIX_WIDTH, Q_BLOCK), lambda b, i: (0, b * nq + i)),
            pl.BlockSpec((nsel, ncmp), lambda b, i: (0, 0)),
            pl.BlockSpec((T, nsel), lambda b, i: (0, 0)),
        ],
        out_specs=pl.BlockSpec((MIX_WIDTH, Q_BLOCK), lambda b, i: (0, b * nq + i)),
        out_shape=jax.ShapeDtypeStruct((MIX_WIDTH, N), BF16),
        scratch_shapes=[
            pltpu.VMEM((nsel, Q_BLOCK), F32),
            pltpu.VMEM((T, N_KV_HEADS * GROUP * Q_BLOCK), F32),
            pltpu.VMEM((KV_WIDTH + ONES_ROWS, N_KV_HEADS * GROUP * Q_BLOCK), F32),
        ],
        compiler_params=_cparams(("parallel", "arbitrary")),
        name="nsa_attention",
    )(qT, kcmp_rows, vcmpT, k_rows, vT128, miscT, o_winT, ovT, emat)


def _merge_kernel(x_ref, g_ref, wg_ref, oa_ref, ob_ref, oc_ref, wb_ref, wo_ref, out_ref):
    tm = x_ref.shape[1]
    x = x_ref[...]
    ss = jnp.sum(x * x, axis=0, keepdims=True)
    h = (x * lax.rsqrt(ss * (1.0 / D_MODEL) + EPS) * _lane_tile(g_ref[...], tm // LANE)).astype(BF16)
    acc = None
    for n, o_ref in enumerate((oa_ref, ob_ref, oc_ref)):
        gm = jnp.dot(wg_ref[n * D_MODEL:(n + 1) * D_MODEL, :], h, preferred_element_type=F32)
        y = jnp.dot(wb_ref[n], o_ref[...], preferred_element_type=F32)
        term = y * (1.0 / (1.0 + jnp.exp(-gm)))
        acc = term if acc is None else acc + term
    out_ref[...] = x + jnp.dot(wo_ref[...], acc.astype(BF16), preferred_element_type=F32)


def _merge(xT, g_b, wgT, oa, ob, oc, wbT, woT):
    N = xT.shape[1]
    tm = 512
    col = lambda i: (0, i)
    return pl.pallas_call(
        _merge_kernel,
        grid=(N // tm,),
        in_specs=[
            pl.BlockSpec((D_MODEL, tm), col),
            pl.BlockSpec((D_MODEL, LANE), lambda i: (0, 0)),
            pl.BlockSpec((N_MIXERS * D_MODEL, D_MODEL), lambda i: (0, 0)),
            pl.BlockSpec((MIX_WIDTH, tm), col),
            pl.BlockSpec((MIX_WIDTH, tm), col),
            pl.BlockSpec((MIX_WIDTH, tm), col),
            pl.BlockSpec((N_MIXERS, D_MODEL, MIX_WIDTH), lambda i: (0, 0, 0)),
            pl.BlockSpec((D_MODEL, D_MODEL), lambda i: (0, 0)),
        ],
        out_specs=pl.BlockSpec((D_MODEL, tm), col),
        out_shape=jax.ShapeDtypeStruct((D_MODEL, N), F32),
        compiler_params=_cparams(("parallel",)),
        name="merge_out_projection",
    )(xT, g_b, wgT, oa, ob, oc, wbT, woT)


def _mlp_kernel(x_ref, g_ref, wu_ref, wd_ref, out_ref, h_ref, acc_ref):
    f = pl.program_id(1)
    tm = x_ref.shape[1]

    @pl.when(f == 0)
    def _():
        x = x_ref[...]
        ss = jnp.sum(x * x, axis=0, keepdims=True)
        h_ref[...] = (x * lax.rsqrt(ss * (1.0 / D_MODEL) + EPS)
                      * _lane_tile(g_ref[...], tm // LANE)).astype(BF16)
        acc_ref[...] = jnp.zeros(acc_ref.shape, F32)

    u = jnp.maximum(jnp.dot(wu_ref[...], h_ref[...], preferred_element_type=F32), 0.0)
    acc_ref[...] += jnp.dot(wd_ref[...], (u * u).astype(BF16), preferred_element_type=F32)

    @pl.when(f == pl.num_programs(1) - 1)
    def _():
        out_ref[...] = x_ref[...] + acc_ref[...]


def _mlp(xT, g_b, wuT, wdT):
    N = xT.shape[1]
    tm, tf = 512, 1024
    return pl.pallas_call(
        _mlp_kernel,
        grid=(N // tm, D_FF // tf),
        in_specs=[
            pl.BlockSpec((D_MODEL, tm), lambda i, f: (0, i)),
            pl.BlockSpec((D_MODEL, LANE), lambda i, f: (0, 0)),
            pl.BlockSpec((tf, D_MODEL), lambda i, f: (f, 0)),
            pl.BlockSpec((D_MODEL, tf), lambda i, f: (0, f)),
        ],
        out_specs=pl.BlockSpec((D_MODEL, tm), lambda i, f: (0, i)),
        out_shape=jax.ShapeDtypeStruct((D_MODEL, N), F32),
        scratch_shapes=[pltpu.VMEM((D_MODEL, tm), BF16), pltpu.VMEM((D_MODEL, tm), F32)],
        compiler_params=_cparams(("parallel", "arbitrary")),
        name="relu2_mlp",
    )(xT, g_b, wuT, wdT)


def _to_chunks(rows, B, T):
    half_tokens = CMP_LEN // 2
    t = rows.reshape(B, T, N_KV_HEADS, HEAD_DIM).transpose(0, 2, 1, 3)
    return t.reshape(B * N_KV_HEADS, T // half_tokens, half_tokens * HEAD_DIM)


def kernel(x, norm_mix, w_in, q_norm, k_norm, sinks, cmp_pe_k, cmp_pe_v, w_ck1, w_ck2, w_cv1, w_cv2,
           w_branch, w_out, norm_mlp, w_up, w_down):
    B, T, D = x.shape
    N = B * T
    depth = w_in.shape[0]
    nq = T // Q_BLOCK
    ncmp = T // CMP_STRIDE
    nsel = T // SLC_LEN

    inv_freq = ROPE_THETA ** (-jnp.arange(0, HEAD_DIM, 2, dtype=F32) / HEAD_DIM)
    ang = jnp.arange(T, dtype=F32)[:, None] * inv_freq[None, :]
    cosT, sinT = jnp.cos(ang).T, jnp.sin(ang).T

    ci = np.arange(KEY_CHUNK)
    ltri = jnp.asarray(ci[None, :] < ci[:, None], BF16)
    emat = jnp.asarray((np.arange(T)[:, None] // SLC_LEN) == np.arange(nsel)[None, :], BF16)
    cstart = np.arange(ncmp) * CMP_STRIDE
    sstart = np.arange(nsel) * SLC_LEN
    n_cmp_valid = (T - CMP_LEN) // CMP_STRIDE + 1
    ov = ((cstart[None, :] < sstart[:, None] + SLC_LEN) & (cstart[None, :] + CMP_LEN > sstart[:, None])
          & (np.arange(ncmp)[None, :] < n_cmp_valid))
    ovT = jnp.asarray(ov, BF16)

    xT = x.reshape(N, D).T

    sl = lambda s: w_in[:, :, s[0]:s[1]]
    wT_all = jnp.concatenate(
        [sl(S_QA), sl(S_QB), sl(S_QC),
         sl(S_KA), sl(S_KB), sl(S_KC), sl(S_KSL), sl(S_KWN),
         sl(S_VA), sl(S_VB), sl(S_VC), sl(S_VSL), sl(S_VWN),
         sl(S_IQ), sl(S_IK), sl(S_IW), jnp.zeros((depth, D, GATE_ROW0 - IDX_HEADS), w_in.dtype), sl(S_GC)],
        axis=2).transpose(0, 2, 1).astype(BF16)
    wgT_all = sl(S_GM).transpose(0, 2, 1).astype(BF16)
    wbT_all = w_branch.transpose(0, 1, 3, 2).astype(BF16)
    woT_all = w_out.transpose(0, 2, 1).astype(BF16)
    wuT_all = w_up.transpose(0, 2, 1).astype(BF16)
    wdT_all = w_down.transpose(0, 2, 1).astype(BF16)
    lane_b = lambda v: jnp.broadcast_to(v.astype(F32)[..., None], v.shape + (LANE,))
    gmix_all, gmlp_all = lane_b(norm_mix), lane_b(norm_mlp)
    gq_all = lane_b(jnp.tile(q_norm[:, :, None, :], (1, 1, N_Q_HEADS, 1)).reshape(depth, -1))
    gk_all = lane_b(jnp.tile(k_norm[:, jnp.array([0, 1, 2, 2, 2])][:, :, None, :],
                             (1, 1, N_KV_HEADS, 1)).reshape(depth, -1))
    sink_all = jnp.broadcast_to(jnp.repeat(sinks.astype(F32) * LOG2E, Q_BLOCK, axis=1)[:, None, :],
                                (depth, 8, N_Q_HEADS * Q_BLOCK))

    for l in range(depth):
        wgT, sink_lanes = wgT_all[l], sink_all[l]
        qT, kT, vT128, iqT, ikT, miscT = _project(
            xT, gmix_all[l], wT_all[l], gq_all[l], gk_all[l], cosT, sinT, T)
        k_rows = kT.T
        ik_rows = ikT.T

        o_a = _dsa(iqT, miscT, ik_rows, qT, k_rows, vT128, ltri, B, T)
        o_b = _band(qT, 1, k_rows, 1, vT128, sink_lanes, SWA_WINDOW, True, BF16, B, T, "swa_attention")
        o_w = _band(qT, 2, k_rows, 4, vT128, sink_lanes, NSA_WINDOW, False, F32, B, T, "nsa_window_attention")

        kc_chunks = _to_chunks(k_rows[:, 2 * KV_WIDTH:3 * KV_WIDTH].reshape(B, T, KV_WIDTH), B, T)
        vc_rows = vT128[:, 2 * KV_WIDTH:3 * KV_WIDTH, :].reshape(B, nq, KV_WIDTH, LANE)
        vc_rows = vc_rows.transpose(0, 1, 3, 2).reshape(B, T, KV_WIDTH)
        vc_chunks = _to_chunks(vc_rows, B, T)
        k_cmp = _compress(kc_chunks, cmp_pe_k[l], w_ck1[l], w_ck2[l], "compress_k")
        v_cmp = _compress(vc_chunks, cmp_pe_v[l], w_cv1[l], w_cv2[l], "compress_v")
        kcmp_rows = k_cmp.reshape(B, N_KV_HEADS, ncmp, HEAD_DIM).transpose(0, 2, 1, 3)
        kcmp_rows = kcmp_rows.reshape(B, ncmp, KV_WIDTH).astype(BF16)
        vcmpT = v_cmp.reshape(B, N_KV_HEADS, ncmp, HEAD_DIM).transpose(0, 1, 3, 2)
        vcmpT = vcmpT.reshape(B, KV_WIDTH, ncmp).astype(BF16)

        o_c = _nsa(qT, kcmp_rows, vcmpT, k_rows, vT128, miscT, o_w, ovT, emat, B, T)

        xT = _merge(xT, gmix_all[l], wgT, o_a, o_b, o_c, wbT_all[l], woT_all[l])
        xT = _mlp(xT, gmlp_all[l], wuT_all[l], wdT_all[l])

    return xT.T.reshape(B, T, D)
```

```python
import functools

import numpy as np
import jax
import jax.numpy as jnp
from jax import lax
from jax.experimental import pallas as pl
from jax.experimental.pallas import tpu as pltpu

F32 = jnp.float32
BF16 = jnp.bfloat16

D_MODEL = 1024
HEAD_DIM = 64
HALF = HEAD_DIM // 2
N_Q_HEADS = 8
N_KV_HEADS = 2
GROUP = N_Q_HEADS // N_KV_HEADS
MIX_WIDTH = N_Q_HEADS * HEAD_DIM
KV_WIDTH = N_KV_HEADS * HEAD_DIM
N_MIXERS = 3
D_FF = 4 * D_MODEL
ROPE_THETA = 10000.0
EPS = 1e-6
NEG_INF = -1e30
MASK_VAL = -2e30
M_INIT = -1e30
BIG = 3e38
Q_BLOCK = 128
KEY_CHUNK = 512
BISECT_FIXED_STEPS = 12
BISECT_F32_FIXED_STEPS = 2
ONES_ROWS = 16
IDX_HEADS = 4
IDX_DIM = 64
DSA_TOPK_MAX = 256
SWA_WINDOW = 128
CMP_LEN = 32
CMP_STRIDE = 16
CMP_HIDDEN = 256
SLC_LEN = 64
SLC_TOPN = 16
NSA_WINDOW = 512
FORCE_SCORE = 1e9
LOG2E = 1.4426950408889634
Q_SCALE = (HEAD_DIM ** -0.5) * LOG2E
LANE = 128
LANE_SHIFT = 7
ATT_BIG_STEP = 1024
VMEM_LIMIT = 56 * 1024 * 1024

IN_SIZES = (
    MIX_WIDTH, KV_WIDTH, KV_WIDTH, IDX_HEADS * IDX_DIM, IDX_DIM, IDX_HEADS,
    MIX_WIDTH, KV_WIDTH, KV_WIDTH,
    MIX_WIDTH, KV_WIDTH, KV_WIDTH, KV_WIDTH, KV_WIDTH, KV_WIDTH, KV_WIDTH, 3 * N_Q_HEADS,
    N_MIXERS * D_MODEL,
)
_OFF = np.concatenate([[0], np.cumsum(IN_SIZES)]).astype(int)
(S_QA, S_KA, S_VA, S_IQ, S_IK, S_IW, S_QB, S_KB, S_VB,
 S_QC, S_KC, S_VC, S_KSL, S_VSL, S_KWN, S_VWN, S_GC, S_GM) = [
    (int(_OFF[i]), int(_OFF[i + 1])) for i in range(len(IN_SIZES))]

R_Q = 0
R_K = R_Q + N_MIXERS * MIX_WIDTH
R_V = R_K + 5 * KV_WIDTH
R_IQ = R_V + 5 * KV_WIDTH
R_IK = R_IQ + IDX_HEADS * IDX_DIM
R_MISC = R_IK + IDX_DIM
MISC_ROWS = 32
GATE_ROW0 = 8
R_END = R_MISC + MISC_ROWS
PROJ_TM = 1024


def _cparams(sem):
    return pltpu.CompilerParams(dimension_semantics=sem, vmem_limit_bytes=VMEM_LIMIT)


def _lane_tile(a, n):
    return a if n == 1 else jnp.concatenate([a] * n, axis=1)


def _proj_kernel(x_ref, g_ref, w_ref, gq_ref, gk_ref, cos_ref, sin_ref,
                 q_ref, k_ref, v_ref, iq_ref, ik_ref, misc_ref):
    tm = x_ref.shape[1]
    rep = tm // LANE
    x = x_ref[...]
    ss = jnp.sum(x * x, axis=0, keepdims=True)
    h = (x * lax.rsqrt(ss * (1.0 / D_MODEL) + EPS) * _lane_tile(g_ref[...], rep)).astype(BF16)
    cos = cos_ref[...]
    sin = sin_ref[...]

    def rope(z):
        z1, z2 = z[:HALF], z[HALF:]
        return jnp.concatenate([z1 * cos - z2 * sin, z2 * cos + z1 * sin], axis=0)

    def head_norm(z, gain):
        ssq = jnp.sum(z * z, axis=0, keepdims=True)
        return z * lax.rsqrt(ssq * (1.0 / HEAD_DIM) + EPS) * _lane_tile(gain, rep)

    for m in range(N_MIXERS):
        z = jnp.dot(w_ref[R_Q + m * MIX_WIDTH:R_Q + (m + 1) * MIX_WIDTH, :], h,
                    preferred_element_type=F32)
        for hh in range(N_Q_HEADS):
            r0 = m * MIX_WIDTH + hh * HEAD_DIM
            zh = head_norm(z[hh * HEAD_DIM:(hh + 1) * HEAD_DIM], gq_ref[r0:r0 + HEAD_DIM, :])
            q_ref[r0:r0 + HEAD_DIM, :] = (rope(zh) * Q_SCALE).astype(BF16)

    z = jnp.dot(w_ref[R_K:R_V, :], h, preferred_element_type=F32)
    for hh in range(5 * N_KV_HEADS):
        r0 = hh * HEAD_DIM
        zh = head_norm(z[r0:r0 + HEAD_DIM], gk_ref[r0:r0 + HEAD_DIM, :])
        k_ref[r0:r0 + HEAD_DIM, :] = rope(zh).astype(BF16)

    z = jnp.dot(w_ref[R_V:R_IQ, :], h, preferred_element_type=F32)
    for j in range(rep):
        v_ref[j] = z[:, j * LANE:(j + 1) * LANE].astype(BF16)

    z = jnp.dot(w_ref[R_IQ:R_END, :], h, preferred_element_type=F32)
    for hh in range(IDX_HEADS):
        r0 = hh * IDX_DIM
        iq_ref[r0:r0 + IDX_DIM, :] = rope(z[r0:r0 + IDX_DIM])
    ik_ref[...] = rope(z[R_IK - R_IQ:R_MISC - R_IQ])
    misc_ref[...] = z[R_MISC - R_IQ:R_END - R_IQ]


def _project(xT, g_b, wT, gq_b, gk_b, cosT, sinT, T):
    N = xT.shape[1]
    tm = PROJ_TM
    tpb = T // tm
    rep = tm // LANE
    col = lambda i: (0, i)
    const = lambda i: (0, 0)
    return pl.pallas_call(
        _proj_kernel,
        grid=(N // tm,),
        in_specs=[
            pl.BlockSpec((D_MODEL, tm), col),
            pl.BlockSpec((D_MODEL, LANE), const),
            pl.BlockSpec((R_END, D_MODEL), const),
            pl.BlockSpec((N_MIXERS * MIX_WIDTH, LANE), const),
            pl.BlockSpec((5 * KV_WIDTH, LANE), const),
            pl.BlockSpec((HALF, tm), lambda i: (0, i % tpb)),
            pl.BlockSpec((HALF, tm), lambda i: (0, i % tpb)),
        ],
        out_specs=[
            pl.BlockSpec((N_MIXERS * MIX_WIDTH, tm), col),
            pl.BlockSpec((5 * KV_WIDTH, tm), col),
            pl.BlockSpec((rep, 5 * KV_WIDTH, LANE), lambda i: (i, 0, 0)),
            pl.BlockSpec((IDX_HEADS * IDX_DIM, tm), col),
            pl.BlockSpec((IDX_DIM, tm), col),
            pl.BlockSpec((MISC_ROWS, tm), col),
        ],
        out_shape=[
            jax.ShapeDtypeStruct((N_MIXERS * MIX_WIDTH, N), BF16),
            jax.ShapeDtypeStruct((5 * KV_WIDTH, N), BF16),
            jax.ShapeDtypeStruct((N // LANE, 5 * KV_WIDTH, LANE), BF16),
            jax.ShapeDtypeStruct((IDX_HEADS * IDX_DIM, N), F32),
            jax.ShapeDtypeStruct((IDX_DIM, N), F32),
            jax.ShapeDtypeStruct((MISC_ROWS, N), F32),
        ],
        compiler_params=_cparams(("parallel",)),
        name="in_projection",
    )(xT, g_b, wT, gq_b, gk_b, cosT, sinT)


def _padded_queries(q):
    zeros = jnp.zeros((HEAD_DIM, Q_BLOCK), q.dtype)
    out = []
    for kv in range(N_KV_HEADS):
        cols = []
        for g in range(GROUP):
            hq = kv * GROUP + g
            qh = q[hq * HEAD_DIM:(hq + 1) * HEAD_DIM, :]
            cols.append(jnp.concatenate([qh, zeros] if kv == 0 else [zeros, qh], axis=0))
        out.append(jnp.concatenate(cols, axis=1))
    return out


def _v_chunk(v_ref, blk0, nblk):
    return jnp.concatenate([v_ref[blk0 + j] for j in range(nblk)], axis=1)


def _masked_attention(nch, sel_fn, k_ref, v_ref, qall, s_ref, acc_ref):
    C = KEY_CHUNK
    CB = ATT_BIG_STEP
    lanes = N_KV_HEADS * GROUP * Q_BLOCK
    nbig = nch // (CB // C)
    nsmall = nch - nbig * (CB // C)
    small0 = nbig * CB

    def walk(fn, carry):
        carry = lax.fori_loop(0, nbig, lambda c, x: fn(pl.multiple_of(c * CB, CB), CB, x), carry)
        return lax.fori_loop(0, nsmall, lambda c, x: fn(pl.multiple_of(small0 + c * C, C), C, x), carry)

    def qk_step(off, size, mx):
        s = jnp.dot(k_ref[pl.ds(off, size), :], qall, preferred_element_type=F32) + sel_fn(off, size)
        s_ref[pl.ds(off, size), :] = s
        return jnp.maximum(mx, jnp.max(s, axis=0, keepdims=True))

    mx = walk(qk_step, jnp.full((1, lanes), M_INIT, F32))
    acc_ref[...] = jnp.zeros(acc_ref.shape, F32)

    def pv_step(off, size, carry):
        p = jnp.exp2(s_ref[pl.ds(off, size), :] - mx).astype(BF16)
        vc = jnp.concatenate([_v_chunk(v_ref, lax.shift_right_logical(off, LANE_SHIFT), size // LANE),
                              jnp.ones((ONES_ROWS, size), BF16)], axis=0)
        acc_ref[...] += jnp.dot(vc, p, preferred_element_type=F32)
        return carry

    walk(pv_step, 0)
    l = acc_ref[KV_WIDTH:KV_WIDTH + 1, :]
    inv = jnp.where(l > 0.0, 1.0 / l, 0.0)
    half = GROUP * Q_BLOCK
    return [acc_ref[kv * HEAD_DIM:(kv + 1) * HEAD_DIM, kv * half:(kv + 1) * half]
            * inv[:, kv * half:(kv + 1) * half] for kv in range(N_KV_HEADS)]


def _dsa_kernel(iq_ref, misc_ref, ik_ref, q_ref, k_ref, v_ref, ltri_ref, o_ref,
                sc_ref, sb_ref, s_ref, acc_ref, *, k_top):
    C = KEY_CHUNK
    i = pl.program_id(1)
    nch = i // (C // Q_BLOCK) + 1
    qpos = i * Q_BLOCK + lax.broadcasted_iota(jnp.int32, (1, Q_BLOCK), 1)
    row = lax.broadcasted_iota(jnp.int32, (C, Q_BLOCK), 0)

    iq = iq_ref[...].astype(BF16)
    iq_cat = jnp.concatenate([iq[h * IDX_DIM:(h + 1) * IDX_DIM] for h in range(IDX_HEADS)], axis=1)
    w = misc_ref[0:IDX_HEADS, :] * ((IDX_HEADS ** -0.5) * (IDX_DIM ** -0.5))

    def fold(x):
        return x.reshape(C // 8, 8, Q_BLOCK)

    part = lambda v: jnp.full((8, Q_BLOCK), v, F32)

    def score_chunk(c, carry):
        lo, hi = carry
        off = pl.multiple_of(c * C, C)
        ikc = ik_ref[pl.ds(off, C), :].astype(BF16)
        d = jnp.dot(ikc, iq_cat, preferred_element_type=F32)
        s = jnp.maximum(d[:, 0:Q_BLOCK], 0.0) * w[0:1, :]
        for h in range(1, IDX_HEADS):
            s = s + jnp.maximum(d[:, h * Q_BLOCK:(h + 1) * Q_BLOCK], 0.0) * w[h:h + 1, :]
        causal = (off + row) <= qpos
        masked = jnp.where(causal, s, NEG_INF)
        sc_ref[pl.ds(off, C), :] = masked
        sb_ref[pl.ds(off, C), :] = masked.astype(BF16)
        lo = jnp.minimum(lo, jnp.min(fold(jnp.where(causal, s, BIG)), axis=0))
        hi = jnp.maximum(hi, jnp.max(fold(jnp.where(causal, s, -BIG)), axis=0))
        return lo, hi

    lo0, hi0 = lax.fori_loop(0, nch, score_chunk, (part(BIG), part(-BIG)))
    lo0 = jnp.min(lo0, axis=0, keepdims=True)
    hi0 = jnp.max(hi0, axis=0, keepdims=True)

    kf = float(k_top)
    select = i * Q_BLOCK >= k_top

    def tree(op, xs):
        while len(xs) > 1:
            xs = [op(xs[j], xs[j + 1]) for j in range(0, len(xs), 2)]
        return xs[0]

    def stats16(mid):
        mid16 = jnp.broadcast_to(mid, (16, Q_BLOCK)).astype(BF16)
        one, zero = jnp.ones((16, Q_BLOCK), BF16), jnp.zeros((16, Q_BLOCK), BF16)
        big = jnp.full((16, Q_BLOCK), BIG, BF16)

        def body(c, carry):
            cnt, a, b = carry
            for j in range(C // 128):
                off = pl.multiple_of(c * C + j * 128, 128)
                s = sb_ref[pl.ds(off, 128), :].reshape(8, 16, Q_BLOCK)
                rows = [s[r] for r in range(8)]
                gts = [x > mid16 for x in rows]
                cnt = cnt + tree(jnp.add, [jnp.where(g, one, zero) for g in gts]).astype(F32)
                a = jnp.minimum(a, tree(jnp.minimum, [jnp.where(g, x, big) for g, x in zip(gts, rows)]))
                b = jnp.maximum(b, tree(jnp.maximum, [jnp.where(g, -big, x) for g, x in zip(gts, rows)]))
            return cnt, a, b

        cnt, a, b = lax.fori_loop(0, nch, body, (jnp.zeros((16, Q_BLOCK), F32), big, -big))
        return (jnp.sum(cnt, axis=0, keepdims=True), jnp.min(a.astype(F32), axis=0, keepdims=True),
                jnp.max(b.astype(F32), axis=0, keepdims=True))

    def stats(mid):
        mid8 = jnp.broadcast_to(mid, (8, Q_BLOCK))[None]

        def body(c, carry):
            cnt, a, b = carry
            for j in range(C // 64):
                off = pl.multiple_of(c * C + j * 64, 64)
                s = sc_ref[pl.ds(off, 64), :].reshape(8, 8, Q_BLOCK)
                gt = s > mid8
                cnt = cnt + jnp.sum(jnp.where(gt, 1.0, 0.0), axis=0)
                a = jnp.minimum(a, jnp.min(jnp.where(gt, s, BIG), axis=0))
                b = jnp.maximum(b, jnp.max(jnp.where(gt, -BIG, s), axis=0))
            return cnt, a, b

        cnt, a, b = lax.fori_loop(0, nch, body, (part(0.0), part(BIG), part(-BIG)))
        return (jnp.sum(cnt, axis=0, keepdims=True), jnp.min(a, axis=0, keepdims=True),
                jnp.max(b, axis=0, keepdims=True))

    def make_step(stats_fn, rounded):
        def step(st):
            lo, hi, clo, chi = st
            mid = lo + (hi - lo) * 0.5
            if rounded:
                mid = mid.astype(BF16).astype(F32)
            mid = jnp.maximum(mid, lo)
            mid = jnp.where(mid >= hi, lo, mid)
            cnt, a, b = stats_fn(mid)
            up = cnt >= kf
            return (jnp.where(up, a, lo), jnp.where(up, hi, b),
                    jnp.where(up, cnt, clo), jnp.where(up, chi, cnt))
        return step

    def cond(st):
        return jnp.logical_and(select, jnp.max(st[1] - st[0]) > 0.0)

    def search(step, st, fixed_steps):
        st = lax.fori_loop(0, jnp.where(select, fixed_steps, 0), lambda _, s: step(s), st)
        return lax.while_loop(cond, lambda s: step(step(s)), st)

    n_causal = (qpos + 1).astype(F32)
    rnd = lambda v: v.astype(BF16).astype(F32)
    beta, _, clo, chi = search(make_step(stats16, True),
                               (rnd(lo0), rnd(hi0), n_causal, jnp.zeros((1, Q_BLOCK), F32)),
                               BISECT_FIXED_STEPS)

    def bracket_chunk(c, carry):
        lo, hi = carry
        off = pl.multiple_of(c * C, C)
        s = sc_ref[pl.ds(off, C), :]
        eq = sb_ref[pl.ds(off, C), :].astype(F32) == beta
        lo = jnp.minimum(lo, jnp.min(fold(jnp.where(eq, s, BIG)), axis=0))
        hi = jnp.maximum(hi, jnp.max(fold(jnp.where(eq, s, -BIG)), axis=0))
        return lo, hi

    lo1, hi1 = lax.fori_loop(0, jnp.where(select, nch, 0), bracket_chunk, (part(BIG), part(-BIG)))
    lo1 = jnp.min(lo1, axis=0, keepdims=True)
    hi1 = jnp.max(hi1, axis=0, keepdims=True)
    lo, _, clo, chi = search(make_step(stats, False), (lo1, hi1, clo, chi), BISECT_F32_FIXED_STEPS)
    thr = jnp.where(select, lo, 0.5 * NEG_INF)
    need = jnp.where(select, kf - chi, 0.0)
    ties = jnp.logical_and(select, jnp.max((clo - chi) - need) > 0.0)

    @pl.when(ties)
    def _():
        def mask_chunk(c, tie_carry):
            off = pl.multiple_of(c * C, C)
            s_idx = sc_ref[pl.ds(off, C), :]
            eqf = jnp.where(s_idx == thr, 1.0, 0.0)
            before = jnp.dot(ltri_ref[...], eqf.astype(BF16), preferred_element_type=F32) + tie_carry
            sc_ref[pl.ds(off, C), :] = jnp.where(s_idx > thr, 1.0, jnp.where(before < need, eqf, 0.0))
            return tie_carry + jnp.sum(eqf, axis=0, keepdims=True)

        lax.fori_loop(0, nch, mask_chunk, jnp.zeros((1, Q_BLOCK), F32))

    thr_sel = jnp.where(ties, 0.5, thr)
    qall = jnp.concatenate(_padded_queries(q_ref[...]), axis=1)

    def sel_fn(off, size):
        bias = jnp.where(sc_ref[pl.ds(off, size), :] >= thr_sel, 0.0, MASK_VAL)
        return _lane_tile(bias, N_KV_HEADS * GROUP)

    outs = _masked_attention(nch, sel_fn, k_ref, v_ref, qall, s_ref, acc_ref)
    for kv in range(N_KV_HEADS):
        o = outs[kv]
        for g in range(GROUP):
            hq = kv * GROUP + g
            o_ref[hq * HEAD_DIM:(hq + 1) * HEAD_DIM, :] = o[:, g * Q_BLOCK:(g + 1) * Q_BLOCK].astype(BF16)


def _dsa(iqT, miscT, ik_rows, qT, k_rows, vT128, ltri, B, T):
    nq = T // Q_BLOCK
    k_top = min(DSA_TOPK_MAX, T // 4)
    N = B * T
    return pl.pallas_call(
        functools.partial(_dsa_kernel, k_top=k_top),
        grid=(B, nq),
        in_specs=[
            pl.BlockSpec((IDX_HEADS * IDX_DIM, Q_BLOCK), lambda b, i: (0, b * nq + i)),
            pl.BlockSpec((MISC_ROWS, Q_BLOCK), lambda b, i: (0, b * nq + i)),
            pl.BlockSpec((T, IDX_DIM), lambda b, i: (b, 0)),
            pl.BlockSpec((MIX_WIDTH, Q_BLOCK), lambda b, i: (0, b * nq + i)),
            pl.BlockSpec((T, KV_WIDTH), lambda b, i: (b, 0)),
            pl.BlockSpec((nq, KV_WIDTH, LANE), lambda b, i: (b, 0, 0)),
            pl.BlockSpec((KEY_CHUNK, KEY_CHUNK), lambda b, i: (0, 0)),
        ],
        out_specs=pl.BlockSpec((MIX_WIDTH, Q_BLOCK), lambda b, i: (0, b * nq + i)),
        out_shape=jax.ShapeDtypeStruct((MIX_WIDTH, N), BF16),
        scratch_shapes=[
            pltpu.VMEM((T, Q_BLOCK), F32),
            pltpu.VMEM((T, Q_BLOCK), BF16),
            pltpu.VMEM((T, N_KV_HEADS * GROUP * Q_BLOCK), F32),
            pltpu.VMEM((KV_WIDTH + ONES_ROWS, N_KV_HEADS * GROUP * Q_BLOCK), F32),
        ],
        compiler_params=_cparams(("parallel", "arbitrary")),
        name="dsa_attention",
    )(iqT, miscT, ik_rows, qT, k_rows, vT128, ltri)


def _band_kernel(q_ref, k_ref, v_ref, sink_ref, o_ref, *, n_prev, window, use_sink):
    i = pl.program_id(1)
    blk0 = jnp.maximum(i - n_prev, 0)
    nblk = n_prev + 1
    wb = nblk * Q_BLOCK
    kb = k_ref[pl.ds(pl.multiple_of(blk0 * Q_BLOCK, Q_BLOCK), wb), :]
    vb = jnp.concatenate([_v_chunk(v_ref, blk0, nblk), jnp.ones((ONES_ROWS, wb), BF16)], axis=0)
    qall = jnp.concatenate(_padded_queries(q_ref[...]), axis=1)
    rel = ((i - blk0) * Q_BLOCK + lax.broadcasted_iota(jnp.int32, (wb, Q_BLOCK), 1)
           - lax.broadcasted_iota(jnp.int32, (wb, Q_BLOCK), 0))
    bias = jnp.where((rel >= 0) & (rel < window), 0.0, MASK_VAL)
    s = jnp.dot(kb, qall, preferred_element_type=F32) + _lane_tile(bias, N_KV_HEADS * GROUP)
    m = jnp.max(s, axis=0, keepdims=True)
    if use_sink:
        sink = sink_ref[0:1, :]
        m = jnp.maximum(m, sink)
    o = jnp.dot(vb, jnp.exp2(s - m).astype(BF16), preferred_element_type=F32)
    l = o[KV_WIDTH:KV_WIDTH + 1, :]
    if use_sink:
        l = l + jnp.exp2(sink - m)
    inv = 1.0 / l
    half = GROUP * Q_BLOCK
    for kv in range(N_KV_HEADS):
        for g in range(GROUP):
            hq = kv * GROUP + g
            lanes = slice(kv * half + g * Q_BLOCK, kv * half + (g + 1) * Q_BLOCK)
            o_ref[hq * HEAD_DIM:(hq + 1) * HEAD_DIM, :] = (
                o[kv * HEAD_DIM:(kv + 1) * HEAD_DIM, lanes] * inv[:, lanes]).astype(o_ref.dtype)


def _band(qT, q_mixer, k_rows, kv_index, vT128, sink_lanes, window, use_sink, out_dtype, B, T, name):
    nq = T // Q_BLOCK
    N = B * T
    n_prev = (window + Q_BLOCK - 2) // Q_BLOCK
    return pl.pallas_call(
        functools.partial(_band_kernel, n_prev=n_prev, window=window, use_sink=use_sink),
        grid=(B, nq),
        in_specs=[
            pl.BlockSpec((MIX_WIDTH, Q_BLOCK), lambda b, i: (q_mixer, b * nq + i)),
            pl.BlockSpec((T, KV_WIDTH), lambda b, i: (b, kv_index)),
            pl.BlockSpec((nq, KV_WIDTH, LANE), lambda b, i: (b, kv_index, 0)),
            pl.BlockSpec((8, N_Q_HEADS * Q_BLOCK), lambda b, i: (0, 0)),
        ],
        out_specs=pl.BlockSpec((MIX_WIDTH, Q_BLOCK), lambda b, i: (0, b * nq + i)),
        out_shape=jax.ShapeDtypeStruct((MIX_WIDTH, N), out_dtype),
        compiler_params=_cparams(("parallel", "arbitrary")),
        name=name,
    )(qT, k_rows, vT128, sink_lanes)


def _cmp_kernel(xa_ref, xb_ref, pe_ref, w1_ref, w2_ref, o_ref):
    half = (CMP_LEN // 2) * HEAD_DIM
    xa = (xa_ref[...].astype(F32) + pe_ref[0:1, :]).astype(BF16)
    xb = (xb_ref[...].astype(F32) + pe_ref[1:2, :]).astype(BF16)
    hid = (jnp.dot(xa, w1_ref[0:half, :], preferred_element_type=F32)
           + jnp.dot(xb, w1_ref[half:2 * half, :], preferred_element_type=F32))
    hid = jnp.maximum(hid, 0.0).astype(BF16)
    o_ref[...] = jnp.dot(hid, w2_ref[...], preferred_element_type=F32)


def _compress(chunks, pe, w1, w2, name):
    G, nc, half = chunks.shape
    nxt = jnp.concatenate([chunks[:, 1:], jnp.zeros((G, 1, half), chunks.dtype)], axis=1)
    pe2 = jnp.zeros((8, half), F32).at[0:2].set(pe.reshape(2, half))
    return pl.pallas_call(
        _cmp_kernel,
        grid=(G,),
        in_specs=[
            pl.BlockSpec((None, nc, half), lambda g: (g, 0, 0)),
            pl.BlockSpec((None, nc, half), lambda g: (g, 0, 0)),
            pl.BlockSpec((8, half), lambda g: (0, 0)),
            pl.BlockSpec((2 * half, CMP_HIDDEN), lambda g: (0, 0)),
            pl.BlockSpec((CMP_HIDDEN, HEAD_DIM), lambda g: (0, 0)),
        ],
        out_specs=pl.BlockSpec((None, nc, HEAD_DIM), lambda g: (g, 0, 0)),
        out_shape=jax.ShapeDtypeStruct((G, nc, HEAD_DIM), F32),
        compiler_params=_cparams(("parallel",)),
        name=name,
    )(chunks, nxt, pe2, w1.astype(BF16), w2.astype(BF16))


def _nsa_kernel(q_ref, kc_ref, vc_ref, ks_ref, vs_ref, misc_ref, ow_ref, ov_ref, e_ref, o_ref,
                s_ref, acc_ref, *, n_top):
    C = KEY_CHUNK
    i = pl.program_id(1)
    nch = i // (C // Q_BLOCK) + 1
    ncmp = kc_ref.shape[0]
    nsel = ov_ref.shape[0]
    lanes = GROUP * Q_BLOCK
    t = i * Q_BLOCK + lax.broadcasted_iota(jnp.int32, (1, Q_BLOCK), 1)
    t4 = i * Q_BLOCK + (lax.broadcasted_iota(jnp.int32, (1, lanes), 1) & (Q_BLOCK - 1))
    row = lax.broadcasted_iota(jnp.int32, (C, Q_BLOCK), 0)
    crow = lax.broadcasted_iota(jnp.int32, (ncmp, lanes), 0)
    jrow = lax.broadcasted_iota(jnp.int32, (nsel, Q_BLOCK), 0)
    cur = t // SLC_LEN
    qpad = _padded_queries(q_ref[...])
    kc = kc_ref[...]
    vcm = vc_ref[...]
    ov = ov_ref[...]

    o_cmps, selbs = [], []
    for kv in range(N_KV_HEADS):
        s = jnp.dot(kc, qpad[kv], preferred_element_type=F32)
        s = jnp.where(crow * CMP_STRIDE + (CMP_LEN - 1) <= t4, s, MASK_VAL)
        m = jnp.maximum(jnp.max(s, axis=0, keepdims=True), M_INIT)
        p = jnp.exp2(s - m)
        l = jnp.sum(p, axis=0, keepdims=True)
        p = p * jnp.where(l > 0.0, 1.0 / l, 0.0)
        o_cmp = jnp.dot(vcm, p.astype(BF16), preferred_element_type=F32)[kv * HEAD_DIM:(kv + 1) * HEAD_DIM, :]

        pg = p[:, 0:Q_BLOCK]
        for g in range(1, GROUP):
            pg = pg + p[:, g * Q_BLOCK:(g + 1) * Q_BLOCK]
        p_hi = pg.astype(BF16)
        p_lo = (pg - p_hi.astype(F32)).astype(BF16)
        imp = (jnp.dot(ov, p_hi, preferred_element_type=F32)
               + jnp.dot(ov, p_lo, preferred_element_type=F32))
        forced = (jrow == 0) | (jrow == cur) | (jrow == cur - 1)
        imp = jnp.where(forced, FORCE_SCORE, imp)
        imp = jnp.where(jrow * SLC_LEN <= t, imp, NEG_INF)
        tiles = [imp[r0:r0 + 8] for r0 in range(0, nsel, 8)]
        ranks = [jnp.zeros((8, Q_BLOCK), F32) for _ in tiles]
        sub = lax.broadcasted_iota(jnp.int32, (8, Q_BLOCK), 0)
        for ii in range(nsel):
            r = jnp.broadcast_to(imp[ii:ii + 1], (8, Q_BLOCK))
            for jt, tile in enumerate(tiles):
                ge = jnp.where(r >= tile, 1.0, 0.0)
                gt = jnp.where(r > tile, 1.0, 0.0)
                if jt > ii // 8:
                    ahead = ge
                elif jt < ii // 8:
                    ahead = gt
                else:
                    ahead = jnp.where(sub > ii % 8, ge, gt)
                ranks[jt] = ranks[jt] + ahead
        rank = jnp.concatenate(ranks, axis=0)
        selbs.append(jnp.where(rank < float(n_top), 1.0, 0.0).astype(BF16))
        o_cmps.append(o_cmp)

    qall = jnp.concatenate(qpad, axis=1)
    selb_all = jnp.concatenate(selbs, axis=1)

    def sel_fn(off, size):
        mk = jnp.dot(e_ref[pl.ds(off, size), :], selb_all, preferred_element_type=F32)
        causal = (off + lax.broadcasted_iota(jnp.int32, (size, Q_BLOCK), 0)) <= t
        parts = []
        for kv in range(N_KV_HEADS):
            keep = jnp.where(causal, mk[:, kv * Q_BLOCK:(kv + 1) * Q_BLOCK], 0.0) > 0.5
            parts.append(_lane_tile(jnp.where(keep, 0.0, MASK_VAL), GROUP))
        return jnp.concatenate(parts, axis=1)

    o_slcs = _masked_attention(nch, sel_fn, ks_ref, vs_ref, qall, s_ref, acc_ref)

    for kv in range(N_KV_HEADS):
        for g in range(GROUP):
            hq = kv * GROUP + g
            gr = GATE_ROW0 + hq * 3
            gate = [1.0 / (1.0 + jnp.exp(-misc_ref[gr + br:gr + br + 1, :])) for br in range(3)]
            out = (gate[0] * o_cmps[kv][:, g * Q_BLOCK:(g + 1) * Q_BLOCK]
                   + gate[1] * o_slcs[kv][:, g * Q_BLOCK:(g + 1) * Q_BLOCK]
                   + gate[2] * ow_ref[hq * HEAD_DIM:(hq + 1) * HEAD_DIM, :])
            o_ref[hq * HEAD_DIM:(hq + 1) * HEAD_DIM, :] = out.astype(BF16)


def _nsa(qT, kcmp_rows, vcmpT, k_rows, vT128, miscT, o_winT, ovT, emat, B, T):
    nq = T // Q_BLOCK
    N = B * T
    ncmp = T // CMP_STRIDE
    nsel = T // SLC_LEN
    n_top = min(SLC_TOPN, nsel)
    return pl.pallas_call(
        functools.partial(_nsa_kernel, n_top=n_top),
        grid=(B, nq),
        in_specs=[
            pl.BlockSpec((MIX_WIDTH, Q_BLOCK), lambda b, i: (2, b * nq + i)),
            pl.BlockSpec((None, ncmp, KV_WIDTH), lambda b, i: (b, 0, 0)),
            pl.BlockSpec((None, KV_WIDTH, ncmp), lambda b, i: (b, 0, 0)),
            pl.BlockSpec((T, KV_WIDTH), lambda b, i: (b, 3)),
            pl.BlockSpec((nq, KV_WIDTH, LANE), lambda b, i: (b, 3, 0)),
            pl.BlockSpec((MISC_ROWS, Q_BLOCK), lambda b, i: (0, b * nq + i)),
            pl.BlockSpec((MIX_WIDTH, Q_BLOCK), lambda b, i: (0, b * nq + i)),
            pl.BlockSpec((nsel, ncmp), lambda b, i: (0, 0)),
            pl.BlockSpec((T, nsel), lambda b, i: (0, 0)),
        ],
        out_specs=pl.BlockSpec((MIX_WIDTH, Q_BLOCK), lambda b, i: (0, b * nq + i)),
        out_shape=jax.ShapeDtypeStruct((MIX_WIDTH, N), BF16),
        scratch_shapes=[
            pltpu.VMEM((T, N_KV_HEADS * GROUP * Q_BLOCK), F32),
            pltpu.VMEM((KV_WIDTH + ONES_ROWS, N_KV_HEADS * GROUP * Q_BLOCK), F32),
        ],
        compiler_params=_cparams(("parallel", "arbitrary")),
        name="nsa_attention",
    )(qT, kcmp_rows, vcmpT, k_rows, vT128, miscT, o_winT, ovT, emat)


def _merge_kernel(x_ref, g_ref, wg_ref, oa_ref, ob_ref, oc_ref, wb_ref, wo_ref, out_ref):
    tm = x_ref.shape[1]
    x = x_ref[...]
    ss = jnp.sum(x * x, axis=0, keepdims=True)
    h = (x * lax.rsqrt(ss * (1.0 / D_MODEL) + EPS) * _lane_tile(g_ref[...], tm // LANE)).astype(BF16)
    acc = None
    for n, o_ref in enumerate((oa_ref, ob_ref, oc_ref)):
        gm = jnp.dot(wg_ref[n * D_MODEL:(n + 1) * D_MODEL, :], h, preferred_element_type=F32)
        y = jnp.dot(wb_ref[n], o_ref[...], preferred_element_type=F32)
        term = y * (1.0 / (1.0 + jnp.exp(-gm)))
        acc = term if acc is None else acc + term
    out_ref[...] = x + jnp.dot(wo_ref[...], acc.astype(BF16), preferred_element_type=F32)


def _merge(xT, g_b, wgT, oa, ob, oc, wbT, woT):
    N = xT.shape[1]
    tm = 512
    col = lambda i: (0, i)
    return pl.pallas_call(
        _merge_kernel,
        grid=(N // tm,),
        in_specs=[
            pl.BlockSpec((D_MODEL, tm), col),
            pl.BlockSpec((D_MODEL, LANE), lambda i: (0, 0)),
            pl.BlockSpec((N_MIXERS * D_MODEL, D_MODEL), lambda i: (0, 0)),
            pl.BlockSpec((MIX_WIDTH, tm), col),
            pl.BlockSpec((MIX_WIDTH, tm), col),
            pl.BlockSpec((MIX_WIDTH, tm), col),
            pl.BlockSpec((N_MIXERS, D_MODEL, MIX_WIDTH), lambda i: (0, 0, 0)),
            pl.BlockSpec((D_MODEL, D_MODEL), lambda i: (0, 0)),
        ],
        out_specs=pl.BlockSpec((D_MODEL, tm), col),
        out_shape=jax.ShapeDtypeStruct((D_MODEL, N), F32),
        compiler_params=_cparams(("parallel",)),
        name="merge_out_projection",
    )(xT, g_b, wgT, oa, ob, oc, wbT, woT)


def _mlp_kernel(x_ref, g_ref, wu_ref, wd_ref, out_ref, h_ref, acc_ref):
    f = pl.program_id(1)
    tm = x_ref.shape[1]

    @pl.when(f == 0)
    def _():
        x = x_ref[...]
        ss = jnp.sum(x * x, axis=0, keepdims=True)
        h_ref[...] = (x * lax.rsqrt(ss * (1.0 / D_MODEL) + EPS)
                      * _lane_tile(g_ref[...], tm // LANE)).astype(BF16)
        acc_ref[...] = jnp.zeros(acc_ref.shape, F32)

    u = jnp.maximum(jnp.dot(wu_ref[...], h_ref[...], preferred_element_type=F32), 0.0)
    acc_ref[...] += jnp.dot(wd_ref[...], (u * u).astype(BF16), preferred_element_type=F32)

    @pl.when(f == pl.num_programs(1) - 1)
    def _():
        out_ref[...] = x_ref[...] + acc_ref[...]


def _mlp(xT, g_b, wuT, wdT):
    N = xT.shape[1]
    tm, tf = 1024, 1024
    return pl.pallas_call(
        _mlp_kernel,
        grid=(N // tm, D_FF // tf),
        in_specs=[
            pl.BlockSpec((D_MODEL, tm), lambda i, f: (0, i)),
            pl.BlockSpec((D_MODEL, LANE), lambda i, f: (0, 0)),
            pl.BlockSpec((tf, D_MODEL), lambda i, f: (f, 0)),
            pl.BlockSpec((D_MODEL, tf), lambda i, f: (0, f)),
        ],
        out_specs=pl.BlockSpec((D_MODEL, tm), lambda i, f: (0, i)),
        out_shape=jax.ShapeDtypeStruct((D_MODEL, N), F32),
        scratch_shapes=[pltpu.VMEM((D_MODEL, tm), BF16), pltpu.VMEM((D_MODEL, tm), F32)],
        compiler_params=_cparams(("parallel", "arbitrary")),
        name="relu2_mlp",
    )(xT, g_b, wuT, wdT)


def _to_chunks(rows, B, T):
    half_tokens = CMP_LEN // 2
    t = rows.reshape(B, T, N_KV_HEADS, HEAD_DIM).transpose(0, 2, 1, 3)
    return t.reshape(B * N_KV_HEADS, T // half_tokens, half_tokens * HEAD_DIM)


def kernel(x, norm_mix, w_in, q_norm, k_norm, sinks, cmp_pe_k, cmp_pe_v, w_ck1, w_ck2, w_cv1, w_cv2,
           w_branch, w_out, norm_mlp, w_up, w_down):
    B, T, D = x.shape
    N = B * T
    depth = w_in.shape[0]
    nq = T // Q_BLOCK
    ncmp = T // CMP_STRIDE
    nsel = T // SLC_LEN

    inv_freq = ROPE_THETA ** (-jnp.arange(0, HEAD_DIM, 2, dtype=F32) / HEAD_DIM)
    ang = jnp.arange(T, dtype=F32)[:, None] * inv_freq[None, :]
    cosT, sinT = jnp.cos(ang).T, jnp.sin(ang).T

    ci = np.arange(KEY_CHUNK)
    ltri = jnp.asarray(ci[None, :] < ci[:, None], BF16)
    emat = jnp.asarray((np.arange(T)[:, None] // SLC_LEN) == np.arange(nsel)[None, :], BF16)
    cstart = np.arange(ncmp) * CMP_STRIDE
    sstart = np.arange(nsel) * SLC_LEN
    n_cmp_valid = (T - CMP_LEN) // CMP_STRIDE + 1
    ov = ((cstart[None, :] < sstart[:, None] + SLC_LEN) & (cstart[None, :] + CMP_LEN > sstart[:, None])
          & (np.arange(ncmp)[None, :] < n_cmp_valid))
    ovT = jnp.asarray(ov, BF16)

    xT = x.reshape(N, D).T

    sl = lambda s: w_in[:, :, s[0]:s[1]]
    wT_all = jnp.concatenate(
        [sl(S_QA), sl(S_QB), sl(S_QC),
         sl(S_KA), sl(S_KB), sl(S_KC), sl(S_KSL), sl(S_KWN),
         sl(S_VA), sl(S_VB), sl(S_VC), sl(S_VSL), sl(S_VWN),
         sl(S_IQ), sl(S_IK), sl(S_IW), jnp.zeros((depth, D, GATE_ROW0 - IDX_HEADS), w_in.dtype), sl(S_GC)],
        axis=2).transpose(0, 2, 1).astype(BF16)
    wgT_all = sl(S_GM).transpose(0, 2, 1).astype(BF16)
    wbT_all = w_branch.transpose(0, 1, 3, 2).astype(BF16)
    woT_all = w_out.transpose(0, 2, 1).astype(BF16)
    wuT_all = w_up.transpose(0, 2, 1).astype(BF16)
    wdT_all = w_down.transpose(0, 2, 1).astype(BF16)
    lane_b = lambda v: jnp.broadcast_to(v.astype(F32)[..., None], v.shape + (LANE,))
    gmix_all, gmlp_all = lane_b(norm_mix), lane_b(norm_mlp)
    gq_all = lane_b(jnp.tile(q_norm[:, :, None, :], (1, 1, N_Q_HEADS, 1)).reshape(depth, -1))
    gk_all = lane_b(jnp.tile(k_norm[:, jnp.array([0, 1, 2, 2, 2])][:, :, None, :],
                             (1, 1, N_KV_HEADS, 1)).reshape(depth, -1))
    sink_all = jnp.broadcast_to(jnp.repeat(sinks.astype(F32) * LOG2E, Q_BLOCK, axis=1)[:, None, :],
                                (depth, 8, N_Q_HEADS * Q_BLOCK))

    for l in range(depth):
        wgT, sink_lanes = wgT_all[l], sink_all[l]
        qT, kT, vT128, iqT, ikT, miscT = _project(
            xT, gmix_all[l], wT_all[l], gq_all[l], gk_all[l], cosT, sinT, T)
        k_rows = kT.T
        ik_rows = ikT.T

        o_a = _dsa(iqT, miscT, ik_rows, qT, k_rows, vT128, ltri, B, T)
        o_b = _band(qT, 1, k_rows, 1, vT128, sink_lanes, SWA_WINDOW, True, BF16, B, T, "swa_attention")
        o_w = _band(qT, 2, k_rows, 4, vT128, sink_lanes, NSA_WINDOW, False, F32, B, T, "nsa_window_attention")

        kc_chunks = _to_chunks(k_rows[:, 2 * KV_WIDTH:3 * KV_WIDTH].reshape(B, T, KV_WIDTH), B, T)
        vc_rows = vT128[:, 2 * KV_WIDTH:3 * KV_WIDTH, :].reshape(B, nq, KV_WIDTH, LANE)
        vc_rows = vc_rows.transpose(0, 1, 3, 2).reshape(B, T, KV_WIDTH)
        vc_chunks = _to_chunks(vc_rows, B, T)
        k_cmp = _compress(kc_chunks, cmp_pe_k[l], w_ck1[l], w_ck2[l], "compress_k")
        v_cmp = _compress(vc_chunks, cmp_pe_v[l], w_cv1[l], w_cv2[l], "compress_v")
        kcmp_rows = k_cmp.reshape(B, N_KV_HEADS, ncmp, HEAD_DIM).transpose(0, 2, 1, 3)
        kcmp_rows = kcmp_rows.reshape(B, ncmp, KV_WIDTH).astype(BF16)
        vcmpT = v_cmp.reshape(B, N_KV_HEADS, ncmp, HEAD_DIM).transpose(0, 1, 3, 2)
        vcmpT = vcmpT.reshape(B, KV_WIDTH, ncmp).astype(BF16)

        o_c = _nsa(qT, kcmp_rows, vcmpT, k_rows, vT128, miscT, o_w, ovT, emat, B, T)

        xT = _merge(xT, gmix_all[l], wgT, o_a, o_b, o_c, wbT_all[l], woT_all[l])
        xT = _mlp(xT, gmlp_all[l], wuT_all[l], wdT_all[l])

    return xT.T.reshape(B, T, D)
```

```python
import functools

import numpy as np
import jax
import jax.numpy as jnp
from jax import lax
from jax.experimental import pallas as pl
from jax.experimental.pallas import tpu as pltpu

F32 = jnp.float32
BF16 = jnp.bfloat16

D_MODEL = 1024
HEAD_DIM = 64
HALF = HEAD_DIM // 2
N_Q_HEADS = 8
N_KV_HEADS = 2
GROUP = N_Q_HEADS // N_KV_HEADS
MIX_WIDTH = N_Q_HEADS * HEAD_DIM
KV_WIDTH = N_KV_HEADS * HEAD_DIM
N_MIXERS = 3
D_FF = 4 * D_MODEL
ROPE_THETA = 10000.0
EPS = 1e-6
NEG_INF = -1e30
MASK_VAL = -2e30
M_INIT = -1e30
BIG = 3e38
Q_BLOCK = 128
KEY_CHUNK = 512
BISECT_COUNT_STEPS = 14
BISECT_FIXED_STEPS = 2
ONES_ROWS = 16
IDX_HEADS = 4
IDX_DIM = 64
DSA_TOPK_MAX = 256
SWA_WINDOW = 128
CMP_LEN = 32
CMP_STRIDE = 16
CMP_HIDDEN = 256
SLC_LEN = 64
SLC_TOPN = 16
NSA_WINDOW = 512
FORCE_SCORE = 1e9
LOG2E = 1.4426950408889634
Q_SCALE = (HEAD_DIM ** -0.5) * LOG2E
LANE = 128
LANE_SHIFT = 7
ATT_BIG_STEP = 1024
VMEM_LIMIT = 56 * 1024 * 1024

IN_SIZES = (
    MIX_WIDTH, KV_WIDTH, KV_WIDTH, IDX_HEADS * IDX_DIM, IDX_DIM, IDX_HEADS,
    MIX_WIDTH, KV_WIDTH, KV_WIDTH,
    MIX_WIDTH, KV_WIDTH, KV_WIDTH, KV_WIDTH, KV_WIDTH, KV_WIDTH, KV_WIDTH, 3 * N_Q_HEADS,
    N_MIXERS * D_MODEL,
)
_OFF = np.concatenate([[0], np.cumsum(IN_SIZES)]).astype(int)
(S_QA, S_KA, S_VA, S_IQ, S_IK, S_IW, S_QB, S_KB, S_VB,
 S_QC, S_KC, S_VC, S_KSL, S_VSL, S_KWN, S_VWN, S_GC, S_GM) = [
    (int(_OFF[i]), int(_OFF[i + 1])) for i in range(len(IN_SIZES))]

R_Q = 0
R_K = R_Q + N_MIXERS * MIX_WIDTH
R_V = R_K + 5 * KV_WIDTH
R_IQ = R_V + 5 * KV_WIDTH
R_IK = R_IQ + IDX_HEADS * IDX_DIM
R_MISC = R_IK + IDX_DIM
MISC_ROWS = 32
GATE_ROW0 = 8
R_END = R_MISC + MISC_ROWS
PROJ_TM = 1024


def _cparams(sem):
    return pltpu.CompilerParams(dimension_semantics=sem, vmem_limit_bytes=VMEM_LIMIT)


def _lane_tile(a, n):
    return a if n == 1 else jnp.concatenate([a] * n, axis=1)


def _proj_kernel(x_ref, g_ref, w_ref, gq_ref, gk_ref, cos_ref, sin_ref,
                 q_ref, k_ref, v_ref, iq_ref, ik_ref, misc_ref):
    tm = x_ref.shape[1]
    rep = tm // LANE
    x = x_ref[...]
    ss = jnp.sum(x * x, axis=0, keepdims=True)
    h = (x * lax.rsqrt(ss * (1.0 / D_MODEL) + EPS) * _lane_tile(g_ref[...], rep)).astype(BF16)
    cos = cos_ref[...]
    sin = sin_ref[...]

    def rope(z):
        z1, z2 = z[:HALF], z[HALF:]
        return jnp.concatenate([z1 * cos - z2 * sin, z2 * cos + z1 * sin], axis=0)

    def head_norm(z, gain):
        ssq = jnp.sum(z * z, axis=0, keepdims=True)
        return z * lax.rsqrt(ssq * (1.0 / HEAD_DIM) + EPS) * _lane_tile(gain, rep)

    for m in range(N_MIXERS):
        z = jnp.dot(w_ref[R_Q + m * MIX_WIDTH:R_Q + (m + 1) * MIX_WIDTH, :], h,
                    preferred_element_type=F32)
        for hh in range(N_Q_HEADS):
            r0 = m * MIX_WIDTH + hh * HEAD_DIM
            zh = head_norm(z[hh * HEAD_DIM:(hh + 1) * HEAD_DIM], gq_ref[r0:r0 + HEAD_DIM, :])
            q_ref[r0:r0 + HEAD_DIM, :] = (rope(zh) * Q_SCALE).astype(BF16)

    z = jnp.dot(w_ref[R_K:R_V, :], h, preferred_element_type=F32)
    for hh in range(5 * N_KV_HEADS):
        r0 = hh * HEAD_DIM
        zh = head_norm(z[r0:r0 + HEAD_DIM], gk_ref[r0:r0 + HEAD_DIM, :])
        k_ref[r0:r0 + HEAD_DIM, :] = rope(zh).astype(BF16)

    z = jnp.dot(w_ref[R_V:R_IQ, :], h, preferred_element_type=F32)
    for j in range(rep):
        v_ref[j] = z[:, j * LANE:(j + 1) * LANE].astype(BF16)

    z = jnp.dot(w_ref[R_IQ:R_END, :], h, preferred_element_type=F32)
    for hh in range(IDX_HEADS):
        r0 = hh * IDX_DIM
        iq_ref[r0:r0 + IDX_DIM, :] = rope(z[r0:r0 + IDX_DIM])
    ik_ref[...] = rope(z[R_IK - R_IQ:R_MISC - R_IQ])
    misc_ref[...] = z[R_MISC - R_IQ:R_END - R_IQ]


def _project(xT, g_b, wT, gq_b, gk_b, cosT, sinT, T):
    N = xT.shape[1]
    tm = PROJ_TM
    tpb = T // tm
    rep = tm // LANE
    col = lambda i: (0, i)
    const = lambda i: (0, 0)
    return pl.pallas_call(
        _proj_kernel,
        grid=(N // tm,),
        in_specs=[
            pl.BlockSpec((D_MODEL, tm), col),
            pl.BlockSpec((D_MODEL, LANE), const),
            pl.BlockSpec((R_END, D_MODEL), const),
            pl.BlockSpec((N_MIXERS * MIX_WIDTH, LANE), const),
            pl.BlockSpec((5 * KV_WIDTH, LANE), const),
            pl.BlockSpec((HALF, tm), lambda i: (0, i % tpb)),
            pl.BlockSpec((HALF, tm), lambda i: (0, i % tpb)),
        ],
        out_specs=[
            pl.BlockSpec((N_MIXERS * MIX_WIDTH, tm), col),
            pl.BlockSpec((5 * KV_WIDTH, tm), col),
            pl.BlockSpec((rep, 5 * KV_WIDTH, LANE), lambda i: (i, 0, 0)),
            pl.BlockSpec((IDX_HEADS * IDX_DIM, tm), col),
            pl.BlockSpec((IDX_DIM, tm), col),
            pl.BlockSpec((MISC_ROWS, tm), col),
        ],
        out_shape=[
            jax.ShapeDtypeStruct((N_MIXERS * MIX_WIDTH, N), BF16),
            jax.ShapeDtypeStruct((5 * KV_WIDTH, N), BF16),
            jax.ShapeDtypeStruct((N // LANE, 5 * KV_WIDTH, LANE), BF16),
            jax.ShapeDtypeStruct((IDX_HEADS * IDX_DIM, N), F32),
            jax.ShapeDtypeStruct((IDX_DIM, N), F32),
            jax.ShapeDtypeStruct((MISC_ROWS, N), F32),
        ],
        compiler_params=_cparams(("parallel",)),
        name="in_projection",
    )(xT, g_b, wT, gq_b, gk_b, cosT, sinT)


def _padded_queries(q):
    zeros = jnp.zeros((HEAD_DIM, Q_BLOCK), q.dtype)
    out = []
    for kv in range(N_KV_HEADS):
        cols = []
        for g in range(GROUP):
            hq = kv * GROUP + g
            qh = q[hq * HEAD_DIM:(hq + 1) * HEAD_DIM, :]
            cols.append(jnp.concatenate([qh, zeros] if kv == 0 else [zeros, qh], axis=0))
        out.append(jnp.concatenate(cols, axis=1))
    return out


def _v_chunk(v_ref, blk0, nblk):
    return jnp.concatenate([v_ref[blk0 + j] for j in range(nblk)], axis=1)


def _masked_attention(nch, sel_fn, k_ref, v_ref, qall, s_ref, acc_ref):
    C = KEY_CHUNK
    CB = ATT_BIG_STEP
    lanes = N_KV_HEADS * GROUP * Q_BLOCK
    nbig = nch // (CB // C)
    nsmall = nch - nbig * (CB // C)
    small0 = nbig * CB

    def walk(fn, carry):
        carry = lax.fori_loop(0, nbig, lambda c, x: fn(pl.multiple_of(c * CB, CB), CB, x), carry)
        return lax.fori_loop(0, nsmall, lambda c, x: fn(pl.multiple_of(small0 + c * C, C), C, x), carry)

    def qk_step(off, size, mx):
        s = jnp.dot(k_ref[pl.ds(off, size), :], qall, preferred_element_type=F32) + sel_fn(off, size)
        s_ref[pl.ds(off, size), :] = s
        return jnp.maximum(mx, jnp.max(s, axis=0, keepdims=True))

    mx = walk(qk_step, jnp.full((1, lanes), M_INIT, F32))
    acc_ref[...] = jnp.zeros(acc_ref.shape, F32)

    def pv_step(off, size, carry):
        p = jnp.exp2(s_ref[pl.ds(off, size), :] - mx).astype(BF16)
        vc = jnp.concatenate([_v_chunk(v_ref, lax.shift_right_logical(off, LANE_SHIFT), size // LANE),
                              jnp.ones((ONES_ROWS, size), BF16)], axis=0)
        acc_ref[...] += jnp.dot(vc, p, preferred_element_type=F32)
        return carry

    walk(pv_step, 0)
    l = acc_ref[KV_WIDTH:KV_WIDTH + 1, :]
    inv = jnp.where(l > 0.0, 1.0 / l, 0.0)
    half = GROUP * Q_BLOCK
    return [acc_ref[kv * HEAD_DIM:(kv + 1) * HEAD_DIM, kv * half:(kv + 1) * half]
            * inv[:, kv * half:(kv + 1) * half] for kv in range(N_KV_HEADS)]


def _dsa_kernel(iq_ref, misc_ref, ik_ref, q_ref, k_ref, v_ref, ltri_ref, o_ref,
                sc_ref, s_ref, acc_ref, *, k_top):
    C = KEY_CHUNK
    i = pl.program_id(1)
    nch = i // (C // Q_BLOCK) + 1
    qpos = i * Q_BLOCK + lax.broadcasted_iota(jnp.int32, (1, Q_BLOCK), 1)
    row = lax.broadcasted_iota(jnp.int32, (C, Q_BLOCK), 0)

    iq = iq_ref[...].astype(BF16)
    iq_cat = jnp.concatenate([iq[h * IDX_DIM:(h + 1) * IDX_DIM] for h in range(IDX_HEADS)], axis=1)
    w = misc_ref[0:IDX_HEADS, :] * ((IDX_HEADS ** -0.5) * (IDX_DIM ** -0.5))

    def fold(x):
        return x.reshape(C // 8, 8, Q_BLOCK)

    part = lambda v: jnp.full((8, Q_BLOCK), v, F32)

    def score_chunk(c, carry):
        lo, hi = carry
        off = pl.multiple_of(c * C, C)
        ikc = ik_ref[pl.ds(off, C), :].astype(BF16)
        d = jnp.dot(ikc, iq_cat, preferred_element_type=F32)
        s = jnp.maximum(d[:, 0:Q_BLOCK], 0.0) * w[0:1, :]
        for h in range(1, IDX_HEADS):
            s = s + jnp.maximum(d[:, h * Q_BLOCK:(h + 1) * Q_BLOCK], 0.0) * w[h:h + 1, :]
        causal = (off + row) <= qpos
        sc_ref[pl.ds(off, C), :] = jnp.where(causal, s, NEG_INF)
        lo = jnp.minimum(lo, jnp.min(fold(jnp.where(causal, s, BIG)), axis=0))
        hi = jnp.maximum(hi, jnp.max(fold(jnp.where(causal, s, -BIG)), axis=0))
        return lo, hi

    lo0, hi0 = lax.fori_loop(0, nch, score_chunk, (part(BIG), part(-BIG)))
    lo0 = jnp.min(lo0, axis=0, keepdims=True)
    hi0 = jnp.max(hi0, axis=0, keepdims=True)

    kf = float(k_top)
    select = i * Q_BLOCK >= k_top

    def count_ge(mid):
        mid8 = jnp.broadcast_to(mid, (8, Q_BLOCK))[None]

        def body(c, cnts):
            cnts = list(cnts)
            for j in range(C // 64):
                off = pl.multiple_of(c * C + j * 64, 64)
                s = sc_ref[pl.ds(off, 64), :].reshape(8, 8, Q_BLOCK)
                cnts[j % 4] = cnts[j % 4] + jnp.sum(jnp.where(s >= mid8, 1.0, 0.0), axis=0)
            return tuple(cnts)

        cnts = lax.fori_loop(0, nch, body, (part(0.0),) * 4)
        return jnp.sum((cnts[0] + cnts[1]) + (cnts[2] + cnts[3]), axis=0, keepdims=True)

    def count_step(_, st):
        lo, hi, clo, chi, moved = st
        mid = lo + (hi - lo) * 0.5
        cnt = count_ge(mid)
        up = cnt >= kf
        return (jnp.where(up, mid, lo), jnp.where(up, hi, mid), jnp.where(up, cnt, clo),
                jnp.where(up, chi, cnt), jnp.where(up, moved, 1.0))

    def stats(mid):
        mid8 = jnp.broadcast_to(mid, (8, Q_BLOCK))[None]

        def body(c, carry):
            cnt, a, b = carry
            for j in range(C // 64):
                off = pl.multiple_of(c * C + j * 64, 64)
                s = sc_ref[pl.ds(off, 64), :].reshape(8, 8, Q_BLOCK)
                gt = s > mid8
                cnt = cnt + jnp.sum(jnp.where(gt, 1.0, 0.0), axis=0)
                a = jnp.minimum(a, jnp.min(jnp.where(gt, s, BIG), axis=0))
                b = jnp.maximum(b, jnp.max(jnp.where(gt, -BIG, s), axis=0))
            return cnt, a, b

        cnt, a, b = lax.fori_loop(0, nch, body, (part(0.0), part(BIG), part(-BIG)))
        return (jnp.sum(cnt, axis=0, keepdims=True), jnp.min(a, axis=0, keepdims=True),
                jnp.max(b, axis=0, keepdims=True))

    def step(st):
        lo, hi, clo, chi = st
        mid = jnp.maximum(lo + (hi - lo) * 0.5, lo)
        mid = jnp.where(mid >= hi, lo, mid)
        cnt, a, b = stats(mid)
        up = cnt >= kf
        return (jnp.where(up, a, lo), jnp.where(up, hi, b),
                jnp.where(up, cnt, clo), jnp.where(up, chi, cnt))

    def cond(st):
        return jnp.logical_and(select, jnp.max(st[1] - st[0]) > 0.0)

    n_causal = (qpos + 1).astype(F32)
    zero = jnp.zeros((1, Q_BLOCK), F32)
    lo_v, hi_v, clo, chi, moved = lax.fori_loop(
        0, jnp.where(select, BISECT_COUNT_STEPS, 0), count_step, (lo0, hi0, n_causal, zero, zero))

    def snap_chunk(c, carry):
        lo, hi = carry
        s = fold(sc_ref[pl.ds(pl.multiple_of(c * C, C), C), :])
        lo = jnp.minimum(lo, jnp.min(jnp.where(s >= lo_v[None], s, BIG), axis=0))
        hi = jnp.maximum(hi, jnp.max(jnp.where(s < hi_v[None], s, -BIG), axis=0))
        return lo, hi

    lo1, hi1 = lax.fori_loop(0, jnp.where(select, nch, 0), snap_chunk, (part(BIG), part(-BIG)))
    lo1 = jnp.min(lo1, axis=0, keepdims=True)
    hi1 = jnp.where(moved > 0.0, jnp.max(hi1, axis=0, keepdims=True), hi0)
    st = lax.fori_loop(0, jnp.where(select, BISECT_FIXED_STEPS, 0), lambda _, s: step(s),
                       (lo1, hi1, clo, chi))
    lo, _, clo, chi = lax.while_loop(cond, lambda s: step(step(s)), st)
    thr = jnp.where(select, lo, 0.5 * NEG_INF)
    need = jnp.where(select, kf - chi, 0.0)
    ties = jnp.logical_and(select, jnp.max((clo - chi) - need) > 0.0)

    @pl.when(ties)
    def _():
        def mask_chunk(c, tie_carry):
            off = pl.multiple_of(c * C, C)
            s_idx = sc_ref[pl.ds(off, C), :]
            eqf = jnp.where(s_idx == thr, 1.0, 0.0)
            before = jnp.dot(ltri_ref[...], eqf.astype(BF16), preferred_element_type=F32) + tie_carry
            sc_ref[pl.ds(off, C), :] = jnp.where(s_idx > thr, 1.0, jnp.where(before < need, eqf, 0.0))
            return tie_carry + jnp.sum(eqf, axis=0, keepdims=True)

        lax.fori_loop(0, nch, mask_chunk, jnp.zeros((1, Q_BLOCK), F32))

    thr_sel = jnp.where(ties, 0.5, thr)
    qall = jnp.concatenate(_padded_queries(q_ref[...]), axis=1)

    def sel_fn(off, size):
        bias = jnp.where(sc_ref[pl.ds(off, size), :] >= thr_sel, 0.0, MASK_VAL)
        return _lane_tile(bias, N_KV_HEADS * GROUP)

    outs = _masked_attention(nch, sel_fn, k_ref, v_ref, qall, s_ref, acc_ref)
    for kv in range(N_KV_HEADS):
        o = outs[kv]
        for g in range(GROUP):
            hq = kv * GROUP + g
            o_ref[hq * HEAD_DIM:(hq + 1) * HEAD_DIM, :] = o[:, g * Q_BLOCK:(g + 1) * Q_BLOCK].astype(BF16)


def _dsa(iqT, miscT, ik_rows, qT, k_rows, vT128, ltri, B, T):
    nq = T // Q_BLOCK
    k_top = min(DSA_TOPK_MAX, T // 4)
    N = B * T
    return pl.pallas_call(
        functools.partial(_dsa_kernel, k_top=k_top),
        grid=(B, nq),
        in_specs=[
            pl.BlockSpec((IDX_HEADS * IDX_DIM, Q_BLOCK), lambda b, i: (0, b * nq + i)),
            pl.BlockSpec((MISC_ROWS, Q_BLOCK), lambda b, i: (0, b * nq + i)),
            pl.BlockSpec((T, IDX_DIM), lambda b, i: (b, 0)),
            pl.BlockSpec((MIX_WIDTH, Q_BLOCK), lambda b, i: (0, b * nq + i)),
            pl.BlockSpec((T, KV_WIDTH), lambda b, i: (b, 0)),
            pl.BlockSpec((nq, KV_WIDTH, LANE), lambda b, i: (b, 0, 0)),
            pl.BlockSpec((KEY_CHUNK, KEY_CHUNK), lambda b, i: (0, 0)),
        ],
        out_specs=pl.BlockSpec((MIX_WIDTH, Q_BLOCK), lambda b, i: (0, b * nq + i)),
        out_shape=jax.ShapeDtypeStruct((MIX_WIDTH, N), BF16),
        scratch_shapes=[
            pltpu.VMEM((T, Q_BLOCK), F32),
            pltpu.VMEM((T, N_KV_HEADS * GROUP * Q_BLOCK), F32),
            pltpu.VMEM((KV_WIDTH + ONES_ROWS, N_KV_HEADS * GROUP * Q_BLOCK), F32),
        ],
        compiler_params=_cparams(("parallel", "arbitrary")),
        name="dsa_attention",
    )(iqT, miscT, ik_rows, qT, k_rows, vT128, ltri)


def _band_kernel(q_ref, k_ref, v_ref, sink_ref, o_ref, *, n_prev, window, use_sink):
    i = pl.program_id(1)
    blk0 = jnp.maximum(i - n_prev, 0)
    nblk = n_prev + 1
    wb = nblk * Q_BLOCK
    kb = k_ref[pl.ds(pl.multiple_of(blk0 * Q_BLOCK, Q_BLOCK), wb), :]
    vb = jnp.concatenate([_v_chunk(v_ref, blk0, nblk), jnp.ones((ONES_ROWS, wb), BF16)], axis=0)
    qall = jnp.concatenate(_padded_queries(q_ref[...]), axis=1)
    rel = ((i - blk0) * Q_BLOCK + lax.broadcasted_iota(jnp.int32, (wb, Q_BLOCK), 1)
           - lax.broadcasted_iota(jnp.int32, (wb, Q_BLOCK), 0))
    bias = jnp.where((rel >= 0) & (rel < window), 0.0, MASK_VAL)
    s = jnp.dot(kb, qall, preferred_element_type=F32) + _lane_tile(bias, N_KV_HEADS * GROUP)
    m = jnp.max(s, axis=0, keepdims=True)
    if use_sink:
        sink = sink_ref[0:1, :]
        m = jnp.maximum(m, sink)
    o = jnp.dot(vb, jnp.exp2(s - m).astype(BF16), preferred_element_type=F32)
    l = o[KV_WIDTH:KV_WIDTH + 1, :]
    if use_sink:
        l = l + jnp.exp2(sink - m)
    inv = 1.0 / l
    half = GROUP * Q_BLOCK
    for kv in range(N_KV_HEADS):
        for g in range(GROUP):
            hq = kv * GROUP + g
            lanes = slice(kv * half + g * Q_BLOCK, kv * half + (g + 1) * Q_BLOCK)
            o_ref[hq * HEAD_DIM:(hq + 1) * HEAD_DIM, :] = (
                o[kv * HEAD_DIM:(kv + 1) * HEAD_DIM, lanes] * inv[:, lanes]).astype(o_ref.dtype)


def _band(qT, q_mixer, k_rows, kv_index, vT128, sink_lanes, window, use_sink, out_dtype, B, T, name):
    nq = T // Q_BLOCK
    N = B * T
    n_prev = (window + Q_BLOCK - 2) // Q_BLOCK
    return pl.pallas_call(
        functools.partial(_band_kernel, n_prev=n_prev, window=window, use_sink=use_sink),
        grid=(B, nq),
        in_specs=[
            pl.BlockSpec((MIX_WIDTH, Q_BLOCK), lambda b, i: (q_mixer, b * nq + i)),
            pl.BlockSpec((T, KV_WIDTH), lambda b, i: (b, kv_index)),
            pl.BlockSpec((nq, KV_WIDTH, LANE), lambda b, i: (b, kv_index, 0)),
            pl.BlockSpec((8, N_Q_HEADS * Q_BLOCK), lambda b, i: (0, 0)),
        ],
        out_specs=pl.BlockSpec((MIX_WIDTH, Q_BLOCK), lambda b, i: (0, b * nq + i)),
        out_shape=jax.ShapeDtypeStruct((MIX_WIDTH, N), out_dtype),
        compiler_params=_cparams(("parallel", "arbitrary")),
        name=name,
    )(qT, k_rows, vT128, sink_lanes)


def _cmp_kernel(xa_ref, xb_ref, pe_ref, w1_ref, w2_ref, o_ref):
    half = (CMP_LEN // 2) * HEAD_DIM
    xa = (xa_ref[...].astype(F32) + pe_ref[0:1, :]).astype(BF16)
    xb = (xb_ref[...].astype(F32) + pe_ref[1:2, :]).astype(BF16)
    hid = (jnp.dot(xa, w1_ref[0:half, :], preferred_element_type=F32)
           + jnp.dot(xb, w1_ref[half:2 * half, :], preferred_element_type=F32))
    hid = jnp.maximum(hid, 0.0).astype(BF16)
    o_ref[...] = jnp.dot(hid, w2_ref[...], preferred_element_type=F32)


def _compress(chunks, pe, w1, w2, name):
    G, nc, half = chunks.shape
    nxt = jnp.concatenate([chunks[:, 1:], jnp.zeros((G, 1, half), chunks.dtype)], axis=1)
    pe2 = jnp.zeros((8, half), F32).at[0:2].set(pe.reshape(2, half))
    return pl.pallas_call(
        _cmp_kernel,
        grid=(G,),
        in_specs=[
            pl.BlockSpec((None, nc, half), lambda g: (g, 0, 0)),
            pl.BlockSpec((None, nc, half), lambda g: (g, 0, 0)),
            pl.BlockSpec((8, half), lambda g: (0, 0)),
            pl.BlockSpec((2 * half, CMP_HIDDEN), lambda g: (0, 0)),
            pl.BlockSpec((CMP_HIDDEN, HEAD_DIM), lambda g: (0, 0)),
        ],
        out_specs=pl.BlockSpec((None, nc, HEAD_DIM), lambda g: (g, 0, 0)),
        out_shape=jax.ShapeDtypeStruct((G, nc, HEAD_DIM), F32),
        compiler_params=_cparams(("parallel",)),
        name=name,
    )(chunks, nxt, pe2, w1.astype(BF16), w2.astype(BF16))


def _nsa_kernel(q_ref, kc_ref, vc_ref, ks_ref, vs_ref, misc_ref, ow_ref, ov_ref, e_ref, o_ref,
                s_ref, acc_ref, *, n_top):
    C = KEY_CHUNK
    i = pl.program_id(1)
    nch = i // (C // Q_BLOCK) + 1
    ncmp = kc_ref.shape[0]
    nsel = ov_ref.shape[0]
    lanes = GROUP * Q_BLOCK
    t = i * Q_BLOCK + lax.broadcasted_iota(jnp.int32, (1, Q_BLOCK), 1)
    t4 = i * Q_BLOCK + (lax.broadcasted_iota(jnp.int32, (1, lanes), 1) & (Q_BLOCK - 1))
    row = lax.broadcasted_iota(jnp.int32, (C, Q_BLOCK), 0)
    crow = lax.broadcasted_iota(jnp.int32, (ncmp, lanes), 0)
    jrow = lax.broadcasted_iota(jnp.int32, (nsel, Q_BLOCK), 0)
    cur = t // SLC_LEN
    qpad = _padded_queries(q_ref[...])
    kc = kc_ref[...]
    vcm = vc_ref[...]
    ov = ov_ref[...]

    o_cmps, selbs = [], []
    for kv in range(N_KV_HEADS):
        s = jnp.dot(kc, qpad[kv], preferred_element_type=F32)
        s = jnp.where(crow * CMP_STRIDE + (CMP_LEN - 1) <= t4, s, MASK_VAL)
        m = jnp.maximum(jnp.max(s, axis=0, keepdims=True), M_INIT)
        p = jnp.exp2(s - m)
        l = jnp.sum(p, axis=0, keepdims=True)
        p = p * jnp.where(l > 0.0, 1.0 / l, 0.0)
        o_cmp = jnp.dot(vcm, p.astype(BF16), preferred_element_type=F32)[kv * HEAD_DIM:(kv + 1) * HEAD_DIM, :]

        pg = p[:, 0:Q_BLOCK]
        for g in range(1, GROUP):
            pg = pg + p[:, g * Q_BLOCK:(g + 1) * Q_BLOCK]
        p_hi = pg.astype(BF16)
        p_lo = (pg - p_hi.astype(F32)).astype(BF16)
        imp = (jnp.dot(ov, p_hi, preferred_element_type=F32)
               + jnp.dot(ov, p_lo, preferred_element_type=F32))
        forced = (jrow == 0) | (jrow == cur) | (jrow == cur - 1)
        imp = jnp.where(forced, FORCE_SCORE, imp)
        imp = jnp.where(jrow * SLC_LEN <= t, imp, NEG_INF)
        tiles = [imp[r0:r0 + 8] for r0 in range(0, nsel, 8)]
        ranks = [jnp.zeros((8, Q_BLOCK), F32) for _ in tiles]
        sub = lax.broadcasted_iota(jnp.int32, (8, Q_BLOCK), 0)
        for ii in range(nsel):
            r = jnp.broadcast_to(imp[ii:ii + 1], (8, Q_BLOCK))
            for jt, tile in enumerate(tiles):
                ge = jnp.where(r >= tile, 1.0, 0.0)
                gt = jnp.where(r > tile, 1.0, 0.0)
                if jt > ii // 8:
                    ahead = ge
                elif jt < ii // 8:
                    ahead = gt
                else:
                    ahead = jnp.where(sub > ii % 8, ge, gt)
                ranks[jt] = ranks[jt] + ahead
        rank = jnp.concatenate(ranks, axis=0)
        selbs.append(jnp.where(rank < float(n_top), 1.0, 0.0).astype(BF16))
        o_cmps.append(o_cmp)

    qall = jnp.concatenate(qpad, axis=1)
    selb_all = jnp.concatenate(selbs, axis=1)

    def sel_fn(off, size):
        mk = jnp.dot(e_ref[pl.ds(off, size), :], selb_all, preferred_element_type=F32)
        causal = (off + lax.broadcasted_iota(jnp.int32, (size, Q_BLOCK), 0)) <= t
        parts = []
        for kv in range(N_KV_HEADS):
            keep = jnp.where(causal, mk[:, kv * Q_BLOCK:(kv + 1) * Q_BLOCK], 0.0) > 0.5
            parts.append(_lane_tile(jnp.where(keep, 0.0, MASK_VAL), GROUP))
        return jnp.concatenate(parts, axis=1)

    o_slcs = _masked_attention(nch, sel_fn, ks_ref, vs_ref, qall, s_ref, acc_ref)

    for kv in range(N_KV_HEADS):
        for g in range(GROUP):
            hq = kv * GROUP + g
            gr = GATE_ROW0 + hq * 3
            gate = [1.0 / (1.0 + jnp.exp(-misc_ref[gr + br:gr + br + 1, :])) for br in range(3)]
            out = (gate[0] * o_cmps[kv][:, g * Q_BLOCK:(g + 1) * Q_BLOCK]
                   + gate[1] * o_slcs[kv][:, g * Q_BLOCK:(g + 1) * Q_BLOCK]
                   + gate[2] * ow_ref[hq * HEAD_DIM:(hq + 1) * HEAD_DIM, :])
            o_ref[hq * HEAD_DIM:(hq + 1) * HEAD_DIM, :] = out.astype(BF16)


def _nsa(qT, kcmp_rows, vcmpT, k_rows, vT128, miscT, o_winT, ovT, emat, B, T):
    nq = T // Q_BLOCK
    N = B * T
    ncmp = T // CMP_STRIDE
    nsel = T // SLC_LEN
    n_top = min(SLC_TOPN, nsel)
    return pl.pallas_call(
        functools.partial(_nsa_kernel, n_top=n_top),
        grid=(B, nq),
        in_specs=[
            pl.BlockSpec((MIX_WIDTH, Q_BLOCK), lambda b, i: (2, b * nq + i)),
            pl.BlockSpec((None, ncmp, KV_WIDTH), lambda b, i: (b, 0, 0)),
            pl.BlockSpec((None, KV_WIDTH, ncmp), lambda b, i: (b, 0, 0)),
            pl.BlockSpec((T, KV_WIDTH), lambda b, i: (b, 3)),
            pl.BlockSpec((nq, KV_WIDTH, LANE), lambda b, i: (b, 3, 0)),
            pl.BlockSpec((MISC_ROWS, Q_BLOCK), lambda b, i: (0, b * nq + i)),
            pl.BlockSpec((MIX_WIDTH, Q_BLOCK), lambda b, i: (0, b * nq + i)),
            pl.BlockSpec((nsel, ncmp), lambda b, i: (0, 0)),
            pl.BlockSpec((T, nsel), lambda b, i: (0, 0)),
        ],
        out_specs=pl.BlockSpec((MIX_WIDTH, Q_BLOCK), lambda b, i: (0, b * nq + i)),
        out_shape=jax.ShapeDtypeStruct((MIX_WIDTH, N), BF16),
        scratch_shapes=[
            pltpu.VMEM((T, N_KV_HEADS * GROUP * Q_BLOCK), F32),
            pltpu.VMEM((KV_WIDTH + ONES_ROWS, N_KV_HEADS * GROUP * Q_BLOCK), F32),
        ],
        compiler_params=_cparams(("parallel", "arbitrary")),
        name="nsa_attention",
    )(qT, kcmp_rows, vcmpT, k_rows, vT128, miscT, o_winT, ovT, emat)


def _merge_kernel(x_ref, g_ref, wg_ref, oa_ref, ob_ref, oc_ref, wb_ref, wo_ref, out_ref):
    tm = x_ref.shape[1]
    x = x_ref[...]
    ss = jnp.sum(x * x, axis=0, keepdims=True)
    h = (x * lax.rsqrt(ss * (1.0 / D_MODEL) + EPS) * _lane_tile(g_ref[...], tm // LANE)).astype(BF16)
    acc = None
    for n, o_ref in enumerate((oa_ref, ob_ref, oc_ref)):
        gm = jnp.dot(wg_ref[n * D_MODEL:(n + 1) * D_MODEL, :], h, preferred_element_type=F32)
        y = jnp.dot(wb_ref[n], o_ref[...], preferred_element_type=F32)
        term = y * (1.0 / (1.0 + jnp.exp(-gm)))
        acc = term if acc is None else acc + term
    out_ref[...] = x + jnp.dot(wo_ref[...], acc.astype(BF16), preferred_element_type=F32)


def _merge(xT, g_b, wgT, oa, ob, oc, wbT, woT):
    N = xT.shape[1]
    tm = 512
    col = lambda i: (0, i)
    return pl.pallas_call(
        _merge_kernel,
        grid=(N // tm,),
        in_specs=[
            pl.BlockSpec((D_MODEL, tm), col),
            pl.BlockSpec((D_MODEL, LANE), lambda i: (0, 0)),
            pl.BlockSpec((N_MIXERS * D_MODEL, D_MODEL), lambda i: (0, 0)),
            pl.BlockSpec((MIX_WIDTH, tm), col),
            pl.BlockSpec((MIX_WIDTH, tm), col),
            pl.BlockSpec((MIX_WIDTH, tm), col),
            pl.BlockSpec((N_MIXERS, D_MODEL, MIX_WIDTH), lambda i: (0, 0, 0)),
            pl.BlockSpec((D_MODEL, D_MODEL), lambda i: (0, 0)),
        ],
        out_specs=pl.BlockSpec((D_MODEL, tm), col),
        out_shape=jax.ShapeDtypeStruct((D_MODEL, N), F32),
        compiler_params=_cparams(("parallel",)),
        name="merge_out_projection",
    )(xT, g_b, wgT, oa, ob, oc, wbT, woT)


def _mlp_kernel(x_ref, g_ref, wu_ref, wd_ref, out_ref, h_ref, acc_ref):
    f = pl.program_id(1)
    tm = x_ref.shape[1]

    @pl.when(f == 0)
    def _():
        x = x_ref[...]
        ss = jnp.sum(x * x, axis=0, keepdims=True)
        h_ref[...] = (x * lax.rsqrt(ss * (1.0 / D_MODEL) + EPS)
                      * _lane_tile(g_ref[...], tm // LANE)).astype(BF16)
        acc_ref[...] = jnp.zeros(acc_ref.shape, F32)

    u = jnp.maximum(jnp.dot(wu_ref[...], h_ref[...], preferred_element_type=F32), 0.0)
    acc_ref[...] += jnp.dot(wd_ref[...], (u * u).astype(BF16), preferred_element_type=F32)

    @pl.when(f == pl.num_programs(1) - 1)
    def _():
        out_ref[...] = x_ref[...] + acc_ref[...]


def _mlp(xT, g_b, wuT, wdT):
    N = xT.shape[1]
    tm, tf = 1024, 1024
    return pl.pallas_call(
        _mlp_kernel,
        grid=(N // tm, D_FF // tf),
        in_specs=[
            pl.BlockSpec((D_MODEL, tm), lambda i, f: (0, i)),
            pl.BlockSpec((D_MODEL, LANE), lambda i, f: (0, 0)),
            pl.BlockSpec((tf, D_MODEL), lambda i, f: (f, 0)),
            pl.BlockSpec((D_MODEL, tf), lambda i, f: (0, f)),
        ],
        out_specs=pl.BlockSpec((D_MODEL, tm), lambda i, f: (0, i)),
        out_shape=jax.ShapeDtypeStruct((D_MODEL, N), F32),
        scratch_shapes=[pltpu.VMEM((D_MODEL, tm), BF16), pltpu.VMEM((D_MODEL, tm), F32)],
        compiler_params=_cparams(("parallel", "arbitrary")),
        name="relu2_mlp",
    )(xT, g_b, wuT, wdT)


def _to_chunks(rows, B, T):
    half_tokens = CMP_LEN // 2
    t = rows.reshape(B, T, N_KV_HEADS, HEAD_DIM).transpose(0, 2, 1, 3)
    return t.reshape(B * N_KV_HEADS, T // half_tokens, half_tokens * HEAD_DIM)


def kernel(x, norm_mix, w_in, q_norm, k_norm, sinks, cmp_pe_k, cmp_pe_v, w_ck1, w_ck2, w_cv1, w_cv2,
           w_branch, w_out, norm_mlp, w_up, w_down):
    B, T, D = x.shape
    N = B * T
    depth = w_in.shape[0]
    nq = T // Q_BLOCK
    ncmp = T // CMP_STRIDE
    nsel = T // SLC_LEN

    inv_freq = ROPE_THETA ** (-jnp.arange(0, HEAD_DIM, 2, dtype=F32) / HEAD_DIM)
    ang = jnp.arange(T, dtype=F32)[:, None] * inv_freq[None, :]
    cosT, sinT = jnp.cos(ang).T, jnp.sin(ang).T

    ci = np.arange(KEY_CHUNK)
    ltri = jnp.asarray(ci[None, :] < ci[:, None], BF16)
    emat = jnp.asarray((np.arange(T)[:, None] // SLC_LEN) == np.arange(nsel)[None, :], BF16)
    cstart = np.arange(ncmp) * CMP_STRIDE
    sstart = np.arange(nsel) * SLC_LEN
    n_cmp_valid = (T - CMP_LEN) // CMP_STRIDE + 1
    ov = ((cstart[None, :] < sstart[:, None] + SLC_LEN) & (cstart[None, :] + CMP_LEN > sstart[:, None])
          & (np.arange(ncmp)[None, :] < n_cmp_valid))
    ovT = jnp.asarray(ov, BF16)

    xT = x.reshape(N, D).T

    sl = lambda s: w_in[:, :, s[0]:s[1]]
    wT_all = jnp.concatenate(
        [sl(S_QA), sl(S_QB), sl(S_QC),
         sl(S_KA), sl(S_KB), sl(S_KC), sl(S_KSL), sl(S_KWN),
         sl(S_VA), sl(S_VB), sl(S_VC), sl(S_VSL), sl(S_VWN),
         sl(S_IQ), sl(S_IK), sl(S_IW), jnp.zeros((depth, D, GATE_ROW0 - IDX_HEADS), w_in.dtype), sl(S_GC)],
        axis=2).transpose(0, 2, 1).astype(BF16)
    wgT_all = sl(S_GM).transpose(0, 2, 1).astype(BF16)
    wbT_all = w_branch.transpose(0, 1, 3, 2).astype(BF16)
    woT_all = w_out.transpose(0, 2, 1).astype(BF16)
    wuT_all = w_up.transpose(0, 2, 1).astype(BF16)
    wdT_all = w_down.transpose(0, 2, 1).astype(BF16)
    lane_b = lambda v: jnp.broadcast_to(v.astype(F32)[..., None], v.shape + (LANE,))
    gmix_all, gmlp_all = lane_b(norm_mix), lane_b(norm_mlp)
    gq_all = lane_b(jnp.tile(q_norm[:, :, None, :], (1, 1, N_Q_HEADS, 1)).reshape(depth, -1))
    gk_all = lane_b(jnp.tile(k_norm[:, jnp.array([0, 1, 2, 2, 2])][:, :, None, :],
                             (1, 1, N_KV_HEADS, 1)).reshape(depth, -1))
    sink_all = jnp.broadcast_to(jnp.repeat(sinks.astype(F32) * LOG2E, Q_BLOCK, axis=1)[:, None, :],
                                (depth, 8, N_Q_HEADS * Q_BLOCK))

    for l in range(depth):
        wgT, sink_lanes = wgT_all[l], sink_all[l]
        qT, kT, vT128, iqT, ikT, miscT = _project(
            xT, gmix_all[l], wT_all[l], gq_all[l], gk_all[l], cosT, sinT, T)
        k_rows = kT.T
        ik_rows = ikT.T

        o_a = _dsa(iqT, miscT, ik_rows, qT, k_rows, vT128, ltri, B, T)
        o_b = _band(qT, 1, k_rows, 1, vT128, sink_lanes, SWA_WINDOW, True, BF16, B, T, "swa_attention")
        o_w = _band(qT, 2, k_rows, 4, vT128, sink_lanes, NSA_WINDOW, False, F32, B, T, "nsa_window_attention")

        kc_chunks = _to_chunks(k_rows[:, 2 * KV_WIDTH:3 * KV_WIDTH].reshape(B, T, KV_WIDTH), B, T)
        vc_rows = vT128[:, 2 * KV_WIDTH:3 * KV_WIDTH, :].reshape(B, nq, KV_WIDTH, LANE)
        vc_rows = vc_rows.transpose(0, 1, 3, 2).reshape(B, T, KV_WIDTH)
        vc_chunks = _to_chunks(vc_rows, B, T)
        k_cmp = _compress(kc_chunks, cmp_pe_k[l], w_ck1[l], w_ck2[l], "compress_k")
        v_cmp = _compress(vc_chunks, cmp_pe_v[l], w_cv1[l], w_cv2[l], "compress_v")
        kcmp_rows = k_cmp.reshape(B, N_KV_HEADS, ncmp, HEAD_DIM).transpose(0, 2, 1, 3)
        kcmp_rows = kcmp_rows.reshape(B, ncmp, KV_WIDTH).astype(BF16)
        vcmpT = v_cmp.reshape(B, N_KV_HEADS, ncmp, HEAD_DIM).transpose(0, 1, 3, 2)
        vcmpT = vcmpT.reshape(B, KV_WIDTH, ncmp).astype(BF16)

        o_c = _nsa(qT, kcmp_rows, vcmpT, k_rows, vT128, miscT, o_w, ovT, emat, B, T)

        xT = _merge(xT, gmix_all[l], wgT, o_a, o_b, o_c, wbT_all[l], woT_all[l])
        xT = _mlp(xT, gmlp_all[l], wuT_all[l], wdT_all[l])

    return xT.T.reshape(B, T, D)
```

```python
import functools

import numpy as np
import jax
import jax.numpy as jnp
from jax import lax
from jax.experimental import pallas as pl
from jax.experimental.pallas import tpu as pltpu

F32 = jnp.float32
BF16 = jnp.bfloat16

D_MODEL = 1024
HEAD_DIM = 64
HALF = HEAD_DIM // 2
N_Q_HEADS = 8
N_KV_HEADS = 2
GROUP = N_Q_HEADS // N_KV_HEADS
MIX_WIDTH = N_Q_HEADS * HEAD_DIM
KV_WIDTH = N_KV_HEADS * HEAD_DIM
N_MIXERS = 3
D_FF = 4 * D_MODEL
ROPE_THETA = 10000.0
EPS = 1e-6
NEG_INF = -1e30
MASK_VAL = -2e30
M_INIT = -1e30
BIG = 3e38
Q_BLOCK = 128
KEY_CHUNK = 512
BISECT_COUNT_STEPS = 14
BISECT_FIXED_STEPS = 2
ONES_ROWS = 16
IDX_HEADS = 4
IDX_DIM = 64
DSA_TOPK_MAX = 256
SWA_WINDOW = 128
CMP_LEN = 32
CMP_STRIDE = 16
CMP_HIDDEN = 256
SLC_LEN = 64
SLC_TOPN = 16
NSA_WINDOW = 512
FORCE_SCORE = 1e9
LOG2E = 1.4426950408889634
Q_SCALE = (HEAD_DIM ** -0.5) * LOG2E
LANE = 128
LANE_SHIFT = 7
ATT_BIG_STEP = 1024
VMEM_LIMIT = 56 * 1024 * 1024

IN_SIZES = (
    MIX_WIDTH, KV_WIDTH, KV_WIDTH, IDX_HEADS * IDX_DIM, IDX_DIM, IDX_HEADS,
    MIX_WIDTH, KV_WIDTH, KV_WIDTH,
    MIX_WIDTH, KV_WIDTH, KV_WIDTH, KV_WIDTH, KV_WIDTH, KV_WIDTH, KV_WIDTH, 3 * N_Q_HEADS,
    N_MIXERS * D_MODEL,
)
_OFF = np.concatenate([[0], np.cumsum(IN_SIZES)]).astype(int)
(S_QA, S_KA, S_VA, S_IQ, S_IK, S_IW, S_QB, S_KB, S_VB,
 S_QC, S_KC, S_VC, S_KSL, S_VSL, S_KWN, S_VWN, S_GC, S_GM) = [
    (int(_OFF[i]), int(_OFF[i + 1])) for i in range(len(IN_SIZES))]

R_Q = 0
R_K = R_Q + N_MIXERS * MIX_WIDTH
R_V = R_K + 5 * KV_WIDTH
R_IQ = R_V + 5 * KV_WIDTH
R_IK = R_IQ + IDX_HEADS * IDX_DIM
R_MISC = R_IK + IDX_DIM
MISC_ROWS = 32
GATE_ROW0 = 8
R_END = R_MISC + MISC_ROWS
PROJ_TM = 1024


def _cparams(sem):
    return pltpu.CompilerParams(dimension_semantics=sem, vmem_limit_bytes=VMEM_LIMIT)


def _lane_tile(a, n):
    return a if n == 1 else jnp.concatenate([a] * n, axis=1)


def _proj_kernel(x_ref, g_ref, w_ref, gq_ref, gk_ref, cos_ref, sin_ref,
                 q_ref, k_ref, v_ref, iq_ref, ik_ref, misc_ref):
    tm = x_ref.shape[1]
    rep = tm // LANE
    x = x_ref[...]
    ss = jnp.sum(x * x, axis=0, keepdims=True)
    h = (x * lax.rsqrt(ss * (1.0 / D_MODEL) + EPS) * _lane_tile(g_ref[...], rep)).astype(BF16)
    cos = cos_ref[...]
    sin = sin_ref[...]

    def rope(z):
        z1, z2 = z[:HALF], z[HALF:]
        return jnp.concatenate([z1 * cos - z2 * sin, z2 * cos + z1 * sin], axis=0)

    def head_norm(z, gain):
        ssq = jnp.sum(z * z, axis=0, keepdims=True)
        return z * lax.rsqrt(ssq * (1.0 / HEAD_DIM) + EPS) * _lane_tile(gain, rep)

    for m in range(N_MIXERS):
        z = jnp.dot(w_ref[R_Q + m * MIX_WIDTH:R_Q + (m + 1) * MIX_WIDTH, :], h,
                    preferred_element_type=F32)
        for hh in range(N_Q_HEADS):
            r0 = m * MIX_WIDTH + hh * HEAD_DIM
            zh = head_norm(z[hh * HEAD_DIM:(hh + 1) * HEAD_DIM], gq_ref[r0:r0 + HEAD_DIM, :])
            q_ref[r0:r0 + HEAD_DIM, :] = (rope(zh) * Q_SCALE).astype(BF16)

    z = jnp.dot(w_ref[R_K:R_V, :], h, preferred_element_type=F32)
    for mk in range(5):
        heads = []
        for kvh in range(N_KV_HEADS):
            r0 = mk * KV_WIDTH + kvh * HEAD_DIM
            heads.append(rope(head_norm(z[r0:r0 + HEAD_DIM], gk_ref[r0:r0 + HEAD_DIM, :])))
        k_ref[:, mk * KV_WIDTH:(mk + 1) * KV_WIDTH] = jnp.concatenate(heads, axis=0).T.astype(BF16)

    z = jnp.dot(w_ref[R_V:R_IQ, :], h, preferred_element_type=F32)
    for j in range(rep):
        v_ref[j] = z[:, j * LANE:(j + 1) * LANE].astype(BF16)

    z = jnp.dot(w_ref[R_IQ:R_END, :], h, preferred_element_type=F32)
    for hh in range(IDX_HEADS):
        r0 = hh * IDX_DIM
        iq_ref[r0:r0 + IDX_DIM, :] = rope(z[r0:r0 + IDX_DIM])
    ik = rope(z[R_IK - R_IQ:R_MISC - R_IQ])
    ik_ref[...] = jnp.concatenate([ik, jnp.zeros((LANE - IDX_DIM, tm), F32)], axis=0).T
    misc_ref[...] = z[R_MISC - R_IQ:R_END - R_IQ]


def _project(xT, g_b, wT, gq_b, gk_b, cosT, sinT, T):
    N = xT.shape[1]
    tm = PROJ_TM
    tpb = T // tm
    rep = tm // LANE
    col = lambda i: (0, i)
    const = lambda i: (0, 0)
    return pl.pallas_call(
        _proj_kernel,
        grid=(N // tm,),
        in_specs=[
            pl.BlockSpec((D_MODEL, tm), col),
            pl.BlockSpec((D_MODEL, LANE), const),
            pl.BlockSpec((R_END, D_MODEL), const),
            pl.BlockSpec((N_MIXERS * MIX_WIDTH, LANE), const),
            pl.BlockSpec((5 * KV_WIDTH, LANE), const),
            pl.BlockSpec((HALF, tm), lambda i: (0, i % tpb)),
            pl.BlockSpec((HALF, tm), lambda i: (0, i % tpb)),
        ],
        out_specs=[
            pl.BlockSpec((N_MIXERS * MIX_WIDTH, tm), col),
            pl.BlockSpec((tm, 5 * KV_WIDTH), lambda i: (i, 0)),
            pl.BlockSpec((rep, 5 * KV_WIDTH, LANE), lambda i: (i, 0, 0)),
            pl.BlockSpec((IDX_HEADS * IDX_DIM, tm), col),
            pl.BlockSpec((tm, LANE), lambda i: (i, 0)),
            pl.BlockSpec((MISC_ROWS, tm), col),
        ],
        out_shape=[
            jax.ShapeDtypeStruct((N_MIXERS * MIX_WIDTH, N), BF16),
            jax.ShapeDtypeStruct((N, 5 * KV_WIDTH), BF16),
            jax.ShapeDtypeStruct((N // LANE, 5 * KV_WIDTH, LANE), BF16),
            jax.ShapeDtypeStruct((IDX_HEADS * IDX_DIM, N), F32),
            jax.ShapeDtypeStruct((N, LANE), F32),
            jax.ShapeDtypeStruct((MISC_ROWS, N), F32),
        ],
        compiler_params=_cparams(("parallel",)),
        name="in_projection",
    )(xT, g_b, wT, gq_b, gk_b, cosT, sinT)


def _padded_queries(q):
    zeros = jnp.zeros((HEAD_DIM, Q_BLOCK), q.dtype)
    out = []
    for kv in range(N_KV_HEADS):
        cols = []
        for g in range(GROUP):
            hq = kv * GROUP + g
            qh = q[hq * HEAD_DIM:(hq + 1) * HEAD_DIM, :]
            cols.append(jnp.concatenate([qh, zeros] if kv == 0 else [zeros, qh], axis=0))
        out.append(jnp.concatenate(cols, axis=1))
    return out


def _v_chunk(v_ref, blk0, nblk):
    return jnp.concatenate([v_ref[blk0 + j] for j in range(nblk)], axis=1)


def _walk_keys(nch, fn, carry):
    C, CB = KEY_CHUNK, ATT_BIG_STEP
    nbig = nch // (CB // C)
    nsmall = nch - nbig * (CB // C)
    small0 = nbig * CB
    carry = lax.fori_loop(0, nbig, lambda c, x: fn(pl.multiple_of(c * CB, CB), CB, x), carry)
    return lax.fori_loop(0, nsmall, lambda c, x: fn(pl.multiple_of(small0 + c * C, C), C, x), carry)


def _masked_attention(nch, sel_fn, k_ref, v_ref, qall, s_ref, acc_ref):
    lanes = N_KV_HEADS * GROUP * Q_BLOCK
    walk = functools.partial(_walk_keys, nch)

    def qk_step(off, size, mx):
        s = jnp.dot(k_ref[pl.ds(off, size), :], qall, preferred_element_type=F32) + sel_fn(off, size)
        s_ref[pl.ds(off, size), :] = s
        return jnp.maximum(mx, jnp.max(s, axis=0, keepdims=True))

    mx = walk(qk_step, jnp.full((1, lanes), M_INIT, F32))
    acc_ref[...] = jnp.zeros(acc_ref.shape, F32)

    def pv_step(off, size, carry):
        p = jnp.exp2(s_ref[pl.ds(off, size), :] - mx).astype(BF16)
        vc = jnp.concatenate([_v_chunk(v_ref, lax.shift_right_logical(off, LANE_SHIFT), size // LANE),
                              jnp.ones((ONES_ROWS, size), BF16)], axis=0)
        acc_ref[...] += jnp.dot(vc, p, preferred_element_type=F32)
        return carry

    walk(pv_step, 0)
    l = acc_ref[KV_WIDTH:KV_WIDTH + 1, :]
    inv = jnp.where(l > 0.0, 1.0 / l, 0.0)
    half = GROUP * Q_BLOCK
    return [acc_ref[kv * HEAD_DIM:(kv + 1) * HEAD_DIM, kv * half:(kv + 1) * half]
            * inv[:, kv * half:(kv + 1) * half] for kv in range(N_KV_HEADS)]


def _dsa_kernel(iq_ref, misc_ref, ik_ref, q_ref, k_ref, v_ref, ltri_ref, o_ref,
                sc_ref, s_ref, acc_ref, *, k_top):
    C = KEY_CHUNK
    i = pl.program_id(1)
    nch = i // (C // Q_BLOCK) + 1
    qpos = i * Q_BLOCK + lax.broadcasted_iota(jnp.int32, (1, Q_BLOCK), 1)

    iq = iq_ref[...].astype(BF16)
    iq_cat = jnp.concatenate([iq[h * IDX_DIM:(h + 1) * IDX_DIM] for h in range(IDX_HEADS)], axis=1)
    w = misc_ref[0:IDX_HEADS, :] * ((IDX_HEADS ** -0.5) * (IDX_DIM ** -0.5))

    def fold(x):
        return x.reshape(x.shape[0] // 8, 8, Q_BLOCK)

    part = lambda v: jnp.full((8, Q_BLOCK), v, F32)

    def score_step(off, size, carry):
        lo, hi = carry
        ikc = ik_ref[pl.ds(off, size), 0:IDX_DIM].astype(BF16)
        d = jnp.dot(ikc, iq_cat, preferred_element_type=F32)
        s = jnp.maximum(d[:, 0:Q_BLOCK], 0.0) * w[0:1, :]
        for h in range(1, IDX_HEADS):
            s = s + jnp.maximum(d[:, h * Q_BLOCK:(h + 1) * Q_BLOCK], 0.0) * w[h:h + 1, :]
        causal = (off + lax.broadcasted_iota(jnp.int32, (size, Q_BLOCK), 0)) <= qpos
        sc_ref[pl.ds(off, size), :] = jnp.where(causal, s, NEG_INF)
        lo = jnp.minimum(lo, jnp.min(fold(jnp.where(causal, s, BIG)), axis=0))
        hi = jnp.maximum(hi, jnp.max(fold(jnp.where(causal, s, -BIG)), axis=0))
        return lo, hi

    lo0, hi0 = _walk_keys(nch, score_step, (part(BIG), part(-BIG)))
    lo0 = jnp.min(lo0, axis=0, keepdims=True)
    hi0 = jnp.max(hi0, axis=0, keepdims=True)

    kf = float(k_top)
    select = i * Q_BLOCK >= k_top

    def count_ge(mid):
        mid8 = jnp.broadcast_to(mid, (8, Q_BLOCK))[None]

        def body(c, cnts):
            cnts = list(cnts)
            for j in range(C // 64):
                off = pl.multiple_of(c * C + j * 64, 64)
                s = sc_ref[pl.ds(off, 64), :].reshape(8, 8, Q_BLOCK)
                cnts[j % 4] = cnts[j % 4] + jnp.sum(jnp.where(s >= mid8, 1.0, 0.0), axis=0)
            return tuple(cnts)

        cnts = lax.fori_loop(0, nch, body, (part(0.0),) * 4)
        return jnp.sum((cnts[0] + cnts[1]) + (cnts[2] + cnts[3]), axis=0, keepdims=True)

    def count_step(_, st):
        lo, hi, clo, chi, moved = st
        mid = lo + (hi - lo) * 0.5
        cnt = count_ge(mid)
        up = cnt >= kf
        return (jnp.where(up, mid, lo), jnp.where(up, hi, mid), jnp.where(up, cnt, clo),
                jnp.where(up, chi, cnt), jnp.where(up, moved, 1.0))

    def stats(mid):
        mid8 = jnp.broadcast_to(mid, (8, Q_BLOCK))[None]

        def body(c, carry):
            cnt, a, b = carry
            for j in range(C // 64):
                off = pl.multiple_of(c * C + j * 64, 64)
                s = sc_ref[pl.ds(off, 64), :].reshape(8, 8, Q_BLOCK)
                gt = s > mid8
                cnt = cnt + jnp.sum(jnp.where(gt, 1.0, 0.0), axis=0)
                a = jnp.minimum(a, jnp.min(jnp.where(gt, s, BIG), axis=0))
                b = jnp.maximum(b, jnp.max(jnp.where(gt, -BIG, s), axis=0))
            return cnt, a, b

        cnt, a, b = lax.fori_loop(0, nch, body, (part(0.0), part(BIG), part(-BIG)))
        return (jnp.sum(cnt, axis=0, keepdims=True), jnp.min(a, axis=0, keepdims=True),
                jnp.max(b, axis=0, keepdims=True))

    def step(st):
        lo, hi, clo, chi = st
        mid = jnp.maximum(lo + (hi - lo) * 0.5, lo)
        mid = jnp.where(mid >= hi, lo, mid)
        cnt, a, b = stats(mid)
        up = cnt >= kf
        return (jnp.where(up, a, lo), jnp.where(up, hi, b),
                jnp.where(up, cnt, clo), jnp.where(up, chi, cnt))

    def cond(st):
        return jnp.logical_and(select, jnp.max(st[1] - st[0]) > 0.0)

    n_causal = (qpos + 1).astype(F32)
    zero = jnp.zeros((1, Q_BLOCK), F32)
    lo_v, hi_v, clo, chi, moved = lax.fori_loop(
        0, jnp.where(select, BISECT_COUNT_STEPS, 0), count_step, (lo0, hi0, n_causal, zero, zero))

    def snap_chunk(c, carry):
        lo, hi = carry
        s = fold(sc_ref[pl.ds(pl.multiple_of(c * C, C), C), :])
        lo = jnp.minimum(lo, jnp.min(jnp.where(s >= lo_v[None], s, BIG), axis=0))
        hi = jnp.maximum(hi, jnp.max(jnp.where(s < hi_v[None], s, -BIG), axis=0))
        return lo, hi

    lo1, hi1 = lax.fori_loop(0, jnp.where(select, nch, 0), snap_chunk, (part(BIG), part(-BIG)))
    lo1 = jnp.min(lo1, axis=0, keepdims=True)
    hi1 = jnp.where(moved > 0.0, jnp.max(hi1, axis=0, keepdims=True), hi0)
    st = lax.fori_loop(0, jnp.where(select, BISECT_FIXED_STEPS, 0), lambda _, s: step(s),
                       (lo1, hi1, clo, chi))
    lo, _, clo, chi = lax.while_loop(cond, lambda s: step(step(s)), st)
    thr = jnp.where(select, lo, 0.5 * NEG_INF)
    need = jnp.where(select, kf - chi, 0.0)
    ties = jnp.logical_and(select, jnp.max((clo - chi) - need) > 0.0)

    @pl.when(ties)
    def _():
        def mask_chunk(c, tie_carry):
            off = pl.multiple_of(c * C, C)
            s_idx = sc_ref[pl.ds(off, C), :]
            eqf = jnp.where(s_idx == thr, 1.0, 0.0)
            before = jnp.dot(ltri_ref[...], eqf.astype(BF16), preferred_element_type=F32) + tie_carry
            sc_ref[pl.ds(off, C), :] = jnp.where(s_idx > thr, 1.0, jnp.where(before < need, eqf, 0.0))
            return tie_carry + jnp.sum(eqf, axis=0, keepdims=True)

        lax.fori_loop(0, nch, mask_chunk, jnp.zeros((1, Q_BLOCK), F32))

    thr_sel = jnp.where(ties, 0.5, thr)
    qall = jnp.concatenate(_padded_queries(q_ref[...]), axis=1)

    def sel_fn(off, size):
        bias = jnp.where(sc_ref[pl.ds(off, size), :] >= thr_sel, 0.0, MASK_VAL)
        return _lane_tile(bias, N_KV_HEADS * GROUP)

    outs = _masked_attention(nch, sel_fn, k_ref, v_ref, qall, s_ref, acc_ref)
    for kv in range(N_KV_HEADS):
        o = outs[kv]
        for g in range(GROUP):
            hq = kv * GROUP + g
            o_ref[hq * HEAD_DIM:(hq + 1) * HEAD_DIM, :] = o[:, g * Q_BLOCK:(g + 1) * Q_BLOCK].astype(BF16)


def _dsa(iqT, miscT, ik_rows, qT, k_rows, vT128, ltri, B, T):
    nq = T // Q_BLOCK
    k_top = min(DSA_TOPK_MAX, T // 4)
    N = B * T
    return pl.pallas_call(
        functools.partial(_dsa_kernel, k_top=k_top),
        grid=(B, nq),
        in_specs=[
            pl.BlockSpec((IDX_HEADS * IDX_DIM, Q_BLOCK), lambda b, i: (0, b * nq + i)),
            pl.BlockSpec((MISC_ROWS, Q_BLOCK), lambda b, i: (0, b * nq + i)),
            pl.BlockSpec((T, LANE), lambda b, i: (b, 0)),
            pl.BlockSpec((MIX_WIDTH, Q_BLOCK), lambda b, i: (0, b * nq + i)),
            pl.BlockSpec((T, KV_WIDTH), lambda b, i: (b, 0)),
            pl.BlockSpec((nq, KV_WIDTH, LANE), lambda b, i: (b, 0, 0)),
            pl.BlockSpec((KEY_CHUNK, KEY_CHUNK), lambda b, i: (0, 0)),
        ],
        out_specs=pl.BlockSpec((MIX_WIDTH, Q_BLOCK), lambda b, i: (0, b * nq + i)),
        out_shape=jax.ShapeDtypeStruct((MIX_WIDTH, N), BF16),
        scratch_shapes=[
            pltpu.VMEM((T, Q_BLOCK), F32),
            pltpu.VMEM((T, N_KV_HEADS * GROUP * Q_BLOCK), F32),
            pltpu.VMEM((KV_WIDTH + ONES_ROWS, N_KV_HEADS * GROUP * Q_BLOCK), F32),
        ],
        compiler_params=_cparams(("parallel", "arbitrary")),
        name="dsa_attention",
    )(iqT, miscT, ik_rows, qT, k_rows, vT128, ltri)


def _band_kernel(q_ref, k_ref, v_ref, sink_ref, o_ref, *, n_prev, window, use_sink):
    i = pl.program_id(1)
    blk0 = jnp.maximum(i - n_prev, 0)
    nblk = n_prev + 1
    wb = nblk * Q_BLOCK
    kb = k_ref[pl.ds(pl.multiple_of(blk0 * Q_BLOCK, Q_BLOCK), wb), :]
    vb = jnp.concatenate([_v_chunk(v_ref, blk0, nblk), jnp.ones((ONES_ROWS, wb), BF16)], axis=0)
    qall = jnp.concatenate(_padded_queries(q_ref[...]), axis=1)
    rel = ((i - blk0) * Q_BLOCK + lax.broadcasted_iota(jnp.int32, (wb, Q_BLOCK), 1)
           - lax.broadcasted_iota(jnp.int32, (wb, Q_BLOCK), 0))
    bias = jnp.where((rel >= 0) & (rel < window), 0.0, MASK_VAL)
    s = jnp.dot(kb, qall, preferred_element_type=F32) + _lane_tile(bias, N_KV_HEADS * GROUP)
    m = jnp.max(s, axis=0, keepdims=True)
    if use_sink:
        sink = sink_ref[0:1, :]
        m = jnp.maximum(m, sink)
    o = jnp.dot(vb, jnp.exp2(s - m).astype(BF16), preferred_element_type=F32)
    l = o[KV_WIDTH:KV_WIDTH + 1, :]
    if use_sink:
        l = l + jnp.exp2(sink - m)
    inv = 1.0 / l
    half = GROUP * Q_BLOCK
    for kv in range(N_KV_HEADS):
        for g in range(GROUP):
            hq = kv * GROUP + g
            lanes = slice(kv * half + g * Q_BLOCK, kv * half + (g + 1) * Q_BLOCK)
            o_ref[hq * HEAD_DIM:(hq + 1) * HEAD_DIM, :] = (
                o[kv * HEAD_DIM:(kv + 1) * HEAD_DIM, lanes] * inv[:, lanes]).astype(o_ref.dtype)


def _band(qT, q_mixer, k_rows, kv_index, vT128, sink_lanes, window, use_sink, out_dtype, B, T, name):
    nq = T // Q_BLOCK
    N = B * T
    n_prev = (window + Q_BLOCK - 2) // Q_BLOCK
    return pl.pallas_call(
        functools.partial(_band_kernel, n_prev=n_prev, window=window, use_sink=use_sink),
        grid=(B, nq),
        in_specs=[
            pl.BlockSpec((MIX_WIDTH, Q_BLOCK), lambda b, i: (q_mixer, b * nq + i)),
            pl.BlockSpec((T, KV_WIDTH), lambda b, i: (b, kv_index)),
            pl.BlockSpec((nq, KV_WIDTH, LANE), lambda b, i: (b, kv_index, 0)),
            pl.BlockSpec((8, N_Q_HEADS * Q_BLOCK), lambda b, i: (0, 0)),
        ],
        out_specs=pl.BlockSpec((MIX_WIDTH, Q_BLOCK), lambda b, i: (0, b * nq + i)),
        out_shape=jax.ShapeDtypeStruct((MIX_WIDTH, N), out_dtype),
        compiler_params=_cparams(("parallel", "arbitrary")),
        name=name,
    )(qT, k_rows, vT128, sink_lanes)


def _cmp_kernel(xa_ref, xb_ref, pe_ref, w1_ref, w2_ref, o_ref):
    half = (CMP_LEN // 2) * HEAD_DIM
    xa = (xa_ref[...].astype(F32) + pe_ref[0:1, :]).astype(BF16)
    xb = (xb_ref[...].astype(F32) + pe_ref[1:2, :]).astype(BF16)
    hid = (jnp.dot(xa, w1_ref[0:half, :], preferred_element_type=F32)
           + jnp.dot(xb, w1_ref[half:2 * half, :], preferred_element_type=F32))
    hid = jnp.maximum(hid, 0.0).astype(BF16)
    o_ref[...] = jnp.dot(hid, w2_ref[...], preferred_element_type=F32)


def _compress(chunks, pe, w1, w2, name):
    G, nc, half = chunks.shape
    nxt = jnp.concatenate([chunks[:, 1:], jnp.zeros((G, 1, half), chunks.dtype)], axis=1)
    pe2 = jnp.zeros((8, half), F32).at[0:2].set(pe.reshape(2, half))
    return pl.pallas_call(
        _cmp_kernel,
        grid=(G,),
        in_specs=[
            pl.BlockSpec((None, nc, half), lambda g: (g, 0, 0)),
            pl.BlockSpec((None, nc, half), lambda g: (g, 0, 0)),
            pl.BlockSpec((8, half), lambda g: (0, 0)),
            pl.BlockSpec((2 * half, CMP_HIDDEN), lambda g: (0, 0)),
            pl.BlockSpec((CMP_HIDDEN, HEAD_DIM), lambda g: (0, 0)),
        ],
        out_specs=pl.BlockSpec((None, nc, HEAD_DIM), lambda g: (g, 0, 0)),
        out_shape=jax.ShapeDtypeStruct((G, nc, HEAD_DIM), F32),
        compiler_params=_cparams(("parallel",)),
        name=name,
    )(chunks, nxt, pe2, w1.astype(BF16), w2.astype(BF16))


def _nsa_kernel(q_ref, kc_ref, vc_ref, ks_ref, vs_ref, misc_ref, ow_ref, ov_ref, e_ref, o_ref,
                s_ref, acc_ref, *, n_top):
    C = KEY_CHUNK
    i = pl.program_id(1)
    nch = i // (C // Q_BLOCK) + 1
    ncmp = kc_ref.shape[0]
    nsel = ov_ref.shape[0]
    lanes = GROUP * Q_BLOCK
    t = i * Q_BLOCK + lax.broadcasted_iota(jnp.int32, (1, Q_BLOCK), 1)
    t4 = i * Q_BLOCK + (lax.broadcasted_iota(jnp.int32, (1, lanes), 1) & (Q_BLOCK - 1))
    crow = lax.broadcasted_iota(jnp.int32, (ncmp, lanes), 0)
    jrow = lax.broadcasted_iota(jnp.int32, (nsel, Q_BLOCK), 0)
    cur = t // SLC_LEN
    qpad = _padded_queries(q_ref[...])
    kc = kc_ref[...]
    vcm = vc_ref[...]
    ov = ov_ref[...]

    o_cmps, selbs = [], []
    for kv in range(N_KV_HEADS):
        s = jnp.dot(kc, qpad[kv], preferred_element_type=F32)
        s = jnp.where(crow * CMP_STRIDE + (CMP_LEN - 1) <= t4, s, MASK_VAL)
        m = jnp.maximum(jnp.max(s, axis=0, keepdims=True), M_INIT)
        p = jnp.exp2(s - m)
        l = jnp.sum(p, axis=0, keepdims=True)
        p = p * jnp.where(l > 0.0, 1.0 / l, 0.0)
        o_cmp = jnp.dot(vcm, p.astype(BF16), preferred_element_type=F32)[kv * HEAD_DIM:(kv + 1) * HEAD_DIM, :]

        pg = p[:, 0:Q_BLOCK]
        for g in range(1, GROUP):
            pg = pg + p[:, g * Q_BLOCK:(g + 1) * Q_BLOCK]
        p_hi = pg.astype(BF16)
        p_lo = (pg - p_hi.astype(F32)).astype(BF16)
        imp = (jnp.dot(ov, p_hi, preferred_element_type=F32)
               + jnp.dot(ov, p_lo, preferred_element_type=F32))
        forced = (jrow == 0) | (jrow == cur) | (jrow == cur - 1)
        imp = jnp.where(forced, FORCE_SCORE, imp)
        imp = jnp.where(jrow * SLC_LEN <= t, imp, NEG_INF)
        tiles = [imp[r0:r0 + 8] for r0 in range(0, nsel, 8)]
        ranks = [jnp.zeros((8, Q_BLOCK), F32) for _ in tiles]
        sub = lax.broadcasted_iota(jnp.int32, (8, Q_BLOCK), 0)
        for ii in range(nsel):
            r = jnp.broadcast_to(imp[ii:ii + 1], (8, Q_BLOCK))
            for jt, tile in enumerate(tiles):
                ge = jnp.where(r >= tile, 1.0, 0.0)
                gt = jnp.where(r > tile, 1.0, 0.0)
                if jt > ii // 8:
                    ahead = ge
                elif jt < ii // 8:
                    ahead = gt
                else:
                    ahead = jnp.where(sub > ii % 8, ge, gt)
                ranks[jt] = ranks[jt] + ahead
        rank = jnp.concatenate(ranks, axis=0)
        selbs.append(jnp.where(rank < float(n_top), 1.0, 0.0).astype(BF16))
        o_cmps.append(o_cmp)

    qall = jnp.concatenate(qpad, axis=1)
    selb_all = jnp.concatenate(selbs, axis=1)

    def sel_fn(off, size):
        mk = jnp.dot(e_ref[pl.ds(off, size), :], selb_all, preferred_element_type=F32)
        causal = (off + lax.broadcasted_iota(jnp.int32, (size, Q_BLOCK), 0)) <= t
        parts = []
        for kv in range(N_KV_HEADS):
            keep = jnp.where(causal, mk[:, kv * Q_BLOCK:(kv + 1) * Q_BLOCK], 0.0) > 0.5
            parts.append(_lane_tile(jnp.where(keep, 0.0, MASK_VAL), GROUP))
        return jnp.concatenate(parts, axis=1)

    o_slcs = _masked_attention(nch, sel_fn, ks_ref, vs_ref, qall, s_ref, acc_ref)

    for kv in range(N_KV_HEADS):
        for g in range(GROUP):
            hq = kv * GROUP + g
            gr = GATE_ROW0 + hq * 3
            gate = [1.0 / (1.0 + jnp.exp(-misc_ref[gr + br:gr + br + 1, :])) for br in range(3)]
            out = (gate[0] * o_cmps[kv][:, g * Q_BLOCK:(g + 1) * Q_BLOCK]
                   + gate[1] * o_slcs[kv][:, g * Q_BLOCK:(g + 1) * Q_BLOCK]
                   + gate[2] * ow_ref[hq * HEAD_DIM:(hq + 1) * HEAD_DIM, :])
            o_ref[hq * HEAD_DIM:(hq + 1) * HEAD_DIM, :] = out.astype(BF16)


def _nsa(qT, kcmp_rows, vcmpT, k_rows, vT128, miscT, o_winT, ovT, emat, B, T):
    nq = T // Q_BLOCK
    N = B * T
    ncmp = T // CMP_STRIDE
    nsel = T // SLC_LEN
    n_top = min(SLC_TOPN, nsel)
    return pl.pallas_call(
        functools.partial(_nsa_kernel, n_top=n_top),
        grid=(B, nq),
        in_specs=[
            pl.BlockSpec((MIX_WIDTH, Q_BLOCK), lambda b, i: (2, b * nq + i)),
            pl.BlockSpec((None, ncmp, KV_WIDTH), lambda b, i: (b, 0, 0)),
            pl.BlockSpec((None, KV_WIDTH, ncmp), lambda b, i: (b, 0, 0)),
            pl.BlockSpec((T, KV_WIDTH), lambda b, i: (b, 3)),
            pl.BlockSpec((nq, KV_WIDTH, LANE), lambda b, i: (b, 3, 0)),
            pl.BlockSpec((MISC_ROWS, Q_BLOCK), lambda b, i: (0, b * nq + i)),
            pl.BlockSpec((MIX_WIDTH, Q_BLOCK), lambda b, i: (0, b * nq + i)),
            pl.BlockSpec((nsel, ncmp), lambda b, i: (0, 0)),
            pl.BlockSpec((T, nsel), lambda b, i: (0, 0)),
        ],
        out_specs=pl.BlockSpec((MIX_WIDTH, Q_BLOCK), lambda b, i: (0, b * nq + i)),
        out_shape=jax.ShapeDtypeStruct((MIX_WIDTH, N), BF16),
        scratch_shapes=[
            pltpu.VMEM((T, N_KV_HEADS * GROUP * Q_BLOCK), F32),
            pltpu.VMEM((KV_WIDTH + ONES_ROWS, N_KV_HEADS * GROUP * Q_BLOCK), F32),
        ],
        compiler_params=_cparams(("parallel", "arbitrary")),
        name="nsa_attention",
    )(qT, kcmp_rows, vcmpT, k_rows, vT128, miscT, o_winT, ovT, emat)


def _merge_kernel(x_ref, g_ref, wg_ref, oa_ref, ob_ref, oc_ref, wb_ref, wo_ref, out_ref):
    tm = x_ref.shape[1]
    x = x_ref[...]
    ss = jnp.sum(x * x, axis=0, keepdims=True)
    h = (x * lax.rsqrt(ss * (1.0 / D_MODEL) + EPS) * _lane_tile(g_ref[...], tm // LANE)).astype(BF16)
    acc = None
    for n, o_ref in enumerate((oa_ref, ob_ref, oc_ref)):
        gm = jnp.dot(wg_ref[n * D_MODEL:(n + 1) * D_MODEL, :], h, preferred_element_type=F32)
        y = jnp.dot(wb_ref[n], o_ref[...], preferred_element_type=F32)
        term = y * (1.0 / (1.0 + jnp.exp(-gm)))
        acc = term if acc is None else acc + term
    out_ref[...] = x + jnp.dot(wo_ref[...], acc.astype(BF16), preferred_element_type=F32)


def _merge(xT, g_b, wgT, oa, ob, oc, wbT, woT):
    N = xT.shape[1]
    tm = 512
    col = lambda i: (0, i)
    return pl.pallas_call(
        _merge_kernel,
        grid=(N // tm,),
        in_specs=[
            pl.BlockSpec((D_MODEL, tm), col),
            pl.BlockSpec((D_MODEL, LANE), lambda i: (0, 0)),
            pl.BlockSpec((N_MIXERS * D_MODEL, D_MODEL), lambda i: (0, 0)),
            pl.BlockSpec((MIX_WIDTH, tm), col),
            pl.BlockSpec((MIX_WIDTH, tm), col),
            pl.BlockSpec((MIX_WIDTH, tm), col),
            pl.BlockSpec((N_MIXERS, D_MODEL, MIX_WIDTH), lambda i: (0, 0, 0)),
            pl.BlockSpec((D_MODEL, D_MODEL), lambda i: (0, 0)),
        ],
        out_specs=pl.BlockSpec((D_MODEL, tm), col),
        out_shape=jax.ShapeDtypeStruct((D_MODEL, N), F32),
        compiler_params=_cparams(("parallel",)),
        name="merge_out_projection",
    )(xT, g_b, wgT, oa, ob, oc, wbT, woT)


def _mlp_kernel(x_ref, g_ref, wu_ref, wd_ref, out_ref, h_ref, acc_ref, *, token_major_out):
    f = pl.program_id(1)
    tm = x_ref.shape[1]

    @pl.when(f == 0)
    def _():
        x = x_ref[...]
        ss = jnp.sum(x * x, axis=0, keepdims=True)
        h_ref[...] = (x * lax.rsqrt(ss * (1.0 / D_MODEL) + EPS)
                      * _lane_tile(g_ref[...], tm // LANE)).astype(BF16)
        acc_ref[...] = jnp.zeros(acc_ref.shape, F32)

    u = jnp.maximum(jnp.dot(wu_ref[...], h_ref[...], preferred_element_type=F32), 0.0)
    acc_ref[...] += jnp.dot(wd_ref[...], (u * u).astype(BF16), preferred_element_type=F32)

    @pl.when(f == pl.num_programs(1) - 1)
    def _():
        y = x_ref[...] + acc_ref[...]
        out_ref[...] = y.T if token_major_out else y


def _mlp(xT, g_b, wuT, wdT, token_major_out):
    N = xT.shape[1]
    tm, tf = 1024, 1024
    if token_major_out:
        out_spec = pl.BlockSpec((tm, D_MODEL), lambda i, f: (i, 0))
        out_shape = jax.ShapeDtypeStruct((N, D_MODEL), F32)
    else:
        out_spec = pl.BlockSpec((D_MODEL, tm), lambda i, f: (0, i))
        out_shape = jax.ShapeDtypeStruct((D_MODEL, N), F32)
    return pl.pallas_call(
        functools.partial(_mlp_kernel, token_major_out=token_major_out),
        grid=(N // tm, D_FF // tf),
        in_specs=[
            pl.BlockSpec((D_MODEL, tm), lambda i, f: (0, i)),
            pl.BlockSpec((D_MODEL, LANE), lambda i, f: (0, 0)),
            pl.BlockSpec((tf, D_MODEL), lambda i, f: (f, 0)),
            pl.BlockSpec((D_MODEL, tf), lambda i, f: (0, f)),
        ],
        out_specs=out_spec,
        out_shape=out_shape,
        scratch_shapes=[pltpu.VMEM((D_MODEL, tm), BF16), pltpu.VMEM((D_MODEL, tm), F32)],
        compiler_params=_cparams(("parallel", "arbitrary")),
        name="relu2_mlp",
    )(xT, g_b, wuT, wdT)


def _to_chunks(rows, B, T):
    half_tokens = CMP_LEN // 2
    t = rows.reshape(B, T, N_KV_HEADS, HEAD_DIM).transpose(0, 2, 1, 3)
    return t.reshape(B * N_KV_HEADS, T // half_tokens, half_tokens * HEAD_DIM)


def kernel(x, norm_mix, w_in, q_norm, k_norm, sinks, cmp_pe_k, cmp_pe_v, w_ck1, w_ck2, w_cv1, w_cv2,
           w_branch, w_out, norm_mlp, w_up, w_down):
    B, T, D = x.shape
    N = B * T
    depth = w_in.shape[0]
    nq = T // Q_BLOCK
    ncmp = T // CMP_STRIDE
    nsel = T // SLC_LEN

    inv_freq = ROPE_THETA ** (-jnp.arange(0, HEAD_DIM, 2, dtype=F32) / HEAD_DIM)
    ang = jnp.arange(T, dtype=F32)[:, None] * inv_freq[None, :]
    cosT, sinT = jnp.cos(ang).T, jnp.sin(ang).T

    ci = np.arange(KEY_CHUNK)
    ltri = jnp.asarray(ci[None, :] < ci[:, None], BF16)
    emat = jnp.asarray((np.arange(T)[:, None] // SLC_LEN) == np.arange(nsel)[None, :], BF16)
    cstart = np.arange(ncmp) * CMP_STRIDE
    sstart = np.arange(nsel) * SLC_LEN
    n_cmp_valid = (T - CMP_LEN) // CMP_STRIDE + 1
    ov = ((cstart[None, :] < sstart[:, None] + SLC_LEN) & (cstart[None, :] + CMP_LEN > sstart[:, None])
          & (np.arange(ncmp)[None, :] < n_cmp_valid))
    ovT = jnp.asarray(ov, BF16)

    xT = x.reshape(N, D).T

    sl = lambda s: w_in[:, :, s[0]:s[1]]
    wT_all = jnp.concatenate(
        [sl(S_QA), sl(S_QB), sl(S_QC),
         sl(S_KA), sl(S_KB), sl(S_KC), sl(S_KSL), sl(S_KWN),
         sl(S_VA), sl(S_VB), sl(S_VC), sl(S_VSL), sl(S_VWN),
         sl(S_IQ), sl(S_IK), sl(S_IW), jnp.zeros((depth, D, GATE_ROW0 - IDX_HEADS), w_in.dtype), sl(S_GC)],
        axis=2).transpose(0, 2, 1).astype(BF16)
    wgT_all = sl(S_GM).transpose(0, 2, 1).astype(BF16)
    wbT_all = w_branch.transpose(0, 1, 3, 2).astype(BF16)
    woT_all = w_out.transpose(0, 2, 1).astype(BF16)
    wuT_all = w_up.transpose(0, 2, 1).astype(BF16)
    wdT_all = w_down.transpose(0, 2, 1).astype(BF16)
    lane_b = lambda v: jnp.broadcast_to(v.astype(F32)[..., None], v.shape + (LANE,))
    gmix_all, gmlp_all = lane_b(norm_mix), lane_b(norm_mlp)
    gq_all = lane_b(jnp.tile(q_norm[:, :, None, :], (1, 1, N_Q_HEADS, 1)).reshape(depth, -1))
    gk_all = lane_b(jnp.tile(k_norm[:, jnp.array([0, 1, 2, 2, 2])][:, :, None, :],
                             (1, 1, N_KV_HEADS, 1)).reshape(depth, -1))
    sink_all = jnp.broadcast_to(jnp.repeat(sinks.astype(F32) * LOG2E, Q_BLOCK, axis=1)[:, None, :],
                                (depth, 8, N_Q_HEADS * Q_BLOCK))

    for l in range(depth):
        wgT, sink_lanes = wgT_all[l], sink_all[l]
        qT, k_rows, vT128, iqT, ik_rows, miscT = _project(
            xT, gmix_all[l], wT_all[l], gq_all[l], gk_all[l], cosT, sinT, T)

        o_a = _dsa(iqT, miscT, ik_rows, qT, k_rows, vT128, ltri, B, T)
        o_b = _band(qT, 1, k_rows, 1, vT128, sink_lanes, SWA_WINDOW, True, BF16, B, T, "swa_attention")
        o_w = _band(qT, 2, k_rows, 4, vT128, sink_lanes, NSA_WINDOW, False, F32, B, T, "nsa_window_attention")

        kc_chunks = _to_chunks(k_rows[:, 2 * KV_WIDTH:3 * KV_WIDTH].reshape(B, T, KV_WIDTH), B, T)
        vc_rows = vT128[:, 2 * KV_WIDTH:3 * KV_WIDTH, :].reshape(B, nq, KV_WIDTH, LANE)
        vc_rows = vc_rows.transpose(0, 1, 3, 2).reshape(B, T, KV_WIDTH)
        vc_chunks = _to_chunks(vc_rows, B, T)
        k_cmp = _compress(kc_chunks, cmp_pe_k[l], w_ck1[l], w_ck2[l], "compress_k")
        v_cmp = _compress(vc_chunks, cmp_pe_v[l], w_cv1[l], w_cv2[l], "compress_v")
        kcmp_rows = k_cmp.reshape(B, N_KV_HEADS, ncmp, HEAD_DIM).transpose(0, 2, 1, 3)
        kcmp_rows = kcmp_rows.reshape(B, ncmp, KV_WIDTH).astype(BF16)
        vcmpT = v_cmp.reshape(B, N_KV_HEADS, ncmp, HEAD_DIM).transpose(0, 1, 3, 2)
        vcmpT = vcmpT.reshape(B, KV_WIDTH, ncmp).astype(BF16)

        o_c = _nsa(qT, kcmp_rows, vcmpT, k_rows, vT128, miscT, o_w, ovT, emat, B, T)

        xT = _merge(xT, gmix_all[l], wgT, o_a, o_b, o_c, wbT_all[l], woT_all[l])
        xT = _mlp(xT, gmlp_all[l], wuT_all[l], wdT_all[l], token_major_out=(l == depth - 1))

    return xT.reshape(B, T, D)
```

```python
import functools

import numpy as np
import jax
import jax.numpy as jnp
from jax import lax
from jax.experimental import pallas as pl
from jax.experimental.pallas import tpu as pltpu

F32 = jnp.float32
BF16 = jnp.bfloat16

D_MODEL = 1024
HEAD_DIM = 64
HALF = HEAD_DIM // 2
N_Q_HEADS = 8
N_KV_HEADS = 2
GROUP = N_Q_HEADS // N_KV_HEADS
MIX_WIDTH = N_Q_HEADS * HEAD_DIM
KV_WIDTH = N_KV_HEADS * HEAD_DIM
N_MIXERS = 3
D_FF = 4 * D_MODEL
ROPE_THETA = 10000.0
EPS = 1e-6
NEG_INF = -1e30
MASK_VAL = -2e30
M_INIT = -1e30
BIG = 3e38
Q_BLOCK = 128
KEY_CHUNK = 512
BISECT_COUNT_STEPS = 14
BISECT_FIXED_STEPS = 2
ONES_ROWS = 16
IDX_HEADS = 4
IDX_DIM = 64
DSA_TOPK_MAX = 256
SWA_WINDOW = 128
CMP_LEN = 32
CMP_STRIDE = 16
CMP_HIDDEN = 256
SLC_LEN = 64
SLC_TOPN = 16
NSA_WINDOW = 512
FORCE_SCORE = 1e9
LOG2E = 1.4426950408889634
Q_SCALE = (HEAD_DIM ** -0.5) * LOG2E
LANE = 128
LANE_SHIFT = 7
KEY_STEPS = (2048, 1024, 512)
VMEM_LIMIT = 56 * 1024 * 1024

IN_SIZES = (
    MIX_WIDTH, KV_WIDTH, KV_WIDTH, IDX_HEADS * IDX_DIM, IDX_DIM, IDX_HEADS,
    MIX_WIDTH, KV_WIDTH, KV_WIDTH,
    MIX_WIDTH, KV_WIDTH, KV_WIDTH, KV_WIDTH, KV_WIDTH, KV_WIDTH, KV_WIDTH, 3 * N_Q_HEADS,
    N_MIXERS * D_MODEL,
)
_OFF = np.concatenate([[0], np.cumsum(IN_SIZES)]).astype(int)
(S_QA, S_KA, S_VA, S_IQ, S_IK, S_IW, S_QB, S_KB, S_VB,
 S_QC, S_KC, S_VC, S_KSL, S_VSL, S_KWN, S_VWN, S_GC, S_GM) = [
    (int(_OFF[i]), int(_OFF[i + 1])) for i in range(len(IN_SIZES))]

R_Q = 0
R_K = R_Q + N_MIXERS * MIX_WIDTH
R_V = R_K + 5 * KV_WIDTH
R_IQ = R_V + 5 * KV_WIDTH
R_IK = R_IQ + IDX_HEADS * IDX_DIM
R_MISC = R_IK + IDX_DIM
MISC_ROWS = 32
GATE_ROW0 = 8
R_END = R_MISC + MISC_ROWS
PROJ_TM = 1024


def _cparams(sem):
    return pltpu.CompilerParams(dimension_semantics=sem, vmem_limit_bytes=VMEM_LIMIT)


def _lane_tile(a, n):
    return a if n == 1 else jnp.concatenate([a] * n, axis=1)


def _proj_kernel(x_ref, g_ref, w_ref, gq_ref, gk_ref, cos_ref, sin_ref,
                 q_ref, k_ref, v_ref, iq_ref, ik_ref, misc_ref):
    tm = x_ref.shape[1]
    rep = tm // LANE
    x = x_ref[...]
    ss = jnp.sum(x * x, axis=0, keepdims=True)
    h = (x * lax.rsqrt(ss * (1.0 / D_MODEL) + EPS) * _lane_tile(g_ref[...], rep)).astype(BF16)
    cos = cos_ref[...]
    sin = sin_ref[...]

    def rope(z):
        z1, z2 = z[:HALF], z[HALF:]
        return jnp.concatenate([z1 * cos - z2 * sin, z2 * cos + z1 * sin], axis=0)

    def head_norm(z, gain):
        ssq = jnp.sum(z * z, axis=0, keepdims=True)
        return z * lax.rsqrt(ssq * (1.0 / HEAD_DIM) + EPS) * _lane_tile(gain, rep)

    for m in range(N_MIXERS):
        z = jnp.dot(w_ref[R_Q + m * MIX_WIDTH:R_Q + (m + 1) * MIX_WIDTH, :], h,
                    preferred_element_type=F32)
        for hh in range(N_Q_HEADS):
            r0 = m * MIX_WIDTH + hh * HEAD_DIM
            zh = head_norm(z[hh * HEAD_DIM:(hh + 1) * HEAD_DIM], gq_ref[r0:r0 + HEAD_DIM, :])
            q_ref[r0:r0 + HEAD_DIM, :] = (rope(zh) * Q_SCALE).astype(BF16)

    z = jnp.dot(w_ref[R_K:R_V, :], h, preferred_element_type=F32)
    for mk in range(5):
        heads = []
        for kvh in range(N_KV_HEADS):
            r0 = mk * KV_WIDTH + kvh * HEAD_DIM
            heads.append(rope(head_norm(z[r0:r0 + HEAD_DIM], gk_ref[r0:r0 + HEAD_DIM, :])))
        k_ref[:, mk * KV_WIDTH:(mk + 1) * KV_WIDTH] = jnp.concatenate(heads, axis=0).T.astype(BF16)

    z = jnp.dot(w_ref[R_V:R_IQ, :], h, preferred_element_type=F32)
    for j in range(rep):
        v_ref[j] = z[:, j * LANE:(j + 1) * LANE].astype(BF16)

    z = jnp.dot(w_ref[R_IQ:R_END, :], h, preferred_element_type=F32)
    for hh in range(IDX_HEADS):
        r0 = hh * IDX_DIM
        iq_ref[r0:r0 + IDX_DIM, :] = rope(z[r0:r0 + IDX_DIM])
    ik = rope(z[R_IK - R_IQ:R_MISC - R_IQ])
    ik_ref[...] = jnp.concatenate([ik, jnp.zeros((LANE - IDX_DIM, tm), F32)], axis=0).T
    misc_ref[...] = z[R_MISC - R_IQ:R_END - R_IQ]


def _project(xT, g_b, wT, gq_b, gk_b, cosT, sinT, T):
    N = xT.shape[1]
    tm = PROJ_TM
    tpb = T // tm
    rep = tm // LANE
    col = lambda i: (0, i)
    const = lambda i: (0, 0)
    return pl.pallas_call(
        _proj_kernel,
        grid=(N // tm,),
        in_specs=[
            pl.BlockSpec((D_MODEL, tm), col),
            pl.BlockSpec((D_MODEL, LANE), const),
            pl.BlockSpec((R_END, D_MODEL), const),
            pl.BlockSpec((N_MIXERS * MIX_WIDTH, LANE), const),
            pl.BlockSpec((5 * KV_WIDTH, LANE), const),
            pl.BlockSpec((HALF, tm), lambda i: (0, i % tpb)),
            pl.BlockSpec((HALF, tm), lambda i: (0, i % tpb)),
        ],
        out_specs=[
            pl.BlockSpec((N_MIXERS * MIX_WIDTH, tm), col),
            pl.BlockSpec((tm, 5 * KV_WIDTH), lambda i: (i, 0)),
            pl.BlockSpec((rep, 5 * KV_WIDTH, LANE), lambda i: (i, 0, 0)),
            pl.BlockSpec((IDX_HEADS * IDX_DIM, tm), col),
            pl.BlockSpec((tm, LANE), lambda i: (i, 0)),
            pl.BlockSpec((MISC_ROWS, tm), col),
        ],
        out_shape=[
            jax.ShapeDtypeStruct((N_MIXERS * MIX_WIDTH, N), BF16),
            jax.ShapeDtypeStruct((N, 5 * KV_WIDTH), BF16),
            jax.ShapeDtypeStruct((N // LANE, 5 * KV_WIDTH, LANE), BF16),
            jax.ShapeDtypeStruct((IDX_HEADS * IDX_DIM, N), F32),
            jax.ShapeDtypeStruct((N, LANE), F32),
            jax.ShapeDtypeStruct((MISC_ROWS, N), F32),
        ],
        compiler_params=_cparams(("parallel",)),
        name="in_projection",
    )(xT, g_b, wT, gq_b, gk_b, cosT, sinT)


def _padded_queries(q):
    zeros = jnp.zeros((HEAD_DIM, Q_BLOCK), q.dtype)
    out = []
    for kv in range(N_KV_HEADS):
        cols = []
        for g in range(GROUP):
            hq = kv * GROUP + g
            qh = q[hq * HEAD_DIM:(hq + 1) * HEAD_DIM, :]
            cols.append(jnp.concatenate([qh, zeros] if kv == 0 else [zeros, qh], axis=0))
        out.append(jnp.concatenate(cols, axis=1))
    return out


def _v_chunk(v_ref, blk0, nblk):
    return jnp.concatenate([v_ref[blk0 + j] for j in range(nblk)], axis=1)


def _walk_keys(nch, fn, carry):
    done = 0
    for size in KEY_STEPS:
        per = size // KEY_CHUNK
        n = (nch - done) // per
        base = done * KEY_CHUNK
        carry = lax.fori_loop(
            0, n, lambda c, x, base=base, size=size: fn(pl.multiple_of(base + c * size, KEY_CHUNK), size, x),
            carry)
        done = done + n * per
    return carry


def _masked_attention(nch, sel_fn, k_ref, v_ref, qall, s_ref, acc_ref):
    lanes = N_KV_HEADS * GROUP * Q_BLOCK
    walk = functools.partial(_walk_keys, nch)

    def qk_step(off, size, mx):
        s = jnp.dot(k_ref[pl.ds(off, size), :], qall, preferred_element_type=F32) + sel_fn(off, size)
        s_ref[pl.ds(off, size), :] = s
        return jnp.maximum(mx, jnp.max(s, axis=0, keepdims=True))

    mx = walk(qk_step, jnp.full((1, lanes), M_INIT, F32))
    acc_ref[...] = jnp.zeros(acc_ref.shape, F32)

    def pv_step(off, size, carry):
        p = jnp.exp2(s_ref[pl.ds(off, size), :] - mx).astype(BF16)
        vc = jnp.concatenate([_v_chunk(v_ref, lax.shift_right_logical(off, LANE_SHIFT), size // LANE),
                              jnp.ones((ONES_ROWS, size), BF16)], axis=0)
        acc_ref[...] += jnp.dot(vc, p, preferred_element_type=F32)
        return carry

    walk(pv_step, 0)
    l = acc_ref[KV_WIDTH:KV_WIDTH + 1, :]
    inv = jnp.where(l > 0.0, 1.0 / l, 0.0)
    half = GROUP * Q_BLOCK
    return [acc_ref[kv * HEAD_DIM:(kv + 1) * HEAD_DIM, kv * half:(kv + 1) * half]
            * inv[:, kv * half:(kv + 1) * half] for kv in range(N_KV_HEADS)]


def _dsa_kernel(iq_ref, misc_ref, ik_ref, q_ref, k_ref, v_ref, ltri_ref, o_ref,
                sc_ref, s_ref, acc_ref, *, k_top):
    C = KEY_CHUNK
    i = pl.program_id(1)
    nch = i // (C // Q_BLOCK) + 1
    qpos = i * Q_BLOCK + lax.broadcasted_iota(jnp.int32, (1, Q_BLOCK), 1)

    iq = iq_ref[...].astype(BF16)
    iq_cat = jnp.concatenate([iq[h * IDX_DIM:(h + 1) * IDX_DIM] for h in range(IDX_HEADS)], axis=1)
    w = misc_ref[0:IDX_HEADS, :] * ((IDX_HEADS ** -0.5) * (IDX_DIM ** -0.5))

    def fold(x):
        return x.reshape(x.shape[0] // 8, 8, Q_BLOCK)

    part = lambda v: jnp.full((8, Q_BLOCK), v, F32)

    def score_step(off, size, carry):
        lo, hi = carry
        ikc = ik_ref[pl.ds(off, size), 0:IDX_DIM].astype(BF16)
        d = jnp.dot(ikc, iq_cat, preferred_element_type=F32)
        s = jnp.maximum(d[:, 0:Q_BLOCK], 0.0) * w[0:1, :]
        for h in range(1, IDX_HEADS):
            s = s + jnp.maximum(d[:, h * Q_BLOCK:(h + 1) * Q_BLOCK], 0.0) * w[h:h + 1, :]
        causal = (off + lax.broadcasted_iota(jnp.int32, (size, Q_BLOCK), 0)) <= qpos
        sc_ref[pl.ds(off, size), :] = jnp.where(causal, s, NEG_INF)
        lo = jnp.minimum(lo, jnp.min(fold(jnp.where(causal, s, BIG)), axis=0))
        hi = jnp.maximum(hi, jnp.max(fold(jnp.where(causal, s, -BIG)), axis=0))
        return lo, hi

    lo0, hi0 = _walk_keys(nch, score_step, (part(BIG), part(-BIG)))
    lo0 = jnp.min(lo0, axis=0, keepdims=True)
    hi0 = jnp.max(hi0, axis=0, keepdims=True)

    kf = float(k_top)
    select = i * Q_BLOCK >= k_top

    def count_ge(mid):
        mid8 = jnp.broadcast_to(mid, (8, Q_BLOCK))[None]

        def body(c, cnts):
            cnts = list(cnts)
            for j in range(C // 64):
                off = pl.multiple_of(c * C + j * 64, 64)
                s = sc_ref[pl.ds(off, 64), :].reshape(8, 8, Q_BLOCK)
                cnts[j % 4] = cnts[j % 4] + jnp.sum(jnp.where(s >= mid8, 1.0, 0.0), axis=0)
            return tuple(cnts)

        cnts = lax.fori_loop(0, nch, body, (part(0.0),) * 4)
        return jnp.sum((cnts[0] + cnts[1]) + (cnts[2] + cnts[3]), axis=0, keepdims=True)

    def count_step(_, st):
        lo, hi, clo, chi, moved = st
        mid = lo + (hi - lo) * 0.5
        cnt = count_ge(mid)
        up = cnt >= kf
        return (jnp.where(up, mid, lo), jnp.where(up, hi, mid), jnp.where(up, cnt, clo),
                jnp.where(up, chi, cnt), jnp.where(up, moved, 1.0))

    def stats(mid):
        mid8 = jnp.broadcast_to(mid, (8, Q_BLOCK))[None]

        def body(c, carry):
            cnt, a, b = carry
            for j in range(C // 64):
                off = pl.multiple_of(c * C + j * 64, 64)
                s = sc_ref[pl.ds(off, 64), :].reshape(8, 8, Q_BLOCK)
                gt = s > mid8
                cnt = cnt + jnp.sum(jnp.where(gt, 1.0, 0.0), axis=0)
                a = jnp.minimum(a, jnp.min(jnp.where(gt, s, BIG), axis=0))
                b = jnp.maximum(b, jnp.max(jnp.where(gt, -BIG, s), axis=0))
            return cnt, a, b

        cnt, a, b = lax.fori_loop(0, nch, body, (part(0.0), part(BIG), part(-BIG)))
        return (jnp.sum(cnt, axis=0, keepdims=True), jnp.min(a, axis=0, keepdims=True),
                jnp.max(b, axis=0, keepdims=True))

    def step(st):
        lo, hi, clo, chi = st
        mid = jnp.maximum(lo + (hi - lo) * 0.5, lo)
        mid = jnp.where(mid >= hi, lo, mid)
        cnt, a, b = stats(mid)
        up = cnt >= kf
        return (jnp.where(up, a, lo), jnp.where(up, hi, b),
                jnp.where(up, cnt, clo), jnp.where(up, chi, cnt))

    def cond(st):
        return jnp.logical_and(select, jnp.max(st[1] - st[0]) > 0.0)

    n_causal = (qpos + 1).astype(F32)
    zero = jnp.zeros((1, Q_BLOCK), F32)
    lo_v, hi_v, clo, chi, moved = lax.fori_loop(
        0, jnp.where(select, BISECT_COUNT_STEPS, 0), count_step, (lo0, hi0, n_causal, zero, zero))

    def snap_chunk(c, carry):
        lo, hi = carry
        s = fold(sc_ref[pl.ds(pl.multiple_of(c * C, C), C), :])
        lo = jnp.minimum(lo, jnp.min(jnp.where(s >= lo_v[None], s, BIG), axis=0))
        hi = jnp.maximum(hi, jnp.max(jnp.where(s < hi_v[None], s, -BIG), axis=0))
        return lo, hi

    lo1, hi1 = lax.fori_loop(0, jnp.where(select, nch, 0), snap_chunk, (part(BIG), part(-BIG)))
    lo1 = jnp.min(lo1, axis=0, keepdims=True)
    hi1 = jnp.where(moved > 0.0, jnp.max(hi1, axis=0, keepdims=True), hi0)
    st = lax.fori_loop(0, jnp.where(select, BISECT_FIXED_STEPS, 0), lambda _, s: step(s),
                       (lo1, hi1, clo, chi))
    lo, _, clo, chi = lax.while_loop(cond, lambda s: step(step(s)), st)
    thr = jnp.where(select, lo, 0.5 * NEG_INF)
    need = jnp.where(select, kf - chi, 0.0)
    ties = jnp.logical_and(select, jnp.max((clo - chi) - need) > 0.0)

    @pl.when(ties)
    def _():
        def mask_chunk(c, tie_carry):
            off = pl.multiple_of(c * C, C)
            s_idx = sc_ref[pl.ds(off, C), :]
            eqf = jnp.where(s_idx == thr, 1.0, 0.0)
            before = jnp.dot(ltri_ref[...], eqf.astype(BF16), preferred_element_type=F32) + tie_carry
            sc_ref[pl.ds(off, C), :] = jnp.where(s_idx > thr, 1.0, jnp.where(before < need, eqf, 0.0))
            return tie_carry + jnp.sum(eqf, axis=0, keepdims=True)

        lax.fori_loop(0, nch, mask_chunk, jnp.zeros((1, Q_BLOCK), F32))

    thr_sel = jnp.where(ties, 0.5, thr)
    qall = jnp.concatenate(_padded_queries(q_ref[...]), axis=1)

    def sel_fn(off, size):
        bias = jnp.where(sc_ref[pl.ds(off, size), :] >= thr_sel, 0.0, MASK_VAL)
        return _lane_tile(bias, N_KV_HEADS * GROUP)

    outs = _masked_attention(nch, sel_fn, k_ref, v_ref, qall, s_ref, acc_ref)
    for kv in range(N_KV_HEADS):
        o = outs[kv]
        for g in range(GROUP):
            hq = kv * GROUP + g
            o_ref[hq * HEAD_DIM:(hq + 1) * HEAD_DIM, :] = o[:, g * Q_BLOCK:(g + 1) * Q_BLOCK].astype(BF16)


def _dsa(iqT, miscT, ik_rows, qT, k_rows, vT128, ltri, B, T):
    nq = T // Q_BLOCK
    k_top = min(DSA_TOPK_MAX, T // 4)
    N = B * T
    return pl.pallas_call(
        functools.partial(_dsa_kernel, k_top=k_top),
        grid=(B, nq),
        in_specs=[
            pl.BlockSpec((IDX_HEADS * IDX_DIM, Q_BLOCK), lambda b, i: (0, b * nq + i)),
            pl.BlockSpec((MISC_ROWS, Q_BLOCK), lambda b, i: (0, b * nq + i)),
            pl.BlockSpec((T, LANE), lambda b, i: (b, 0)),
            pl.BlockSpec((MIX_WIDTH, Q_BLOCK), lambda b, i: (0, b * nq + i)),
            pl.BlockSpec((T, KV_WIDTH), lambda b, i: (b, 0)),
            pl.BlockSpec((nq, KV_WIDTH, LANE), lambda b, i: (b, 0, 0)),
            pl.BlockSpec((KEY_CHUNK, KEY_CHUNK), lambda b, i: (0, 0)),
        ],
        out_specs=pl.BlockSpec((MIX_WIDTH, Q_BLOCK), lambda b, i: (0, b * nq + i)),
        out_shape=jax.ShapeDtypeStruct((MIX_WIDTH, N), BF16),
        scratch_shapes=[
            pltpu.VMEM((T, Q_BLOCK), F32),
            pltpu.VMEM((T, N_KV_HEADS * GROUP * Q_BLOCK), F32),
            pltpu.VMEM((KV_WIDTH + ONES_ROWS, N_KV_HEADS * GROUP * Q_BLOCK), F32),
        ],
        compiler_params=_cparams(("parallel", "arbitrary")),
        name="dsa_attention",
    )(iqT, miscT, ik_rows, qT, k_rows, vT128, ltri)


def _band_kernel(q_ref, k_ref, v_ref, sink_ref, o_ref, *, n_prev, window, use_sink):
    i = pl.program_id(1)
    blk0 = jnp.maximum(i - n_prev, 0)
    nblk = n_prev + 1
    wb = nblk * Q_BLOCK
    kb = k_ref[pl.ds(pl.multiple_of(blk0 * Q_BLOCK, Q_BLOCK), wb), :]
    vb = jnp.concatenate([_v_chunk(v_ref, blk0, nblk), jnp.ones((ONES_ROWS, wb), BF16)], axis=0)
    qall = jnp.concatenate(_padded_queries(q_ref[...]), axis=1)
    rel = ((i - blk0) * Q_BLOCK + lax.broadcasted_iota(jnp.int32, (wb, Q_BLOCK), 1)
           - lax.broadcasted_iota(jnp.int32, (wb, Q_BLOCK), 0))
    bias = jnp.where((rel >= 0) & (rel < window), 0.0, MASK_VAL)
    s = jnp.dot(kb, qall, preferred_element_type=F32) + _lane_tile(bias, N_KV_HEADS * GROUP)
    m = jnp.max(s, axis=0, keepdims=True)
    if use_sink:
        sink = sink_ref[0:1, :]
        m = jnp.maximum(m, sink)
    o = jnp.dot(vb, jnp.exp2(s - m).astype(BF16), preferred_element_type=F32)
    l = o[KV_WIDTH:KV_WIDTH + 1, :]
    if use_sink:
        l = l + jnp.exp2(sink - m)
    inv = 1.0 / l
    half = GROUP * Q_BLOCK
    for kv in range(N_KV_HEADS):
        for g in range(GROUP):
            hq = kv * GROUP + g
            lanes = slice(kv * half + g * Q_BLOCK, kv * half + (g + 1) * Q_BLOCK)
            o_ref[hq * HEAD_DIM:(hq + 1) * HEAD_DIM, :] = (
                o[kv * HEAD_DIM:(kv + 1) * HEAD_DIM, lanes] * inv[:, lanes]).astype(o_ref.dtype)


def _band(qT, q_mixer, k_rows, kv_index, vT128, sink_lanes, window, use_sink, out_dtype, B, T, name):
    nq = T // Q_BLOCK
    N = B * T
    n_prev = (window + Q_BLOCK - 2) // Q_BLOCK
    return pl.pallas_call(
        functools.partial(_band_kernel, n_prev=n_prev, window=window, use_sink=use_sink),
        grid=(B, nq),
        in_specs=[
            pl.BlockSpec((MIX_WIDTH, Q_BLOCK), lambda b, i: (q_mixer, b * nq + i)),
            pl.BlockSpec((T, KV_WIDTH), lambda b, i: (b, kv_index)),
            pl.BlockSpec((nq, KV_WIDTH, LANE), lambda b, i: (b, kv_index, 0)),
            pl.BlockSpec((8, N_Q_HEADS * Q_BLOCK), lambda b, i: (0, 0)),
        ],
        out_specs=pl.BlockSpec((MIX_WIDTH, Q_BLOCK), lambda b, i: (0, b * nq + i)),
        out_shape=jax.ShapeDtypeStruct((MIX_WIDTH, N), out_dtype),
        compiler_params=_cparams(("parallel", "arbitrary")),
        name=name,
    )(qT, k_rows, vT128, sink_lanes)


def _cmp_kernel(xa_ref, xb_ref, pe_ref, w1_ref, w2_ref, o_ref):
    half = (CMP_LEN // 2) * HEAD_DIM
    xa = (xa_ref[...].astype(F32) + pe_ref[0:1, :]).astype(BF16)
    xb = (xb_ref[...].astype(F32) + pe_ref[1:2, :]).astype(BF16)
    hid = (jnp.dot(xa, w1_ref[0:half, :], preferred_element_type=F32)
           + jnp.dot(xb, w1_ref[half:2 * half, :], preferred_element_type=F32))
    hid = jnp.maximum(hid, 0.0).astype(BF16)
    o_ref[...] = jnp.dot(hid, w2_ref[...], preferred_element_type=F32)


def _compress(chunks, pe, w1, w2, name):
    G, nc, half = chunks.shape
    nxt = jnp.concatenate([chunks[:, 1:], jnp.zeros((G, 1, half), chunks.dtype)], axis=1)
    pe2 = jnp.zeros((8, half), F32).at[0:2].set(pe.reshape(2, half))
    return pl.pallas_call(
        _cmp_kernel,
        grid=(G,),
        in_specs=[
            pl.BlockSpec((None, nc, half), lambda g: (g, 0, 0)),
            pl.BlockSpec((None, nc, half), lambda g: (g, 0, 0)),
            pl.BlockSpec((8, half), lambda g: (0, 0)),
            pl.BlockSpec((2 * half, CMP_HIDDEN), lambda g: (0, 0)),
            pl.BlockSpec((CMP_HIDDEN, HEAD_DIM), lambda g: (0, 0)),
        ],
        out_specs=pl.BlockSpec((None, nc, HEAD_DIM), lambda g: (g, 0, 0)),
        out_shape=jax.ShapeDtypeStruct((G, nc, HEAD_DIM), F32),
        compiler_params=_cparams(("parallel",)),
        name=name,
    )(chunks, nxt, pe2, w1.astype(BF16), w2.astype(BF16))


def _nsa_kernel(q_ref, kc_ref, vc_ref, ks_ref, vs_ref, misc_ref, ow_ref, ov_ref, e_ref, o_ref,
                s_ref, acc_ref, *, n_top):
    C = KEY_CHUNK
    i = pl.program_id(1)
    nch = i // (C // Q_BLOCK) + 1
    ncmp = kc_ref.shape[0]
    nsel = ov_ref.shape[0]
    lanes = GROUP * Q_BLOCK
    t = i * Q_BLOCK + lax.broadcasted_iota(jnp.int32, (1, Q_BLOCK), 1)
    t4 = i * Q_BLOCK + (lax.broadcasted_iota(jnp.int32, (1, lanes), 1) & (Q_BLOCK - 1))
    crow = lax.broadcasted_iota(jnp.int32, (ncmp, lanes), 0)
    jrow = lax.broadcasted_iota(jnp.int32, (nsel, Q_BLOCK), 0)
    cur = t // SLC_LEN
    qpad = _padded_queries(q_ref[...])
    kc = kc_ref[...]
    vcm = vc_ref[...]
    ov = ov_ref[...]

    o_cmps, selbs = [], []
    for kv in range(N_KV_HEADS):
        s = jnp.dot(kc, qpad[kv], preferred_element_type=F32)
        s = jnp.where(crow * CMP_STRIDE + (CMP_LEN - 1) <= t4, s, MASK_VAL)
        m = jnp.maximum(jnp.max(s, axis=0, keepdims=True), M_INIT)
        p = jnp.exp2(s - m)
        l = jnp.sum(p, axis=0, keepdims=True)
        p = p * jnp.where(l > 0.0, 1.0 / l, 0.0)
        o_cmp = jnp.dot(vcm, p.astype(BF16), preferred_element_type=F32)[kv * HEAD_DIM:(kv + 1) * HEAD_DIM, :]

        pg = p[:, 0:Q_BLOCK]
        for g in range(1, GROUP):
            pg = pg + p[:, g * Q_BLOCK:(g + 1) * Q_BLOCK]
        p_hi = pg.astype(BF16)
        p_lo = (pg - p_hi.astype(F32)).astype(BF16)
        imp = (jnp.dot(ov, p_hi, preferred_element_type=F32)
               + jnp.dot(ov, p_lo, preferred_element_type=F32))
        forced = (jrow == 0) | (jrow == cur) | (jrow == cur - 1)
        imp = jnp.where(forced, FORCE_SCORE, imp)
        imp = jnp.where(jrow * SLC_LEN <= t, imp, NEG_INF)
        tiles = [imp[r0:r0 + 8] for r0 in range(0, nsel, 8)]
        ranks = [jnp.zeros((8, Q_BLOCK), F32) for _ in tiles]
        sub = lax.broadcasted_iota(jnp.int32, (8, Q_BLOCK), 0)
        for ii in range(nsel):
            r = jnp.broadcast_to(imp[ii:ii + 1], (8, Q_BLOCK))
            for jt, tile in enumerate(tiles):
                ge = jnp.where(r >= tile, 1.0, 0.0)
                gt = jnp.where(r > tile, 1.0, 0.0)
                if jt > ii // 8:
                    ahead = ge
                elif jt < ii // 8:
                    ahead = gt
                else:
                    ahead = jnp.where(sub > ii % 8, ge, gt)
                ranks[jt] = ranks[jt] + ahead
        rank = jnp.concatenate(ranks, axis=0)
        selbs.append(jnp.where(rank < float(n_top), 1.0, 0.0).astype(BF16))
        o_cmps.append(o_cmp)

    qall = jnp.concatenate(qpad, axis=1)
    selb_all = jnp.concatenate(selbs, axis=1)

    def sel_fn(off, size):
        mk = jnp.dot(e_ref[pl.ds(off, size), :], selb_all, preferred_element_type=F32)
        causal = (off + lax.broadcasted_iota(jnp.int32, (size, Q_BLOCK), 0)) <= t
        parts = []
        for kv in range(N_KV_HEADS):
            keep = jnp.where(causal, mk[:, kv * Q_BLOCK:(kv + 1) * Q_BLOCK], 0.0) > 0.5
            parts.append(_lane_tile(jnp.where(keep, 0.0, MASK_VAL), GROUP))
        return jnp.concatenate(parts, axis=1)

    o_slcs = _masked_attention(nch, sel_fn, ks_ref, vs_ref, qall, s_ref, acc_ref)

    for kv in range(N_KV_HEADS):
        for g in range(GROUP):
            hq = kv * GROUP + g
            gr = GATE_ROW0 + hq * 3
            gate = [1.0 / (1.0 + jnp.exp(-misc_ref[gr + br:gr + br + 1, :])) for br in range(3)]
            out = (gate[0] * o_cmps[kv][:, g * Q_BLOCK:(g + 1) * Q_BLOCK]
                   + gate[1] * o_slcs[kv][:, g * Q_BLOCK:(g + 1) * Q_BLOCK]
                   + gate[2] * ow_ref[hq * HEAD_DIM:(hq + 1) * HEAD_DIM, :])
            o_ref[hq * HEAD_DIM:(hq + 1) * HEAD_DIM, :] = out.astype(BF16)


def _nsa(qT, kcmp_rows, vcmpT, k_rows, vT128, miscT, o_winT, ovT, emat, B, T):
    nq = T // Q_BLOCK
    N = B * T
    ncmp = T // CMP_STRIDE
    nsel = T // SLC_LEN
    n_top = min(SLC_TOPN, nsel)
    return pl.pallas_call(
        functools.partial(_nsa_kernel, n_top=n_top),
        grid=(B, nq),
        in_specs=[
            pl.BlockSpec((MIX_WIDTH, Q_BLOCK), lambda b, i: (2, b * nq + i)),
            pl.BlockSpec((None, ncmp, KV_WIDTH), lambda b, i: (b, 0, 0)),
            pl.BlockSpec((None, KV_WIDTH, ncmp), lambda b, i: (b, 0, 0)),
            pl.BlockSpec((T, KV_WIDTH), lambda b, i: (b, 3)),
            pl.BlockSpec((nq, KV_WIDTH, LANE), lambda b, i: (b, 3, 0)),
            pl.BlockSpec((MISC_ROWS, Q_BLOCK), lambda b, i: (0, b * nq + i)),
            pl.BlockSpec((MIX_WIDTH, Q_BLOCK), lambda b, i: (0, b * nq + i)),
            pl.BlockSpec((nsel, ncmp), lambda b, i: (0, 0)),
            pl.BlockSpec((T, nsel), lambda b, i: (0, 0)),
        ],
        out_specs=pl.BlockSpec((MIX_WIDTH, Q_BLOCK), lambda b, i: (0, b * nq + i)),
        out_shape=jax.ShapeDtypeStruct((MIX_WIDTH, N), BF16),
        scratch_shapes=[
            pltpu.VMEM((T, N_KV_HEADS * GROUP * Q_BLOCK), F32),
            pltpu.VMEM((KV_WIDTH + ONES_ROWS, N_KV_HEADS * GROUP * Q_BLOCK), F32),
        ],
        compiler_params=_cparams(("parallel", "arbitrary")),
        name="nsa_attention",
    )(qT, kcmp_rows, vcmpT, k_rows, vT128, miscT, o_winT, ovT, emat)


def _merge_kernel(x_ref, g_ref, wg_ref, oa_ref, ob_ref, oc_ref, wb_ref, wo_ref, out_ref):
    tm = x_ref.shape[1]
    x = x_ref[...]
    ss = jnp.sum(x * x, axis=0, keepdims=True)
    h = (x * lax.rsqrt(ss * (1.0 / D_MODEL) + EPS) * _lane_tile(g_ref[...], tm // LANE)).astype(BF16)
    acc = None
    for n, o_ref in enumerate((oa_ref, ob_ref, oc_ref)):
        gm = jnp.dot(wg_ref[n * D_MODEL:(n + 1) * D_MODEL, :], h, preferred_element_type=F32)
        y = jnp.dot(wb_ref[n], o_ref[...], preferred_element_type=F32)
        term = y * (1.0 / (1.0 + jnp.exp(-gm)))
        acc = term if acc is None else acc + term
    out_ref[...] = x + jnp.dot(wo_ref[...], acc.astype(BF16), preferred_element_type=F32)


def _merge(xT, g_b, wgT, oa, ob, oc, wbT, woT):
    N = xT.shape[1]
    tm = 512
    col = lambda i: (0, i)
    return pl.pallas_call(
        _merge_kernel,
        grid=(N // tm,),
        in_specs=[
            pl.BlockSpec((D_MODEL, tm), col),
            pl.BlockSpec((D_MODEL, LANE), lambda i: (0, 0)),
            pl.BlockSpec((N_MIXERS * D_MODEL, D_MODEL), lambda i: (0, 0)),
            pl.BlockSpec((MIX_WIDTH, tm), col),
            pl.BlockSpec((MIX_WIDTH, tm), col),
            pl.BlockSpec((MIX_WIDTH, tm), col),
            pl.BlockSpec((N_MIXERS, D_MODEL, MIX_WIDTH), lambda i: (0, 0, 0)),
            pl.BlockSpec((D_MODEL, D_MODEL), lambda i: (0, 0)),
        ],
        out_specs=pl.BlockSpec((D_MODEL, tm), col),
        out_shape=jax.ShapeDtypeStruct((D_MODEL, N), F32),
        compiler_params=_cparams(("parallel",)),
        name="merge_out_projection",
    )(xT, g_b, wgT, oa, ob, oc, wbT, woT)


def _mlp_kernel(x_ref, g_ref, wu_ref, wd_ref, out_ref, h_ref, acc_ref, *, token_major_out):
    f = pl.program_id(1)
    tm = x_ref.shape[1]

    @pl.when(f == 0)
    def _():
        x = x_ref[...]
        ss = jnp.sum(x * x, axis=0, keepdims=True)
        h_ref[...] = (x * lax.rsqrt(ss * (1.0 / D_MODEL) + EPS)
                      * _lane_tile(g_ref[...], tm // LANE)).astype(BF16)
        acc_ref[...] = jnp.zeros(acc_ref.shape, F32)

    u = jnp.maximum(jnp.dot(wu_ref[...], h_ref[...], preferred_element_type=F32), 0.0)
    acc_ref[...] += jnp.dot(wd_ref[...], (u * u).astype(BF16), preferred_element_type=F32)

    @pl.when(f == pl.num_programs(1) - 1)
    def _():
        y = x_ref[...] + acc_ref[...]
        out_ref[...] = y.T if token_major_out else y


def _mlp(xT, g_b, wuT, wdT, token_major_out):
    N = xT.shape[1]
    tm, tf = 1024, 1024
    if token_major_out:
        out_spec = pl.BlockSpec((tm, D_MODEL), lambda i, f: (i, 0))
        out_shape = jax.ShapeDtypeStruct((N, D_MODEL), F32)
    else:
        out_spec = pl.BlockSpec((D_MODEL, tm), lambda i, f: (0, i))
        out_shape = jax.ShapeDtypeStruct((D_MODEL, N), F32)
    return pl.pallas_call(
        functools.partial(_mlp_kernel, token_major_out=token_major_out),
        grid=(N // tm, D_FF // tf),
        in_specs=[
            pl.BlockSpec((D_MODEL, tm), lambda i, f: (0, i)),
            pl.BlockSpec((D_MODEL, LANE), lambda i, f: (0, 0)),
            pl.BlockSpec((tf, D_MODEL), lambda i, f: (f, 0)),
            pl.BlockSpec((D_MODEL, tf), lambda i, f: (0, f)),
        ],
        out_specs=out_spec,
        out_shape=out_shape,
        scratch_shapes=[pltpu.VMEM((D_MODEL, tm), BF16), pltpu.VMEM((D_MODEL, tm), F32)],
        compiler_params=_cparams(("parallel", "arbitrary")),
        name="relu2_mlp",
    )(xT, g_b, wuT, wdT)


def _to_chunks(rows, B, T):
    half_tokens = CMP_LEN // 2
    t = rows.reshape(B, T, N_KV_HEADS, HEAD_DIM).transpose(0, 2, 1, 3)
    return t.reshape(B * N_KV_HEADS, T // half_tokens, half_tokens * HEAD_DIM)


def kernel(x, norm_mix, w_in, q_norm, k_norm, sinks, cmp_pe_k, cmp_pe_v, w_ck1, w_ck2, w_cv1, w_cv2,
           w_branch, w_out, norm_mlp, w_up, w_down):
    B, T, D = x.shape
    N = B * T
    depth = w_in.shape[0]
    nq = T // Q_BLOCK
    ncmp = T // CMP_STRIDE
    nsel = T // SLC_LEN

    inv_freq = ROPE_THETA ** (-jnp.arange(0, HEAD_DIM, 2, dtype=F32) / HEAD_DIM)
    ang = jnp.arange(T, dtype=F32)[:, None] * inv_freq[None, :]
    cosT, sinT = jnp.cos(ang).T, jnp.sin(ang).T

    ci = np.arange(KEY_CHUNK)
    ltri = jnp.asarray(ci[None, :] < ci[:, None], BF16)
    emat = jnp.asarray((np.arange(T)[:, None] // SLC_LEN) == np.arange(nsel)[None, :], BF16)
    cstart = np.arange(ncmp) * CMP_STRIDE
    sstart = np.arange(nsel) * SLC_LEN
    n_cmp_valid = (T - CMP_LEN) // CMP_STRIDE + 1
    ov = ((cstart[None, :] < sstart[:, None] + SLC_LEN) & (cstart[None, :] + CMP_LEN > sstart[:, None])
          & (np.arange(ncmp)[None, :] < n_cmp_valid))
    ovT = jnp.asarray(ov, BF16)

    xT = x.reshape(N, D).T

    sl = lambda s: w_in[:, :, s[0]:s[1]]
    wT_all = jnp.concatenate(
        [sl(S_QA), sl(S_QB), sl(S_QC),
         sl(S_KA), sl(S_KB), sl(S_KC), sl(S_KSL), sl(S_KWN),
         sl(S_VA), sl(S_VB), sl(S_VC), sl(S_VSL), sl(S_VWN),
         sl(S_IQ), sl(S_IK), sl(S_IW), jnp.zeros((depth, D, GATE_ROW0 - IDX_HEADS), w_in.dtype), sl(S_GC)],
        axis=2).transpose(0, 2, 1).astype(BF16)
    wgT_all = sl(S_GM).transpose(0, 2, 1).astype(BF16)
    wbT_all = w_branch.transpose(0, 1, 3, 2).astype(BF16)
    woT_all = w_out.transpose(0, 2, 1).astype(BF16)
    wuT_all = w_up.transpose(0, 2, 1).astype(BF16)
    wdT_all = w_down.transpose(0, 2, 1).astype(BF16)
    lane_b = lambda v: jnp.broadcast_to(v.astype(F32)[..., None], v.shape + (LANE,))
    gmix_all, gmlp_all = lane_b(norm_mix), lane_b(norm_mlp)
    gq_all = lane_b(jnp.tile(q_norm[:, :, None, :], (1, 1, N_Q_HEADS, 1)).reshape(depth, -1))
    gk_all = lane_b(jnp.tile(k_norm[:, jnp.array([0, 1, 2, 2, 2])][:, :, None, :],
                             (1, 1, N_KV_HEADS, 1)).reshape(depth, -1))
    sink_all = jnp.broadcast_to(jnp.repeat(sinks.astype(F32) * LOG2E, Q_BLOCK, axis=1)[:, None, :],
                                (depth, 8, N_Q_HEADS * Q_BLOCK))

    for l in range(depth):
        wgT, sink_lanes = wgT_all[l], sink_all[l]
        qT, k_rows, vT128, iqT, ik_rows, miscT = _project(
            xT, gmix_all[l], wT_all[l], gq_all[l], gk_all[l], cosT, sinT, T)

        o_a = _dsa(iqT, miscT, ik_rows, qT, k_rows, vT128, ltri, B, T)
        o_b = _band(qT, 1, k_rows, 1, vT128, sink_lanes, SWA_WINDOW, True, BF16, B, T, "swa_attention")
        o_w = _band(qT, 2, k_rows, 4, vT128, sink_lanes, NSA_WINDOW, False, F32, B, T, "nsa_window_attention")

        kc_chunks = _to_chunks(k_rows[:, 2 * KV_WIDTH:3 * KV_WIDTH].reshape(B, T, KV_WIDTH), B, T)
        vc_rows = vT128[:, 2 * KV_WIDTH:3 * KV_WIDTH, :].reshape(B, nq, KV_WIDTH, LANE)
        vc_rows = vc_rows.transpose(0, 1, 3, 2).reshape(B, T, KV_WIDTH)
        vc_chunks = _to_chunks(vc_rows, B, T)
        k_cmp = _compress(kc_chunks, cmp_pe_k[l], w_ck1[l], w_ck2[l], "compress_k")
        v_cmp = _compress(vc_chunks, cmp_pe_v[l], w_cv1[l], w_cv2[l], "compress_v")
        kcmp_rows = k_cmp.reshape(B, N_KV_HEADS, ncmp, HEAD_DIM).transpose(0, 2, 1, 3)
        kcmp_rows = kcmp_rows.reshape(B, ncmp, KV_WIDTH).astype(BF16)
        vcmpT = v_cmp.reshape(B, N_KV_HEADS, ncmp, HEAD_DIM).transpose(0, 1, 3, 2)
        vcmpT = vcmpT.reshape(B, KV_WIDTH, ncmp).astype(BF16)

        o_c = _nsa(qT, kcmp_rows, vcmpT, k_rows, vT128, miscT, o_w, ovT, emat, B, T)

        xT = _merge(xT, gmix_all[l], wgT, o_a, o_b, o_c, wbT_all[l], woT_all[l])
        xT = _mlp(xT, gmlp_all[l], wuT_all[l], wdT_all[l], token_major_out=(l == depth - 1))

    return xT.reshape(B, T, D)
```

```python
import functools

import numpy as np
import jax
import jax.numpy as jnp
from jax import lax
from jax.experimental import pallas as pl
from jax.experimental.pallas import tpu as pltpu

F32 = jnp.float32
BF16 = jnp.bfloat16

D_MODEL = 1024
HEAD_DIM = 64
HALF = HEAD_DIM // 2
N_Q_HEADS = 8
N_KV_HEADS = 2
GROUP = N_Q_HEADS // N_KV_HEADS
MIX_WIDTH = N_Q_HEADS * HEAD_DIM
KV_WIDTH = N_KV_HEADS * HEAD_DIM
N_MIXERS = 3
D_FF = 4 * D_MODEL
ROPE_THETA = 10000.0
EPS = 1e-6
NEG_INF = -1e30
MASK_VAL = -2e30
M_INIT = -1e30
BIG = 3e38
Q_BLOCK = 128
KEY_CHUNK = 512
BISECT_COUNT_STEPS = 14
BISECT_FIXED_STEPS = 2
ONES_ROWS = 16
IDX_HEADS = 4
IDX_DIM = 64
DSA_TOPK_MAX = 256
SWA_WINDOW = 128
CMP_LEN = 32
CMP_STRIDE = 16
CMP_HIDDEN = 256
SLC_LEN = 64
SLC_TOPN = 16
NSA_WINDOW = 512
FORCE_SCORE = 1e9
LOG2E = 1.4426950408889634
Q_SCALE = (HEAD_DIM ** -0.5) * LOG2E
LANE = 128
LANE_SHIFT = 7
KEY_STEPS = (2048, 1024, 512)
VMEM_LIMIT = 56 * 1024 * 1024

IN_SIZES = (
    MIX_WIDTH, KV_WIDTH, KV_WIDTH, IDX_HEADS * IDX_DIM, IDX_DIM, IDX_HEADS,
    MIX_WIDTH, KV_WIDTH, KV_WIDTH,
    MIX_WIDTH, KV_WIDTH, KV_WIDTH, KV_WIDTH, KV_WIDTH, KV_WIDTH, KV_WIDTH, 3 * N_Q_HEADS,
    N_MIXERS * D_MODEL,
)
_OFF = np.concatenate([[0], np.cumsum(IN_SIZES)]).astype(int)
(S_QA, S_KA, S_VA, S_IQ, S_IK, S_IW, S_QB, S_KB, S_VB,
 S_QC, S_KC, S_VC, S_KSL, S_VSL, S_KWN, S_VWN, S_GC, S_GM) = [
    (int(_OFF[i]), int(_OFF[i + 1])) for i in range(len(IN_SIZES))]

R_Q = 0
R_K = R_Q + N_MIXERS * MIX_WIDTH
R_V = R_K + 5 * KV_WIDTH
R_IQ = R_V + 5 * KV_WIDTH
R_IK = R_IQ + IDX_HEADS * IDX_DIM
R_MISC = R_IK + IDX_DIM
MISC_ROWS = 32
GATE_ROW0 = 8
R_END = R_MISC + MISC_ROWS
PROJ_TM = 1024


def _cparams(sem):
    return pltpu.CompilerParams(dimension_semantics=sem, vmem_limit_bytes=VMEM_LIMIT)


def _lane_tile(a, n):
    return a if n == 1 else jnp.concatenate([a] * n, axis=1)


def _proj_kernel(x_ref, g_ref, w_ref, gq_ref, gk_ref, cos_ref, sin_ref,
                 q_ref, k_ref, v_ref, iq_ref, ik_ref, misc_ref):
    tm = x_ref.shape[1]
    rep = tm // LANE
    x = x_ref[...]
    ss = jnp.sum(x * x, axis=0, keepdims=True)
    h = (x * lax.rsqrt(ss * (1.0 / D_MODEL) + EPS) * _lane_tile(g_ref[...], rep)).astype(BF16)
    cos = cos_ref[...]
    sin = sin_ref[...]

    def rope(z):
        z1, z2 = z[:HALF], z[HALF:]
        return jnp.concatenate([z1 * cos - z2 * sin, z2 * cos + z1 * sin], axis=0)

    def head_norm(z, gain):
        ssq = jnp.sum(z * z, axis=0, keepdims=True)
        return z * lax.rsqrt(ssq * (1.0 / HEAD_DIM) + EPS) * _lane_tile(gain, rep)

    for m in range(N_MIXERS):
        z = jnp.dot(w_ref[R_Q + m * MIX_WIDTH:R_Q + (m + 1) * MIX_WIDTH, :], h,
                    preferred_element_type=F32)
        for hh in range(N_Q_HEADS):
            r0 = m * MIX_WIDTH + hh * HEAD_DIM
            zh = head_norm(z[hh * HEAD_DIM:(hh + 1) * HEAD_DIM], gq_ref[r0:r0 + HEAD_DIM, :])
            q_ref[r0:r0 + HEAD_DIM, :] = (rope(zh) * Q_SCALE).astype(BF16)

    z = jnp.dot(w_ref[R_K:R_V, :], h, preferred_element_type=F32)
    for mk in range(5):
        heads = []
        for kvh in range(N_KV_HEADS):
            r0 = mk * KV_WIDTH + kvh * HEAD_DIM
            heads.append(rope(head_norm(z[r0:r0 + HEAD_DIM], gk_ref[r0:r0 + HEAD_DIM, :])))
        k_ref[:, mk * KV_WIDTH:(mk + 1) * KV_WIDTH] = jnp.concatenate(heads, axis=0).T.astype(BF16)

    z = jnp.dot(w_ref[R_V:R_IQ, :], h, preferred_element_type=F32)
    for j in range(rep):
        v_ref[j] = z[:, j * LANE:(j + 1) * LANE].astype(BF16)

    z = jnp.dot(w_ref[R_IQ:R_END, :], h, preferred_element_type=F32)
    for hh in range(IDX_HEADS):
        r0 = hh * IDX_DIM
        iq_ref[r0:r0 + IDX_DIM, :] = rope(z[r0:r0 + IDX_DIM])
    ik = rope(z[R_IK - R_IQ:R_MISC - R_IQ])
    ik_ref[...] = jnp.concatenate([ik, jnp.zeros((LANE - IDX_DIM, tm), F32)], axis=0).T
    misc_ref[...] = z[R_MISC - R_IQ:R_END - R_IQ]


def _project(xT, g_b, wT, gq_b, gk_b, cosT, sinT, T):
    N = xT.shape[1]
    tm = PROJ_TM
    tpb = T // tm
    rep = tm // LANE
    col = lambda i: (0, i)
    const = lambda i: (0, 0)
    return pl.pallas_call(
        _proj_kernel,
        grid=(N // tm,),
        in_specs=[
            pl.BlockSpec((D_MODEL, tm), col),
            pl.BlockSpec((D_MODEL, LANE), const),
            pl.BlockSpec((R_END, D_MODEL), const),
            pl.BlockSpec((N_MIXERS * MIX_WIDTH, LANE), const),
            pl.BlockSpec((5 * KV_WIDTH, LANE), const),
            pl.BlockSpec((HALF, tm), lambda i: (0, i % tpb)),
            pl.BlockSpec((HALF, tm), lambda i: (0, i % tpb)),
        ],
        out_specs=[
            pl.BlockSpec((N_MIXERS * MIX_WIDTH, tm), col),
            pl.BlockSpec((tm, 5 * KV_WIDTH), lambda i: (i, 0)),
            pl.BlockSpec((rep, 5 * KV_WIDTH, LANE), lambda i: (i, 0, 0)),
            pl.BlockSpec((IDX_HEADS * IDX_DIM, tm), col),
            pl.BlockSpec((tm, LANE), lambda i: (i, 0)),
            pl.BlockSpec((MISC_ROWS, tm), col),
        ],
        out_shape=[
            jax.ShapeDtypeStruct((N_MIXERS * MIX_WIDTH, N), BF16),
            jax.ShapeDtypeStruct((N, 5 * KV_WIDTH), BF16),
            jax.ShapeDtypeStruct((N // LANE, 5 * KV_WIDTH, LANE), BF16),
            jax.ShapeDtypeStruct((IDX_HEADS * IDX_DIM, N), F32),
            jax.ShapeDtypeStruct((N, LANE), F32),
            jax.ShapeDtypeStruct((MISC_ROWS, N), F32),
        ],
        compiler_params=_cparams(("parallel",)),
        name="in_projection",
    )(xT, g_b, wT, gq_b, gk_b, cosT, sinT)


def _padded_queries(q):
    zeros = jnp.zeros((HEAD_DIM, Q_BLOCK), q.dtype)
    out = []
    for kv in range(N_KV_HEADS):
        cols = []
        for g in range(GROUP):
            hq = kv * GROUP + g
            qh = q[hq * HEAD_DIM:(hq + 1) * HEAD_DIM, :]
            cols.append(jnp.concatenate([qh, zeros] if kv == 0 else [zeros, qh], axis=0))
        out.append(jnp.concatenate(cols, axis=1))
    return out


def _v_chunk(v_ref, blk0, nblk):
    return jnp.concatenate([v_ref[blk0 + j] for j in range(nblk)], axis=1)


def _walk_keys(nch, fn, carry):
    done = 0
    for size in KEY_STEPS:
        per = size // KEY_CHUNK
        n = (nch - done) // per
        base = done * KEY_CHUNK
        carry = lax.fori_loop(
            0, n, lambda c, x, base=base, size=size: fn(pl.multiple_of(base + c * size, KEY_CHUNK), size, x),
            carry)
        done = done + n * per
    return carry


def _masked_attention(nch, sel_fn, k_ref, v_ref, qall, s_ref, acc_ref):
    lanes = N_KV_HEADS * GROUP * Q_BLOCK
    walk = functools.partial(_walk_keys, nch)

    def qk_step(off, size, mx):
        s = jnp.dot(k_ref[pl.ds(off, size), :], qall, preferred_element_type=F32) + sel_fn(off, size)
        s_ref[pl.ds(off, size), :] = s
        return jnp.maximum(mx, jnp.max(s, axis=0, keepdims=True))

    mx = walk(qk_step, jnp.full((1, lanes), M_INIT, F32))
    acc_ref[...] = jnp.zeros(acc_ref.shape, F32)

    def pv_step(off, size, carry):
        p = jnp.exp2(s_ref[pl.ds(off, size), :] - mx).astype(BF16)
        vc = jnp.concatenate([_v_chunk(v_ref, lax.shift_right_logical(off, LANE_SHIFT), size // LANE),
                              jnp.ones((ONES_ROWS, size), BF16)], axis=0)
        acc_ref[...] += jnp.dot(vc, p, preferred_element_type=F32)
        return carry

    walk(pv_step, 0)
    l = acc_ref[KV_WIDTH:KV_WIDTH + 1, :]
    inv = jnp.where(l > 0.0, 1.0 / l, 0.0)
    half = GROUP * Q_BLOCK
    return [acc_ref[kv * HEAD_DIM:(kv + 1) * HEAD_DIM, kv * half:(kv + 1) * half]
            * inv[:, kv * half:(kv + 1) * half] for kv in range(N_KV_HEADS)]


def _dsa_kernel(iq_ref, misc_ref, ik_ref, q_ref, k_ref, v_ref, ltri_ref, o_ref,
                sc_ref, s_ref, acc_ref, *, k_top):
    C = KEY_CHUNK
    i = pl.program_id(1)
    nch = i // (C // Q_BLOCK) + 1
    qpos = i * Q_BLOCK + lax.broadcasted_iota(jnp.int32, (1, Q_BLOCK), 1)

    iq = iq_ref[...].astype(BF16)
    iq_cat = jnp.concatenate([iq[h * IDX_DIM:(h + 1) * IDX_DIM] for h in range(IDX_HEADS)], axis=1)
    w = misc_ref[0:IDX_HEADS, :] * ((IDX_HEADS ** -0.5) * (IDX_DIM ** -0.5))

    def fold(x):
        return x.reshape(x.shape[0] // 8, 8, Q_BLOCK)

    part = lambda v: jnp.full((8, Q_BLOCK), v, F32)

    def score_step(off, size, carry):
        lo, hi = carry
        ikc = ik_ref[pl.ds(off, size), 0:IDX_DIM].astype(BF16)
        d = jnp.dot(ikc, iq_cat, preferred_element_type=F32)
        s = jnp.maximum(d[:, 0:Q_BLOCK], 0.0) * w[0:1, :]
        for h in range(1, IDX_HEADS):
            s = s + jnp.maximum(d[:, h * Q_BLOCK:(h + 1) * Q_BLOCK], 0.0) * w[h:h + 1, :]
        causal = (off + lax.broadcasted_iota(jnp.int32, (size, Q_BLOCK), 0)) <= qpos
        sc_ref[pl.ds(off, size), :] = jnp.where(causal, s, NEG_INF)
        lo = jnp.minimum(lo, jnp.min(fold(jnp.where(causal, s, BIG)), axis=0))
        hi = jnp.maximum(hi, jnp.max(fold(jnp.where(causal, s, -BIG)), axis=0))
        return lo, hi

    lo0, hi0 = _walk_keys(nch, score_step, (part(BIG), part(-BIG)))
    lo0 = jnp.min(lo0, axis=0, keepdims=True)
    hi0 = jnp.max(hi0, axis=0, keepdims=True)

    kf = float(k_top)
    select = i * Q_BLOCK >= k_top

    def count_ge(mid):
        mid8 = jnp.broadcast_to(mid, (8, Q_BLOCK))[None]

        def body(off0, size, cnts):
            cnts = list(cnts)
            for j in range(size // 64):
                off = pl.multiple_of(off0 + j * 64, 64)
                s = sc_ref[pl.ds(off, 64), :].reshape(8, 8, Q_BLOCK)
                cnts[j % 4] = cnts[j % 4] + jnp.sum(jnp.where(s >= mid8, 1.0, 0.0), axis=0)
            return tuple(cnts)

        cnts = _walk_keys(nch, body, (part(0.0),) * 4)
        return jnp.sum((cnts[0] + cnts[1]) + (cnts[2] + cnts[3]), axis=0, keepdims=True)

    def count_step(_, st):
        lo, hi, clo, chi, moved = st
        mid = lo + (hi - lo) * 0.5
        cnt = count_ge(mid)
        up = cnt >= kf
        return (jnp.where(up, mid, lo), jnp.where(up, hi, mid), jnp.where(up, cnt, clo),
                jnp.where(up, chi, cnt), jnp.where(up, moved, 1.0))

    def stats(mid):
        mid8 = jnp.broadcast_to(mid, (8, Q_BLOCK))[None]

        def body(off0, size, carry):
            cnt, a, b = carry
            for j in range(size // 64):
                off = pl.multiple_of(off0 + j * 64, 64)
                s = sc_ref[pl.ds(off, 64), :].reshape(8, 8, Q_BLOCK)
                gt = s > mid8
                cnt = cnt + jnp.sum(jnp.where(gt, 1.0, 0.0), axis=0)
                a = jnp.minimum(a, jnp.min(jnp.where(gt, s, BIG), axis=0))
                b = jnp.maximum(b, jnp.max(jnp.where(gt, -BIG, s), axis=0))
            return cnt, a, b

        cnt, a, b = _walk_keys(nch, body, (part(0.0), part(BIG), part(-BIG)))
        return (jnp.sum(cnt, axis=0, keepdims=True), jnp.min(a, axis=0, keepdims=True),
                jnp.max(b, axis=0, keepdims=True))

    def step(st):
        lo, hi, clo, chi = st
        mid = jnp.maximum(lo + (hi - lo) * 0.5, lo)
        mid = jnp.where(mid >= hi, lo, mid)
        cnt, a, b = stats(mid)
        up = cnt >= kf
        return (jnp.where(up, a, lo), jnp.where(up, hi, b),
                jnp.where(up, cnt, clo), jnp.where(up, chi, cnt))

    def cond(st):
        return jnp.logical_and(select, jnp.max(st[1] - st[0]) > 0.0)

    n_causal = (qpos + 1).astype(F32)
    zero = jnp.zeros((1, Q_BLOCK), F32)
    lo_v, hi_v, clo, chi, moved = lax.fori_loop(
        0, jnp.where(select, BISECT_COUNT_STEPS, 0), count_step, (lo0, hi0, n_causal, zero, zero))

    def snap_step(off0, size, carry):
        lo, hi = carry
        for j in range(size // 64):
            off = pl.multiple_of(off0 + j * 64, 64)
            s = sc_ref[pl.ds(off, 64), :].reshape(8, 8, Q_BLOCK)
            lo = jnp.minimum(lo, jnp.min(jnp.where(s >= lo_v[None], s, BIG), axis=0))
            hi = jnp.maximum(hi, jnp.max(jnp.where(s < hi_v[None], s, -BIG), axis=0))
        return lo, hi

    lo1, hi1 = _walk_keys(jnp.where(select, nch, 0), snap_step, (part(BIG), part(-BIG)))
    lo1 = jnp.min(lo1, axis=0, keepdims=True)
    hi1 = jnp.where(moved > 0.0, jnp.max(hi1, axis=0, keepdims=True), hi0)
    st = lax.fori_loop(0, jnp.where(select, BISECT_FIXED_STEPS, 0), lambda _, s: step(s),
                       (lo1, hi1, clo, chi))
    lo, _, clo, chi = lax.while_loop(cond, lambda s: step(step(s)), st)
    thr = jnp.where(select, lo, 0.5 * NEG_INF)
    need = jnp.where(select, kf - chi, 0.0)
    ties = jnp.logical_and(select, jnp.max((clo - chi) - need) > 0.0)

    @pl.when(ties)
    def _():
        def mask_chunk(c, tie_carry):
            off = pl.multiple_of(c * C, C)
            s_idx = sc_ref[pl.ds(off, C), :]
            eqf = jnp.where(s_idx == thr, 1.0, 0.0)
            before = jnp.dot(ltri_ref[...], eqf.astype(BF16), preferred_element_type=F32) + tie_carry
            sc_ref[pl.ds(off, C), :] = jnp.where(s_idx > thr, 1.0, jnp.where(before < need, eqf, 0.0))
            return tie_carry + jnp.sum(eqf, axis=0, keepdims=True)

        lax.fori_loop(0, nch, mask_chunk, jnp.zeros((1, Q_BLOCK), F32))

    thr_sel = jnp.where(ties, 0.5, thr)
    qall = jnp.concatenate(_padded_queries(q_ref[...]), axis=1)

    def sel_fn(off, size):
        bias = jnp.where(sc_ref[pl.ds(off, size), :] >= thr_sel, 0.0, MASK_VAL)
        return _lane_tile(bias, N_KV_HEADS * GROUP)

    outs = _masked_attention(nch, sel_fn, k_ref, v_ref, qall, s_ref, acc_ref)
    for kv in range(N_KV_HEADS):
        o = outs[kv]
        for g in range(GROUP):
            hq = kv * GROUP + g
            o_ref[hq * HEAD_DIM:(hq + 1) * HEAD_DIM, :] = o[:, g * Q_BLOCK:(g + 1) * Q_BLOCK].astype(BF16)


def _dsa(iqT, miscT, ik_rows, qT, k_rows, vT128, ltri, B, T):
    nq = T // Q_BLOCK
    k_top = min(DSA_TOPK_MAX, T // 4)
    N = B * T
    return pl.pallas_call(
        functools.partial(_dsa_kernel, k_top=k_top),
        grid=(B, nq),
        in_specs=[
            pl.BlockSpec((IDX_HEADS * IDX_DIM, Q_BLOCK), lambda b, i: (0, b * nq + i)),
            pl.BlockSpec((MISC_ROWS, Q_BLOCK), lambda b, i: (0, b * nq + i)),
            pl.BlockSpec((T, LANE), lambda b, i: (b, 0)),
            pl.BlockSpec((MIX_WIDTH, Q_BLOCK), lambda b, i: (0, b * nq + i)),
            pl.BlockSpec((T, KV_WIDTH), lambda b, i: (b, 0)),
            pl.BlockSpec((nq, KV_WIDTH, LANE), lambda b, i: (b, 0, 0)),
            pl.BlockSpec((KEY_CHUNK, KEY_CHUNK), lambda b, i: (0, 0)),
        ],
        out_specs=pl.BlockSpec((MIX_WIDTH, Q_BLOCK), lambda b, i: (0, b * nq + i)),
        out_shape=jax.ShapeDtypeStruct((MIX_WIDTH, N), BF16),
        scratch_shapes=[
            pltpu.VMEM((T, Q_BLOCK), F32),
            pltpu.VMEM((T, N_KV_HEADS * GROUP * Q_BLOCK), F32),
            pltpu.VMEM((KV_WIDTH + ONES_ROWS, N_KV_HEADS * GROUP * Q_BLOCK), F32),
        ],
        compiler_params=_cparams(("parallel", "arbitrary")),
        name="dsa_attention",
    )(iqT, miscT, ik_rows, qT, k_rows, vT128, ltri)


def _band_kernel(q_ref, k_ref, v_ref, sink_ref, o_ref, *, n_prev, window, use_sink):
    i = pl.program_id(1)
    blk0 = jnp.maximum(i - n_prev, 0)
    nblk = n_prev + 1
    wb = nblk * Q_BLOCK
    kb = k_ref[pl.ds(pl.multiple_of(blk0 * Q_BLOCK, Q_BLOCK), wb), :]
    vb = jnp.concatenate([_v_chunk(v_ref, blk0, nblk), jnp.ones((ONES_ROWS, wb), BF16)], axis=0)
    qall = jnp.concatenate(_padded_queries(q_ref[...]), axis=1)
    rel = ((i - blk0) * Q_BLOCK + lax.broadcasted_iota(jnp.int32, (wb, Q_BLOCK), 1)
           - lax.broadcasted_iota(jnp.int32, (wb, Q_BLOCK), 0))
    bias = jnp.where((rel >= 0) & (rel < window), 0.0, MASK_VAL)
    s = jnp.dot(kb, qall, preferred_element_type=F32) + _lane_tile(bias, N_KV_HEADS * GROUP)
    m = jnp.max(s, axis=0, keepdims=True)
    if use_sink:
        sink = sink_ref[0:1, :]
        m = jnp.maximum(m, sink)
    o = jnp.dot(vb, jnp.exp2(s - m).astype(BF16), preferred_element_type=F32)
    l = o[KV_WIDTH:KV_WIDTH + 1, :]
    if use_sink:
        l = l + jnp.exp2(sink - m)
    inv = 1.0 / l
    half = GROUP * Q_BLOCK
    for kv in range(N_KV_HEADS):
        for g in range(GROUP):
            hq = kv * GROUP + g
            lanes = slice(kv * half + g * Q_BLOCK, kv * half + (g + 1) * Q_BLOCK)
            o_ref[hq * HEAD_DIM:(hq + 1) * HEAD_DIM, :] = (
                o[kv * HEAD_DIM:(kv + 1) * HEAD_DIM, lanes] * inv[:, lanes]).astype(o_ref.dtype)


def _band(qT, q_mixer, k_rows, kv_index, vT128, sink_lanes, window, use_sink, out_dtype, B, T, name):
    nq = T // Q_BLOCK
    N = B * T
    n_prev = (window + Q_BLOCK - 2) // Q_BLOCK
    return pl.pallas_call(
        functools.partial(_band_kernel, n_prev=n_prev, window=window, use_sink=use_sink),
        grid=(B, nq),
        in_specs=[
            pl.BlockSpec((MIX_WIDTH, Q_BLOCK), lambda b, i: (q_mixer, b * nq + i)),
            pl.BlockSpec((T, KV_WIDTH), lambda b, i: (b, kv_index)),
            pl.BlockSpec((nq, KV_WIDTH, LANE), lambda b, i: (b, kv_index, 0)),
            pl.BlockSpec((8, N_Q_HEADS * Q_BLOCK), lambda b, i: (0, 0)),
        ],
        out_specs=pl.BlockSpec((MIX_WIDTH, Q_BLOCK), lambda b, i: (0, b * nq + i)),
        out_shape=jax.ShapeDtypeStruct((MIX_WIDTH, N), out_dtype),
        compiler_params=_cparams(("parallel", "arbitrary")),
        name=name,
    )(qT, k_rows, vT128, sink_lanes)


def _cmp_kernel(xa_ref, xb_ref, pe_ref, w1_ref, w2_ref, o_ref):
    half = (CMP_LEN // 2) * HEAD_DIM
    xa = (xa_ref[...].astype(F32) + pe_ref[0:1, :]).astype(BF16)
    xb = (xb_ref[...].astype(F32) + pe_ref[1:2, :]).astype(BF16)
    hid = (jnp.dot(xa, w1_ref[0:half, :], preferred_element_type=F32)
           + jnp.dot(xb, w1_ref[half:2 * half, :], preferred_element_type=F32))
    hid = jnp.maximum(hid, 0.0).astype(BF16)
    o_ref[...] = jnp.dot(hid, w2_ref[...], preferred_element_type=F32)


def _compress(chunks, pe, w1, w2, name):
    G, nc, half = chunks.shape
    nxt = jnp.concatenate([chunks[:, 1:], jnp.zeros((G, 1, half), chunks.dtype)], axis=1)
    pe2 = jnp.zeros((8, half), F32).at[0:2].set(pe.reshape(2, half))
    return pl.pallas_call(
        _cmp_kernel,
        grid=(G,),
        in_specs=[
            pl.BlockSpec((None, nc, half), lambda g: (g, 0, 0)),
            pl.BlockSpec((None, nc, half), lambda g: (g, 0, 0)),
            pl.BlockSpec((8, half), lambda g: (0, 0)),
            pl.BlockSpec((2 * half, CMP_HIDDEN), lambda g: (0, 0)),
            pl.BlockSpec((CMP_HIDDEN, HEAD_DIM), lambda g: (0, 0)),
        ],
        out_specs=pl.BlockSpec((None, nc, HEAD_DIM), lambda g: (g, 0, 0)),
        out_shape=jax.ShapeDtypeStruct((G, nc, HEAD_DIM), F32),
        compiler_params=_cparams(("parallel",)),
        name=name,
    )(chunks, nxt, pe2, w1.astype(BF16), w2.astype(BF16))


def _nsa_kernel(q_ref, kc_ref, vc_ref, ks_ref, vs_ref, misc_ref, ow_ref, ov_ref, e_ref, o_ref,
                s_ref, acc_ref, *, n_top):
    C = KEY_CHUNK
    i = pl.program_id(1)
    nch = i // (C // Q_BLOCK) + 1
    ncmp = kc_ref.shape[0]
    nsel = ov_ref.shape[0]
    lanes = GROUP * Q_BLOCK
    t = i * Q_BLOCK + lax.broadcasted_iota(jnp.int32, (1, Q_BLOCK), 1)
    t4 = i * Q_BLOCK + (lax.broadcasted_iota(jnp.int32, (1, lanes), 1) & (Q_BLOCK - 1))
    crow = lax.broadcasted_iota(jnp.int32, (ncmp, lanes), 0)
    jrow = lax.broadcasted_iota(jnp.int32, (nsel, Q_BLOCK), 0)
    cur = t // SLC_LEN
    qpad = _padded_queries(q_ref[...])
    kc = kc_ref[...]
    vcm = vc_ref[...]
    ov = ov_ref[...]

    o_cmps, selbs = [], []
    for kv in range(N_KV_HEADS):
        s = jnp.dot(kc, qpad[kv], preferred_element_type=F32)
        s = jnp.where(crow * CMP_STRIDE + (CMP_LEN - 1) <= t4, s, MASK_VAL)
        m = jnp.maximum(jnp.max(s, axis=0, keepdims=True), M_INIT)
        p = jnp.exp2(s - m)
        l = jnp.sum(p, axis=0, keepdims=True)
        p = p * jnp.where(l > 0.0, 1.0 / l, 0.0)
        o_cmp = jnp.dot(vcm, p.astype(BF16), preferred_element_type=F32)[kv * HEAD_DIM:(kv + 1) * HEAD_DIM, :]

        pg = p[:, 0:Q_BLOCK]
        for g in range(1, GROUP):
            pg = pg + p[:, g * Q_BLOCK:(g + 1) * Q_BLOCK]
        p_hi = pg.astype(BF16)
        p_lo = (pg - p_hi.astype(F32)).astype(BF16)
        imp = (jnp.dot(ov, p_hi, preferred_element_type=F32)
               + jnp.dot(ov, p_lo, preferred_element_type=F32))
        forced = (jrow == 0) | (jrow == cur) | (jrow == cur - 1)
        imp = jnp.where(forced, FORCE_SCORE, imp)
        imp = jnp.where(jrow * SLC_LEN <= t, imp, NEG_INF)
        tiles = [imp[r0:r0 + 8] for r0 in range(0, nsel, 8)]
        ranks = [jnp.zeros((8, Q_BLOCK), F32) for _ in tiles]
        sub = lax.broadcasted_iota(jnp.int32, (8, Q_BLOCK), 0)
        for ii in range(nsel):
            r = jnp.broadcast_to(imp[ii:ii + 1], (8, Q_BLOCK))
            for jt, tile in enumerate(tiles):
                ge = jnp.where(r >= tile, 1.0, 0.0)
                gt = jnp.where(r > tile, 1.0, 0.0)
                if jt > ii // 8:
                    ahead = ge
                elif jt < ii // 8:
                    ahead = gt
                else:
                    ahead = jnp.where(sub > ii % 8, ge, gt)
                ranks[jt] = ranks[jt] + ahead
        rank = jnp.concatenate(ranks, axis=0)
        selbs.append(jnp.where(rank < float(n_top), 1.0, 0.0).astype(BF16))
        o_cmps.append(o_cmp)

    qall = jnp.concatenate(qpad, axis=1)
    selb_all = jnp.concatenate(selbs, axis=1)

    def sel_fn(off, size):
        mk = jnp.dot(e_ref[pl.ds(off, size), :], selb_all, preferred_element_type=F32)
        causal = (off + lax.broadcasted_iota(jnp.int32, (size, Q_BLOCK), 0)) <= t
        parts = []
        for kv in range(N_KV_HEADS):
            keep = jnp.where(causal, mk[:, kv * Q_BLOCK:(kv + 1) * Q_BLOCK], 0.0) > 0.5
            parts.append(_lane_tile(jnp.where(keep, 0.0, MASK_VAL), GROUP))
        return jnp.concatenate(parts, axis=1)

    o_slcs = _masked_attention(nch, sel_fn, ks_ref, vs_ref, qall, s_ref, acc_ref)

    for kv in range(N_KV_HEADS):
        for g in range(GROUP):
            hq = kv * GROUP + g
            gr = GATE_ROW0 + hq * 3
            gate = [1.0 / (1.0 + jnp.exp(-misc_ref[gr + br:gr + br + 1, :])) for br in range(3)]
            out = (gate[0] * o_cmps[kv][:, g * Q_BLOCK:(g + 1) * Q_BLOCK]
                   + gate[1] * o_slcs[kv][:, g * Q_BLOCK:(g + 1) * Q_BLOCK]
                   + gate[2] * ow_ref[hq * HEAD_DIM:(hq + 1) * HEAD_DIM, :])
            o_ref[hq * HEAD_DIM:(hq + 1) * HEAD_DIM, :] = out.astype(BF16)


def _nsa(qT, kcmp_rows, vcmpT, k_rows, vT128, miscT, o_winT, ovT, emat, B, T):
    nq = T // Q_BLOCK
    N = B * T
    ncmp = T // CMP_STRIDE
    nsel = T // SLC_LEN
    n_top = min(SLC_TOPN, nsel)
    return pl.pallas_call(
        functools.partial(_nsa_kernel, n_top=n_top),
        grid=(B, nq),
        in_specs=[
            pl.BlockSpec((MIX_WIDTH, Q_BLOCK), lambda b, i: (2, b * nq + i)),
            pl.BlockSpec((None, ncmp, KV_WIDTH), lambda b, i: (b, 0, 0)),
            pl.BlockSpec((None, KV_WIDTH, ncmp), lambda b, i: (b, 0, 0)),
            pl.BlockSpec((T, KV_WIDTH), lambda b, i: (b, 3)),
            pl.BlockSpec((nq, KV_WIDTH, LANE), lambda b, i: (b, 3, 0)),
            pl.BlockSpec((MISC_ROWS, Q_BLOCK), lambda b, i: (0, b * nq + i)),
            pl.BlockSpec((MIX_WIDTH, Q_BLOCK), lambda b, i: (0, b * nq + i)),
            pl.BlockSpec((nsel, ncmp), lambda b, i: (0, 0)),
            pl.BlockSpec((T, nsel), lambda b, i: (0, 0)),
        ],
        out_specs=pl.BlockSpec((MIX_WIDTH, Q_BLOCK), lambda b, i: (0, b * nq + i)),
        out_shape=jax.ShapeDtypeStruct((MIX_WIDTH, N), BF16),
        scratch_shapes=[
            pltpu.VMEM((T, N_KV_HEADS * GROUP * Q_BLOCK), F32),
            pltpu.VMEM((KV_WIDTH + ONES_ROWS, N_KV_HEADS * GROUP * Q_BLOCK), F32),
        ],
        compiler_params=_cparams(("parallel", "arbitrary")),
        name="nsa_attention",
    )(qT, kcmp_rows, vcmpT, k_rows, vT128, miscT, o_winT, ovT, emat)


def _merge_kernel(x_ref, g_ref, wg_ref, oa_ref, ob_ref, oc_ref, wb_ref, wo_ref, out_ref):
    tm = x_ref.shape[1]
    x = x_ref[...]
    ss = jnp.sum(x * x, axis=0, keepdims=True)
    h = (x * lax.rsqrt(ss * (1.0 / D_MODEL) + EPS) * _lane_tile(g_ref[...], tm // LANE)).astype(BF16)
    acc = None
    for n, o_ref in enumerate((oa_ref, ob_ref, oc_ref)):
        gm = jnp.dot(wg_ref[n * D_MODEL:(n + 1) * D_MODEL, :], h, preferred_element_type=F32)
        y = jnp.dot(wb_ref[n], o_ref[...], preferred_element_type=F32)
        term = y * (1.0 / (1.0 + jnp.exp(-gm)))
        acc = term if acc is None else acc + term
    out_ref[...] = x + jnp.dot(wo_ref[...], acc.astype(BF16), preferred_element_type=F32)


def _merge(xT, g_b, wgT, oa, ob, oc, wbT, woT):
    N = xT.shape[1]
    tm = 512
    col = lambda i: (0, i)
    return pl.pallas_call(
        _merge_kernel,
        grid=(N // tm,),
        in_specs=[
            pl.BlockSpec((D_MODEL, tm), col),
            pl.BlockSpec((D_MODEL, LANE), lambda i: (0, 0)),
            pl.BlockSpec((N_MIXERS * D_MODEL, D_MODEL), lambda i: (0, 0)),
            pl.BlockSpec((MIX_WIDTH, tm), col),
            pl.BlockSpec((MIX_WIDTH, tm), col),
            pl.BlockSpec((MIX_WIDTH, tm), col),
            pl.BlockSpec((N_MIXERS, D_MODEL, MIX_WIDTH), lambda i: (0, 0, 0)),
            pl.BlockSpec((D_MODEL, D_MODEL), lambda i: (0, 0)),
        ],
        out_specs=pl.BlockSpec((D_MODEL, tm), col),
        out_shape=jax.ShapeDtypeStruct((D_MODEL, N), F32),
        compiler_params=_cparams(("parallel",)),
        name="merge_out_projection",
    )(xT, g_b, wgT, oa, ob, oc, wbT, woT)


def _mlp_kernel(x_ref, g_ref, wu_ref, wd_ref, out_ref, h_ref, acc_ref, *, token_major_out):
    f = pl.program_id(1)
    tm = x_ref.shape[1]

    @pl.when(f == 0)
    def _():
        x = x_ref[...]
        ss = jnp.sum(x * x, axis=0, keepdims=True)
        h_ref[...] = (x * lax.rsqrt(ss * (1.0 / D_MODEL) + EPS)
                      * _lane_tile(g_ref[...], tm // LANE)).astype(BF16)
        acc_ref[...] = jnp.zeros(acc_ref.shape, F32)

    u = jnp.maximum(jnp.dot(wu_ref[...], h_ref[...], preferred_element_type=F32), 0.0)
    acc_ref[...] += jnp.dot(wd_ref[...], (u * u).astype(BF16), preferred_element_type=F32)

    @pl.when(f == pl.num_programs(1) - 1)
    def _():
        y = x_ref[...] + acc_ref[...]
        out_ref[...] = y.T if token_major_out else y


def _mlp(xT, g_b, wuT, wdT, token_major_out):
    N = xT.shape[1]
    tm, tf = 1024, 1024
    if token_major_out:
        out_spec = pl.BlockSpec((tm, D_MODEL), lambda i, f: (i, 0))
        out_shape = jax.ShapeDtypeStruct((N, D_MODEL), F32)
    else:
        out_spec = pl.BlockSpec((D_MODEL, tm), lambda i, f: (0, i))
        out_shape = jax.ShapeDtypeStruct((D_MODEL, N), F32)
    return pl.pallas_call(
        functools.partial(_mlp_kernel, token_major_out=token_major_out),
        grid=(N // tm, D_FF // tf),
        in_specs=[
            pl.BlockSpec((D_MODEL, tm), lambda i, f: (0, i)),
            pl.BlockSpec((D_MODEL, LANE), lambda i, f: (0, 0)),
            pl.BlockSpec((tf, D_MODEL), lambda i, f: (f, 0)),
            pl.BlockSpec((D_MODEL, tf), lambda i, f: (0, f)),
        ],
        out_specs=out_spec,
        out_shape=out_shape,
        scratch_shapes=[pltpu.VMEM((D_MODEL, tm), BF16), pltpu.VMEM((D_MODEL, tm), F32)],
        compiler_params=_cparams(("parallel", "arbitrary")),
        name="relu2_mlp",
    )(xT, g_b, wuT, wdT)


def _to_chunks(rows, B, T):
    half_tokens = CMP_LEN // 2
    t = rows.reshape(B, T, N_KV_HEADS, HEAD_DIM).transpose(0, 2, 1, 3)
    return t.reshape(B * N_KV_HEADS, T // half_tokens, half_tokens * HEAD_DIM)


def kernel(x, norm_mix, w_in, q_norm, k_norm, sinks, cmp_pe_k, cmp_pe_v, w_ck1, w_ck2, w_cv1, w_cv2,
           w_branch, w_out, norm_mlp, w_up, w_down):
    B, T, D = x.shape
    N = B * T
    depth = w_in.shape[0]
    nq = T // Q_BLOCK
    ncmp = T // CMP_STRIDE
    nsel = T // SLC_LEN

    inv_freq = ROPE_THETA ** (-jnp.arange(0, HEAD_DIM, 2, dtype=F32) / HEAD_DIM)
    ang = jnp.arange(T, dtype=F32)[:, None] * inv_freq[None, :]
    cosT, sinT = jnp.cos(ang).T, jnp.sin(ang).T

    ci = np.arange(KEY_CHUNK)
    ltri = jnp.asarray(ci[None, :] < ci[:, None], BF16)
    emat = jnp.asarray((np.arange(T)[:, None] // SLC_LEN) == np.arange(nsel)[None, :], BF16)
    cstart = np.arange(ncmp) * CMP_STRIDE
    sstart = np.arange(nsel) * SLC_LEN
    n_cmp_valid = (T - CMP_LEN) // CMP_STRIDE + 1
    ov = ((cstart[None, :] < sstart[:, None] + SLC_LEN) & (cstart[None, :] + CMP_LEN > sstart[:, None])
          & (np.arange(ncmp)[None, :] < n_cmp_valid))
    ovT = jnp.asarray(ov, BF16)

    xT = x.reshape(N, D).T

    sl = lambda s: w_in[:, :, s[0]:s[1]]
    wT_all = jnp.concatenate(
        [sl(S_QA), sl(S_QB), sl(S_QC),
         sl(S_KA), sl(S_KB), sl(S_KC), sl(S_KSL), sl(S_KWN),
         sl(S_VA), sl(S_VB), sl(S_VC), sl(S_VSL), sl(S_VWN),
         sl(S_IQ), sl(S_IK), sl(S_IW), jnp.zeros((depth, D, GATE_ROW0 - IDX_HEADS), w_in.dtype), sl(S_GC)],
        axis=2).transpose(0, 2, 1).astype(BF16)
    wgT_all = sl(S_GM).transpose(0, 2, 1).astype(BF16)
    wbT_all = w_branch.transpose(0, 1, 3, 2).astype(BF16)
    woT_all = w_out.transpose(0, 2, 1).astype(BF16)
    wuT_all = w_up.transpose(0, 2, 1).astype(BF16)
    wdT_all = w_down.transpose(0, 2, 1).astype(BF16)
    lane_b = lambda v: jnp.broadcast_to(v.astype(F32)[..., None], v.shape + (LANE,))
    gmix_all, gmlp_all = lane_b(norm_mix), lane_b(norm_mlp)
    gq_all = lane_b(jnp.tile(q_norm[:, :, None, :], (1, 1, N_Q_HEADS, 1)).reshape(depth, -1))
    gk_all = lane_b(jnp.tile(k_norm[:, jnp.array([0, 1, 2, 2, 2])][:, :, None, :],
                             (1, 1, N_KV_HEADS, 1)).reshape(depth, -1))
    sink_all = jnp.broadcast_to(jnp.repeat(sinks.astype(F32) * LOG2E, Q_BLOCK, axis=1)[:, None, :],
                                (depth, 8, N_Q_HEADS * Q_BLOCK))

    for l in range(depth):
        wgT, sink_lanes = wgT_all[l], sink_all[l]
        qT, k_rows, vT128, iqT, ik_rows, miscT = _project(
            xT, gmix_all[l], wT_all[l], gq_all[l], gk_all[l], cosT, sinT, T)

        o_a = _dsa(iqT, miscT, ik_rows, qT, k_rows, vT128, ltri, B, T)
        o_b = _band(qT, 1, k_rows, 1, vT128, sink_lanes, SWA_WINDOW, True, BF16, B, T, "swa_attention")
        o_w = _band(qT, 2, k_rows, 4, vT128, sink_lanes, NSA_WINDOW, False, F32, B, T, "nsa_window_attention")

        kc_chunks = _to_chunks(k_rows[:, 2 * KV_WIDTH:3 * KV_WIDTH].reshape(B, T, KV_WIDTH), B, T)
        vc_rows = vT128[:, 2 * KV_WIDTH:3 * KV_WIDTH, :].reshape(B, nq, KV_WIDTH, LANE)
        vc_rows = vc_rows.transpose(0, 1, 3, 2).reshape(B, T, KV_WIDTH)
        vc_chunks = _to_chunks(vc_rows, B, T)
        k_cmp = _compress(kc_chunks, cmp_pe_k[l], w_ck1[l], w_ck2[l], "compress_k")
        v_cmp = _compress(vc_chunks, cmp_pe_v[l], w_cv1[l], w_cv2[l], "compress_v")
        kcmp_rows = k_cmp.reshape(B, N_KV_HEADS, ncmp, HEAD_DIM).transpose(0, 2, 1, 3)
        kcmp_rows = kcmp_rows.reshape(B, ncmp, KV_WIDTH).astype(BF16)
        vcmpT = v_cmp.reshape(B, N_KV_HEADS, ncmp, HEAD_DIM).transpose(0, 1, 3, 2)
        vcmpT = vcmpT.reshape(B, KV_WIDTH, ncmp).astype(BF16)

        o_c = _nsa(qT, kcmp_rows, vcmpT, k_rows, vT128, miscT, o_w, ovT, emat, B, T)

        xT = _merge(xT, gmix_all[l], wgT, o_a, o_b, o_c, wbT_all[l], woT_all[l])
        xT = _mlp(xT, gmlp_all[l], wuT_all[l], wdT_all[l], token_major_out=(l == depth - 1))

    return xT.reshape(B, T, D)
```

```python
import functools

import numpy as np
import jax
import jax.numpy as jnp
from jax import lax
from jax.experimental import pallas as pl
from jax.experimental.pallas import tpu as pltpu

F32 = jnp.float32
BF16 = jnp.bfloat16

D_MODEL = 1024
HEAD_DIM = 64
HALF = HEAD_DIM // 2
N_Q_HEADS = 8
N_KV_HEADS = 2
GROUP = N_Q_HEADS // N_KV_HEADS
MIX_WIDTH = N_Q_HEADS * HEAD_DIM
KV_WIDTH = N_KV_HEADS * HEAD_DIM
N_MIXERS = 3
D_FF = 4 * D_MODEL
ROPE_THETA = 10000.0
EPS = 1e-6
NEG_INF = -1e30
MASK_VAL = -2e30
M_INIT = -1e30
BIG = 3e38
Q_BLOCK = 128
KEY_CHUNK = 512
BISECT_COUNT_STEPS = 14
BISECT_FIXED_STEPS = 2
ONES_ROWS = 16
IDX_HEADS = 4
IDX_DIM = 64
DSA_TOPK_MAX = 256
SWA_WINDOW = 128
CMP_LEN = 32
CMP_STRIDE = 16
CMP_HIDDEN = 256
SLC_LEN = 64
SLC_TOPN = 16
NSA_WINDOW = 512
FORCE_SCORE = 1e9
LOG2E = 1.4426950408889634
Q_SCALE = (HEAD_DIM ** -0.5) * LOG2E
LANE = 128
LANE_SHIFT = 7
KEY_STEPS = (2048, 1024, 512)
VMEM_LIMIT = 56 * 1024 * 1024

IN_SIZES = (
    MIX_WIDTH, KV_WIDTH, KV_WIDTH, IDX_HEADS * IDX_DIM, IDX_DIM, IDX_HEADS,
    MIX_WIDTH, KV_WIDTH, KV_WIDTH,
    MIX_WIDTH, KV_WIDTH, KV_WIDTH, KV_WIDTH, KV_WIDTH, KV_WIDTH, KV_WIDTH, 3 * N_Q_HEADS,
    N_MIXERS * D_MODEL,
)
_OFF = np.concatenate([[0], np.cumsum(IN_SIZES)]).astype(int)
(S_QA, S_KA, S_VA, S_IQ, S_IK, S_IW, S_QB, S_KB, S_VB,
 S_QC, S_KC, S_VC, S_KSL, S_VSL, S_KWN, S_VWN, S_GC, S_GM) = [
    (int(_OFF[i]), int(_OFF[i + 1])) for i in range(len(IN_SIZES))]

R_Q = 0
R_K = R_Q + N_MIXERS * MIX_WIDTH
R_V = R_K + 5 * KV_WIDTH
R_IQ = R_V + 5 * KV_WIDTH
R_IK = R_IQ + IDX_HEADS * IDX_DIM
R_MISC = R_IK + IDX_DIM
MISC_ROWS = 32
GATE_ROW0 = 8
R_END = R_MISC + MISC_ROWS
PROJ_TM = 1024


def _cparams(sem):
    return pltpu.CompilerParams(dimension_semantics=sem, vmem_limit_bytes=VMEM_LIMIT)


def _lane_tile(a, n):
    return a if n == 1 else jnp.concatenate([a] * n, axis=1)


def _proj_kernel(x_ref, g_ref, w_ref, gq_ref, gk_ref, cos_ref, sin_ref,
                 q_ref, k_ref, v_ref, iq_ref, ik_ref, misc_ref, *, token_major_in):
    x = x_ref[...].T if token_major_in else x_ref[...]
    tm = x.shape[1]
    rep = tm // LANE
    ss = jnp.sum(x * x, axis=0, keepdims=True)
    h = (x * lax.rsqrt(ss * (1.0 / D_MODEL) + EPS) * _lane_tile(g_ref[...], rep)).astype(BF16)
    cos = cos_ref[...]
    sin = sin_ref[...]

    def rope(z):
        z1, z2 = z[:HALF], z[HALF:]
        return jnp.concatenate([z1 * cos - z2 * sin, z2 * cos + z1 * sin], axis=0)

    def head_norm(z, gain):
        ssq = jnp.sum(z * z, axis=0, keepdims=True)
        return z * lax.rsqrt(ssq * (1.0 / HEAD_DIM) + EPS) * _lane_tile(gain, rep)

    for m in range(N_MIXERS):
        z = jnp.dot(w_ref[R_Q + m * MIX_WIDTH:R_Q + (m + 1) * MIX_WIDTH, :], h,
                    preferred_element_type=F32)
        for hh in range(N_Q_HEADS):
            r0 = m * MIX_WIDTH + hh * HEAD_DIM
            zh = head_norm(z[hh * HEAD_DIM:(hh + 1) * HEAD_DIM], gq_ref[r0:r0 + HEAD_DIM, :])
            q_ref[r0:r0 + HEAD_DIM, :] = (rope(zh) * Q_SCALE).astype(BF16)

    z = jnp.dot(w_ref[R_K:R_V, :], h, preferred_element_type=F32)
    for mk in range(5):
        heads = []
        for kvh in range(N_KV_HEADS):
            r0 = mk * KV_WIDTH + kvh * HEAD_DIM
            heads.append(rope(head_norm(z[r0:r0 + HEAD_DIM], gk_ref[r0:r0 + HEAD_DIM, :])))
        k_ref[:, mk * KV_WIDTH:(mk + 1) * KV_WIDTH] = jnp.concatenate(heads, axis=0).T.astype(BF16)

    z = jnp.dot(w_ref[R_V:R_IQ, :], h, preferred_element_type=F32)
    for j in range(rep):
        v_ref[j] = z[:, j * LANE:(j + 1) * LANE].astype(BF16)

    z = jnp.dot(w_ref[R_IQ:R_END, :], h, preferred_element_type=F32)
    for hh in range(IDX_HEADS):
        r0 = hh * IDX_DIM
        iq_ref[r0:r0 + IDX_DIM, :] = rope(z[r0:r0 + IDX_DIM])
    ik = rope(z[R_IK - R_IQ:R_MISC - R_IQ])
    ik_ref[...] = jnp.concatenate([ik, jnp.zeros((LANE - IDX_DIM, tm), F32)], axis=0).T
    misc_ref[...] = z[R_MISC - R_IQ:R_END - R_IQ]


def _x_spec(tm, token_major):
    if token_major:
        return pl.BlockSpec((tm, D_MODEL), lambda i: (i, 0))
    return pl.BlockSpec((D_MODEL, tm), lambda i: (0, i))


def _project(xs, token_major_in, g_b, wT, gq_b, gk_b, cosT, sinT, T):
    N = xs.shape[0] if token_major_in else xs.shape[1]
    xT = xs
    tm = PROJ_TM
    tpb = T // tm
    rep = tm // LANE
    col = lambda i: (0, i)
    const = lambda i: (0, 0)
    return pl.pallas_call(
        functools.partial(_proj_kernel, token_major_in=token_major_in),
        grid=(N // tm,),
        in_specs=[
            _x_spec(tm, token_major_in),
            pl.BlockSpec((D_MODEL, LANE), const),
            pl.BlockSpec((R_END, D_MODEL), const),
            pl.BlockSpec((N_MIXERS * MIX_WIDTH, LANE), const),
            pl.BlockSpec((5 * KV_WIDTH, LANE), const),
            pl.BlockSpec((HALF, tm), lambda i: (0, i % tpb)),
            pl.BlockSpec((HALF, tm), lambda i: (0, i % tpb)),
        ],
        out_specs=[
            pl.BlockSpec((N_MIXERS * MIX_WIDTH, tm), col),
            pl.BlockSpec((tm, 5 * KV_WIDTH), lambda i: (i, 0)),
            pl.BlockSpec((rep, 5 * KV_WIDTH, LANE), lambda i: (i, 0, 0)),
            pl.BlockSpec((IDX_HEADS * IDX_DIM, tm), col),
            pl.BlockSpec((tm, LANE), lambda i: (i, 0)),
            pl.BlockSpec((MISC_ROWS, tm), col),
        ],
        out_shape=[
            jax.ShapeDtypeStruct((N_MIXERS * MIX_WIDTH, N), BF16),
            jax.ShapeDtypeStruct((N, 5 * KV_WIDTH), BF16),
            jax.ShapeDtypeStruct((N // LANE, 5 * KV_WIDTH, LANE), BF16),
            jax.ShapeDtypeStruct((IDX_HEADS * IDX_DIM, N), F32),
            jax.ShapeDtypeStruct((N, LANE), F32),
            jax.ShapeDtypeStruct((MISC_ROWS, N), F32),
        ],
        compiler_params=_cparams(("parallel",)),
        name="in_projection",
    )(xT, g_b, wT, gq_b, gk_b, cosT, sinT)


def _padded_queries(q):
    zeros = jnp.zeros((HEAD_DIM, Q_BLOCK), q.dtype)
    out = []
    for kv in range(N_KV_HEADS):
        cols = []
        for g in range(GROUP):
            hq = kv * GROUP + g
            qh = q[hq * HEAD_DIM:(hq + 1) * HEAD_DIM, :]
            cols.append(jnp.concatenate([qh, zeros] if kv == 0 else [zeros, qh], axis=0))
        out.append(jnp.concatenate(cols, axis=1))
    return out


def _v_chunk(v_ref, blk0, nblk):
    return jnp.concatenate([v_ref[blk0 + j] for j in range(nblk)], axis=1)


def _walk_keys(nch, fn, carry):
    done = 0
    for size in KEY_STEPS:
        per = size // KEY_CHUNK
        n = (nch - done) // per
        base = done * KEY_CHUNK
        carry = lax.fori_loop(
            0, n, lambda c, x, base=base, size=size: fn(pl.multiple_of(base + c * size, KEY_CHUNK), size, x),
            carry)
        done = done + n * per
    return carry


def _masked_attention(nch, sel_fn, k_ref, v_ref, qall, s_ref, acc_ref):
    lanes = N_KV_HEADS * GROUP * Q_BLOCK
    walk = functools.partial(_walk_keys, nch)

    def qk_step(off, size, mx):
        s = jnp.dot(k_ref[pl.ds(off, size), :], qall, preferred_element_type=F32) + sel_fn(off, size)
        s_ref[pl.ds(off, size), :] = s
        return jnp.maximum(mx, jnp.max(s, axis=0, keepdims=True))

    mx = walk(qk_step, jnp.full((1, lanes), M_INIT, F32))
    acc_ref[...] = jnp.zeros(acc_ref.shape, F32)

    def pv_step(off, size, carry):
        p = jnp.exp2(s_ref[pl.ds(off, size), :] - mx).astype(BF16)
        vc = jnp.concatenate([_v_chunk(v_ref, lax.shift_right_logical(off, LANE_SHIFT), size // LANE),
                              jnp.ones((ONES_ROWS, size), BF16)], axis=0)
        acc_ref[...] += jnp.dot(vc, p, preferred_element_type=F32)
        return carry

    walk(pv_step, 0)
    l = acc_ref[KV_WIDTH:KV_WIDTH + 1, :]
    inv = jnp.where(l > 0.0, 1.0 / l, 0.0)
    half = GROUP * Q_BLOCK
    return [acc_ref[kv * HEAD_DIM:(kv + 1) * HEAD_DIM, kv * half:(kv + 1) * half]
            * inv[:, kv * half:(kv + 1) * half] for kv in range(N_KV_HEADS)]


def _dsa_kernel(iq_ref, misc_ref, ik_ref, q_ref, k_ref, v_ref, ltri_ref, o_ref,
                sc_ref, s_ref, acc_ref, *, k_top):
    C = KEY_CHUNK
    i = pl.program_id(1)
    nch = i // (C // Q_BLOCK) + 1
    qpos = i * Q_BLOCK + lax.broadcasted_iota(jnp.int32, (1, Q_BLOCK), 1)

    iq = iq_ref[...].astype(BF16)
    iq_cat = jnp.concatenate([iq[h * IDX_DIM:(h + 1) * IDX_DIM] for h in range(IDX_HEADS)], axis=1)
    w = misc_ref[0:IDX_HEADS, :] * ((IDX_HEADS ** -0.5) * (IDX_DIM ** -0.5))

    def fold(x):
        return x.reshape(x.shape[0] // 8, 8, Q_BLOCK)

    part = lambda v: jnp.full((8, Q_BLOCK), v, F32)

    def score_step(off, size, carry):
        lo, hi = carry
        ikc = ik_ref[pl.ds(off, size), 0:IDX_DIM].astype(BF16)
        d = jnp.dot(ikc, iq_cat, preferred_element_type=F32)
        s = jnp.maximum(d[:, 0:Q_BLOCK], 0.0) * w[0:1, :]
        for h in range(1, IDX_HEADS):
            s = s + jnp.maximum(d[:, h * Q_BLOCK:(h + 1) * Q_BLOCK], 0.0) * w[h:h + 1, :]
        causal = (off + lax.broadcasted_iota(jnp.int32, (size, Q_BLOCK), 0)) <= qpos
        sc_ref[pl.ds(off, size), :] = jnp.where(causal, s, NEG_INF)
        lo = jnp.minimum(lo, jnp.min(fold(jnp.where(causal, s, BIG)), axis=0))
        hi = jnp.maximum(hi, jnp.max(fold(jnp.where(causal, s, -BIG)), axis=0))
        return lo, hi

    lo0, hi0 = _walk_keys(nch, score_step, (part(BIG), part(-BIG)))
    lo0 = jnp.min(lo0, axis=0, keepdims=True)
    hi0 = jnp.max(hi0, axis=0, keepdims=True)

    kf = float(k_top)
    select = i * Q_BLOCK >= k_top

    def count_ge(mid):
        mid8 = jnp.broadcast_to(mid, (8, Q_BLOCK))[None]

        def body(off0, size, cnts):
            cnts = list(cnts)
            for j in range(size // 64):
                off = pl.multiple_of(off0 + j * 64, 64)
                s = sc_ref[pl.ds(off, 64), :].reshape(8, 8, Q_BLOCK)
                cnts[j % 4] = cnts[j % 4] + jnp.sum(jnp.where(s >= mid8, 1.0, 0.0), axis=0)
            return tuple(cnts)

        cnts = _walk_keys(nch, body, (part(0.0),) * 4)
        return jnp.sum((cnts[0] + cnts[1]) + (cnts[2] + cnts[3]), axis=0, keepdims=True)

    def count_step(_, st):
        lo, hi, clo, chi, moved = st
        mid = lo + (hi - lo) * 0.5
        cnt = count_ge(mid)
        up = cnt >= kf
        return (jnp.where(up, mid, lo), jnp.where(up, hi, mid), jnp.where(up, cnt, clo),
                jnp.where(up, chi, cnt), jnp.where(up, moved, 1.0))

    def stats(mid):
        mid8 = jnp.broadcast_to(mid, (8, Q_BLOCK))[None]

        def body(off0, size, carry):
            cnt, a, b = carry
            for j in range(size // 64):
                off = pl.multiple_of(off0 + j * 64, 64)
                s = sc_ref[pl.ds(off, 64), :].reshape(8, 8, Q_BLOCK)
                gt = s > mid8
                cnt = cnt + jnp.sum(jnp.where(gt, 1.0, 0.0), axis=0)
                a = jnp.minimum(a, jnp.min(jnp.where(gt, s, BIG), axis=0))
                b = jnp.maximum(b, jnp.max(jnp.where(gt, -BIG, s), axis=0))
            return cnt, a, b

        cnt, a, b = _walk_keys(nch, body, (part(0.0), part(BIG), part(-BIG)))
        return (jnp.sum(cnt, axis=0, keepdims=True), jnp.min(a, axis=0, keepdims=True),
                jnp.max(b, axis=0, keepdims=True))

    def step(st):
        lo, hi, clo, chi = st
        mid = jnp.maximum(lo + (hi - lo) * 0.5, lo)
        mid = jnp.where(mid >= hi, lo, mid)
        cnt, a, b = stats(mid)
        up = cnt >= kf
        return (jnp.where(up, a, lo), jnp.where(up, hi, b),
                jnp.where(up, cnt, clo), jnp.where(up, chi, cnt))

    def cond(st):
        return jnp.logical_and(select, jnp.max(st[1] - st[0]) > 0.0)

    n_causal = (qpos + 1).astype(F32)
    zero = jnp.zeros((1, Q_BLOCK), F32)
    lo_v, hi_v, clo, chi, moved = lax.fori_loop(
        0, jnp.where(select, BISECT_COUNT_STEPS, 0), count_step, (lo0, hi0, n_causal, zero, zero))

    def snap_step(off0, size, carry):
        lo, hi = carry
        for j in range(size // 64):
            off = pl.multiple_of(off0 + j * 64, 64)
            s = sc_ref[pl.ds(off, 64), :].reshape(8, 8, Q_BLOCK)
            lo = jnp.minimum(lo, jnp.min(jnp.where(s >= lo_v[None], s, BIG), axis=0))
            hi = jnp.maximum(hi, jnp.max(jnp.where(s < hi_v[None], s, -BIG), axis=0))
        return lo, hi

    lo1, hi1 = _walk_keys(jnp.where(select, nch, 0), snap_step, (part(BIG), part(-BIG)))
    lo1 = jnp.min(lo1, axis=0, keepdims=True)
    hi1 = jnp.where(moved > 0.0, jnp.max(hi1, axis=0, keepdims=True), hi0)
    st = lax.fori_loop(0, jnp.where(select, BISECT_FIXED_STEPS, 0), lambda _, s: step(s),
                       (lo1, hi1, clo, chi))
    lo, _, clo, chi = lax.while_loop(cond, lambda s: step(step(s)), st)
    thr = jnp.where(select, lo, 0.5 * NEG_INF)
    need = jnp.where(select, kf - chi, 0.0)
    ties = jnp.logical_and(select, jnp.max((clo - chi) - need) > 0.0)

    @pl.when(ties)
    def _():
        def mask_chunk(c, tie_carry):
            off = pl.multiple_of(c * C, C)
            s_idx = sc_ref[pl.ds(off, C), :]
            eqf = jnp.where(s_idx == thr, 1.0, 0.0)
            before = jnp.dot(ltri_ref[...], eqf.astype(BF16), preferred_element_type=F32) + tie_carry
            sc_ref[pl.ds(off, C), :] = jnp.where(s_idx > thr, 1.0, jnp.where(before < need, eqf, 0.0))
            return tie_carry + jnp.sum(eqf, axis=0, keepdims=True)

        lax.fori_loop(0, nch, mask_chunk, jnp.zeros((1, Q_BLOCK), F32))

    thr_sel = jnp.where(ties, 0.5, thr)
    qall = jnp.concatenate(_padded_queries(q_ref[...]), axis=1)

    def sel_fn(off, size):
        bias = jnp.where(sc_ref[pl.ds(off, size), :] >= thr_sel, 0.0, MASK_VAL)
        return _lane_tile(bias, N_KV_HEADS * GROUP)

    outs = _masked_attention(nch, sel_fn, k_ref, v_ref, qall, s_ref, acc_ref)
    for kv in range(N_KV_HEADS):
        o = outs[kv]
        for g in range(GROUP):
            hq = kv * GROUP + g
            o_ref[hq * HEAD_DIM:(hq + 1) * HEAD_DIM, :] = o[:, g * Q_BLOCK:(g + 1) * Q_BLOCK].astype(BF16)


def _dsa(iqT, miscT, ik_rows, qT, k_rows, vT128, ltri, B, T):
    nq = T // Q_BLOCK
    k_top = min(DSA_TOPK_MAX, T // 4)
    N = B * T
    return pl.pallas_call(
        functools.partial(_dsa_kernel, k_top=k_top),
        grid=(B, nq),
        in_specs=[
            pl.BlockSpec((IDX_HEADS * IDX_DIM, Q_BLOCK), lambda b, i: (0, b * nq + i)),
            pl.BlockSpec((MISC_ROWS, Q_BLOCK), lambda b, i: (0, b * nq + i)),
            pl.BlockSpec((T, LANE), lambda b, i: (b, 0)),
            pl.BlockSpec((MIX_WIDTH, Q_BLOCK), lambda b, i: (0, b * nq + i)),
            pl.BlockSpec((T, KV_WIDTH), lambda b, i: (b, 0)),
            pl.BlockSpec((nq, KV_WIDTH, LANE), lambda b, i: (b, 0, 0)),
            pl.BlockSpec((KEY_CHUNK, KEY_CHUNK), lambda b, i: (0, 0)),
        ],
        out_specs=pl.BlockSpec((MIX_WIDTH, Q_BLOCK), lambda b, i: (0, b * nq + i)),
        out_shape=jax.ShapeDtypeStruct((MIX_WIDTH, N), BF16),
        scratch_shapes=[
            pltpu.VMEM((T, Q_BLOCK), F32),
            pltpu.VMEM((T, N_KV_HEADS * GROUP * Q_BLOCK), F32),
            pltpu.VMEM((KV_WIDTH + ONES_ROWS, N_KV_HEADS * GROUP * Q_BLOCK), F32),
        ],
        compiler_params=_cparams(("parallel", "arbitrary")),
        name="dsa_attention",
    )(iqT, miscT, ik_rows, qT, k_rows, vT128, ltri)


def _band_kernel(q_ref, k_ref, v_ref, sink_ref, o_ref, *, n_prev, window, use_sink):
    i = pl.program_id(1)
    blk0 = jnp.maximum(i - n_prev, 0)
    nblk = n_prev + 1
    wb = nblk * Q_BLOCK
    kb = k_ref[pl.ds(pl.multiple_of(blk0 * Q_BLOCK, Q_BLOCK), wb), :]
    vb = jnp.concatenate([_v_chunk(v_ref, blk0, nblk), jnp.ones((ONES_ROWS, wb), BF16)], axis=0)
    qall = jnp.concatenate(_padded_queries(q_ref[...]), axis=1)
    rel = ((i - blk0) * Q_BLOCK + lax.broadcasted_iota(jnp.int32, (wb, Q_BLOCK), 1)
           - lax.broadcasted_iota(jnp.int32, (wb, Q_BLOCK), 0))
    bias = jnp.where((rel >= 0) & (rel < window), 0.0, MASK_VAL)
    s = jnp.dot(kb, qall, preferred_element_type=F32) + _lane_tile(bias, N_KV_HEADS * GROUP)
    m = jnp.max(s, axis=0, keepdims=True)
    if use_sink:
        sink = sink_ref[0:1, :]
        m = jnp.maximum(m, sink)
    o = jnp.dot(vb, jnp.exp2(s - m).astype(BF16), preferred_element_type=F32)
    l = o[KV_WIDTH:KV_WIDTH + 1, :]
    if use_sink:
        l = l + jnp.exp2(sink - m)
    inv = 1.0 / l
    half = GROUP * Q_BLOCK
    for kv in range(N_KV_HEADS):
        for g in range(GROUP):
            hq = kv * GROUP + g
            lanes = slice(kv * half + g * Q_BLOCK, kv * half + (g + 1) * Q_BLOCK)
            o_ref[hq * HEAD_DIM:(hq + 1) * HEAD_DIM, :] = (
                o[kv * HEAD_DIM:(kv + 1) * HEAD_DIM, lanes] * inv[:, lanes]).astype(o_ref.dtype)


def _band(qT, q_mixer, k_rows, kv_index, vT128, sink_lanes, window, use_sink, out_dtype, B, T, name):
    nq = T // Q_BLOCK
    N = B * T
    n_prev = (window + Q_BLOCK - 2) // Q_BLOCK
    return pl.pallas_call(
        functools.partial(_band_kernel, n_prev=n_prev, window=window, use_sink=use_sink),
        grid=(B, nq),
        in_specs=[
            pl.BlockSpec((MIX_WIDTH, Q_BLOCK), lambda b, i: (q_mixer, b * nq + i)),
            pl.BlockSpec((T, KV_WIDTH), lambda b, i: (b, kv_index)),
            pl.BlockSpec((nq, KV_WIDTH, LANE), lambda b, i: (b, kv_index, 0)),
            pl.BlockSpec((8, N_Q_HEADS * Q_BLOCK), lambda b, i: (0, 0)),
        ],
        out_specs=pl.BlockSpec((MIX_WIDTH, Q_BLOCK), lambda b, i: (0, b * nq + i)),
        out_shape=jax.ShapeDtypeStruct((MIX_WIDTH, N), out_dtype),
        compiler_params=_cparams(("parallel", "arbitrary")),
        name=name,
    )(qT, k_rows, vT128, sink_lanes)


def _cmp_kernel(xa_ref, xb_ref, pe_ref, w1_ref, w2_ref, o_ref):
    half = (CMP_LEN // 2) * HEAD_DIM
    xa = (xa_ref[...].astype(F32) + pe_ref[0:1, :]).astype(BF16)
    xb = (xb_ref[...].astype(F32) + pe_ref[1:2, :]).astype(BF16)
    hid = (jnp.dot(xa, w1_ref[0:half, :], preferred_element_type=F32)
           + jnp.dot(xb, w1_ref[half:2 * half, :], preferred_element_type=F32))
    hid = jnp.maximum(hid, 0.0).astype(BF16)
    o_ref[...] = jnp.dot(hid, w2_ref[...], preferred_element_type=F32)


def _compress(chunks, pe, w1, w2, name):
    G, nc, half = chunks.shape
    nxt = jnp.concatenate([chunks[:, 1:], jnp.zeros((G, 1, half), chunks.dtype)], axis=1)
    pe2 = jnp.zeros((8, half), F32).at[0:2].set(pe.reshape(2, half))
    return pl.pallas_call(
        _cmp_kernel,
        grid=(G,),
        in_specs=[
            pl.BlockSpec((None, nc, half), lambda g: (g, 0, 0)),
            pl.BlockSpec((None, nc, half), lambda g: (g, 0, 0)),
            pl.BlockSpec((8, half), lambda g: (0, 0)),
            pl.BlockSpec((2 * half, CMP_HIDDEN), lambda g: (0, 0)),
            pl.BlockSpec((CMP_HIDDEN, HEAD_DIM), lambda g: (0, 0)),
        ],
        out_specs=pl.BlockSpec((None, nc, HEAD_DIM), lambda g: (g, 0, 0)),
        out_shape=jax.ShapeDtypeStruct((G, nc, HEAD_DIM), F32),
        compiler_params=_cparams(("parallel",)),
        name=name,
    )(chunks, nxt, pe2, w1.astype(BF16), w2.astype(BF16))


def _nsa_kernel(q_ref, kc_ref, vc_ref, ks_ref, vs_ref, misc_ref, ow_ref, ov_ref, e_ref, o_ref,
                s_ref, acc_ref, *, n_top):
    C = KEY_CHUNK
    i = pl.program_id(1)
    nch = i // (C // Q_BLOCK) + 1
    ncmp = kc_ref.shape[0]
    nsel = ov_ref.shape[0]
    half = GROUP * Q_BLOCK
    t = i * Q_BLOCK + lax.broadcasted_iota(jnp.int32, (1, Q_BLOCK), 1)
    crow = lax.broadcasted_iota(jnp.int32, (ncmp, Q_BLOCK), 0)
    jrow = lax.broadcasted_iota(jnp.int32, (nsel, Q_BLOCK), 0)
    cur = t // SLC_LEN
    qall = jnp.concatenate(_padded_queries(q_ref[...]), axis=1)
    ov = ov_ref[...]

    cmp_bias = jnp.where(crow * CMP_STRIDE + (CMP_LEN - 1) <= t, 0.0, MASK_VAL)
    s = jnp.dot(kc_ref[...], qall, preferred_element_type=F32) + _lane_tile(cmp_bias, N_KV_HEADS * GROUP)
    m = jnp.maximum(jnp.max(s, axis=0, keepdims=True), M_INIT)
    p_all = jnp.exp2(s - m)
    l = jnp.sum(p_all, axis=0, keepdims=True)
    p_all = p_all * jnp.where(l > 0.0, 1.0 / l, 0.0)
    o_cmp_all = jnp.dot(vc_ref[...], p_all.astype(BF16), preferred_element_type=F32)

    o_cmps, selbs = [], []
    for kv in range(N_KV_HEADS):
        o_cmp = o_cmp_all[kv * HEAD_DIM:(kv + 1) * HEAD_DIM, kv * half:(kv + 1) * half]
        pg = p_all[:, kv * half:kv * half + Q_BLOCK]
        for g in range(1, GROUP):
            pg = pg + p_all[:, kv * half + g * Q_BLOCK:kv * half + (g + 1) * Q_BLOCK]
        p_hi = pg.astype(BF16)
        p_lo = (pg - p_hi.astype(F32)).astype(BF16)
        imp = (jnp.dot(ov, p_hi, preferred_element_type=F32)
               + jnp.dot(ov, p_lo, preferred_element_type=F32))
        forced = (jrow == 0) | (jrow == cur) | (jrow == cur - 1)
        imp = jnp.where(forced, FORCE_SCORE, imp)
        imp = jnp.where(jrow * SLC_LEN <= t, imp, NEG_INF)
        tiles = [imp[r0:r0 + 8] for r0 in range(0, nsel, 8)]
        ranks = [jnp.zeros((8, Q_BLOCK), F32) for _ in tiles]
        sub = lax.broadcasted_iota(jnp.int32, (8, Q_BLOCK), 0)
        for ii in range(nsel):
            r = jnp.broadcast_to(imp[ii:ii + 1], (8, Q_BLOCK))
            for jt, tile in enumerate(tiles):
                ge = jnp.where(r >= tile, 1.0, 0.0)
                gt = jnp.where(r > tile, 1.0, 0.0)
                if jt > ii // 8:
                    ahead = ge
                elif jt < ii // 8:
                    ahead = gt
                else:
                    ahead = jnp.where(sub > ii % 8, ge, gt)
                ranks[jt] = ranks[jt] + ahead
        rank = jnp.concatenate(ranks, axis=0)
        selbs.append(jnp.where(rank < float(n_top), 1.0, 0.0).astype(BF16))
        o_cmps.append(o_cmp)

    selb_all = jnp.concatenate(selbs, axis=1)

    def sel_fn(off, size):
        mk = jnp.dot(e_ref[pl.ds(off, size), :], selb_all, preferred_element_type=F32)
        causal = (off + lax.broadcasted_iota(jnp.int32, (size, Q_BLOCK), 0)) <= t
        parts = []
        for kv in range(N_KV_HEADS):
            keep = jnp.where(causal, mk[:, kv * Q_BLOCK:(kv + 1) * Q_BLOCK], 0.0) > 0.5
            parts.append(_lane_tile(jnp.where(keep, 0.0, MASK_VAL), GROUP))
        return jnp.concatenate(parts, axis=1)

    o_slcs = _masked_attention(nch, sel_fn, ks_ref, vs_ref, qall, s_ref, acc_ref)

    for kv in range(N_KV_HEADS):
        for g in range(GROUP):
            hq = kv * GROUP + g
            gr = GATE_ROW0 + hq * 3
            gate = [1.0 / (1.0 + jnp.exp(-misc_ref[gr + br:gr + br + 1, :])) for br in range(3)]
            out = (gate[0] * o_cmps[kv][:, g * Q_BLOCK:(g + 1) * Q_BLOCK]
                   + gate[1] * o_slcs[kv][:, g * Q_BLOCK:(g + 1) * Q_BLOCK]
                   + gate[2] * ow_ref[hq * HEAD_DIM:(hq + 1) * HEAD_DIM, :])
            o_ref[hq * HEAD_DIM:(hq + 1) * HEAD_DIM, :] = out.astype(BF16)


def _nsa(qT, kcmp_rows, vcmpT, k_rows, vT128, miscT, o_winT, ovT, emat, B, T):
    nq = T // Q_BLOCK
    N = B * T
    ncmp = T // CMP_STRIDE
    nsel = T // SLC_LEN
    n_top = min(SLC_TOPN, nsel)
    return pl.pallas_call(
        functools.partial(_nsa_kernel, n_top=n_top),
        grid=(B, nq),
        in_specs=[
            pl.BlockSpec((MIX_WIDTH, Q_BLOCK), lambda b, i: (2, b * nq + i)),
            pl.BlockSpec((None, ncmp, KV_WIDTH), lambda b, i: (b, 0, 0)),
            pl.BlockSpec((None, KV_WIDTH, ncmp), lambda b, i: (b, 0, 0)),
            pl.BlockSpec((T, KV_WIDTH), lambda b, i: (b, 3)),
            pl.BlockSpec((nq, KV_WIDTH, LANE), lambda b, i: (b, 3, 0)),
            pl.BlockSpec((MISC_ROWS, Q_BLOCK), lambda b, i: (0, b * nq + i)),
            pl.BlockSpec((MIX_WIDTH, Q_BLOCK), lambda b, i: (0, b * nq + i)),
            pl.BlockSpec((nsel, ncmp), lambda b, i: (0, 0)),
            pl.BlockSpec((T, nsel), lambda b, i: (0, 0)),
        ],
        out_specs=pl.BlockSpec((MIX_WIDTH, Q_BLOCK), lambda b, i: (0, b * nq + i)),
        out_shape=jax.ShapeDtypeStruct((MIX_WIDTH, N), BF16),
        scratch_shapes=[
            pltpu.VMEM((T, N_KV_HEADS * GROUP * Q_BLOCK), F32),
            pltpu.VMEM((KV_WIDTH + ONES_ROWS, N_KV_HEADS * GROUP * Q_BLOCK), F32),
        ],
        compiler_params=_cparams(("parallel", "arbitrary")),
        name="nsa_attention",
    )(qT, kcmp_rows, vcmpT, k_rows, vT128, miscT, o_winT, ovT, emat)


def _merge_kernel(x_ref, g_ref, wg_ref, oa_ref, ob_ref, oc_ref, wb_ref, wo_ref, out_ref, *, token_major_in):
    x = x_ref[...].T if token_major_in else x_ref[...]
    tm = x.shape[1]
    ss = jnp.sum(x * x, axis=0, keepdims=True)
    h = (x * lax.rsqrt(ss * (1.0 / D_MODEL) + EPS) * _lane_tile(g_ref[...], tm // LANE)).astype(BF16)
    acc = None
    for n, o_ref in enumerate((oa_ref, ob_ref, oc_ref)):
        gm = jnp.dot(wg_ref[n * D_MODEL:(n + 1) * D_MODEL, :], h, preferred_element_type=F32)
        y = jnp.dot(wb_ref[n], o_ref[...], preferred_element_type=F32)
        term = y * (1.0 / (1.0 + jnp.exp(-gm)))
        acc = term if acc is None else acc + term
    out_ref[...] = x + jnp.dot(wo_ref[...], acc.astype(BF16), preferred_element_type=F32)


def _merge(xT, token_major_in, g_b, wgT, oa, ob, oc, wbT, woT):
    N = xT.shape[0] if token_major_in else xT.shape[1]
    tm = 512
    col = lambda i: (0, i)
    return pl.pallas_call(
        functools.partial(_merge_kernel, token_major_in=token_major_in),
        grid=(N // tm,),
        in_specs=[
            _x_spec(tm, token_major_in),
            pl.BlockSpec((D_MODEL, LANE), lambda i: (0, 0)),
            pl.BlockSpec((N_MIXERS * D_MODEL, D_MODEL), lambda i: (0, 0)),
            pl.BlockSpec((MIX_WIDTH, tm), col),
            pl.BlockSpec((MIX_WIDTH, tm), col),
            pl.BlockSpec((MIX_WIDTH, tm), col),
            pl.BlockSpec((N_MIXERS, D_MODEL, MIX_WIDTH), lambda i: (0, 0, 0)),
            pl.BlockSpec((D_MODEL, D_MODEL), lambda i: (0, 0)),
        ],
        out_specs=pl.BlockSpec((D_MODEL, tm), col),
        out_shape=jax.ShapeDtypeStruct((D_MODEL, N), F32),
        compiler_params=_cparams(("parallel",)),
        name="merge_out_projection",
    )(xT, g_b, wgT, oa, ob, oc, wbT, woT)


def _mlp_kernel(x_ref, g_ref, wu_ref, wd_ref, out_ref, h_ref, acc_ref, *, token_major_out):
    f = pl.program_id(1)
    tm = x_ref.shape[1]

    @pl.when(f == 0)
    def _():
        x = x_ref[...]
        ss = jnp.sum(x * x, axis=0, keepdims=True)
        h_ref[...] = (x * lax.rsqrt(ss * (1.0 / D_MODEL) + EPS)
                      * _lane_tile(g_ref[...], tm // LANE)).astype(BF16)
        acc_ref[...] = jnp.zeros(acc_ref.shape, F32)

    u = jnp.maximum(jnp.dot(wu_ref[...], h_ref[...], preferred_element_type=F32), 0.0)
    acc_ref[...] += jnp.dot(wd_ref[...], (u * u).astype(BF16), preferred_element_type=F32)

    @pl.when(f == pl.num_programs(1) - 1)
    def _():
        y = x_ref[...] + acc_ref[...]
        out_ref[...] = y.T if token_major_out else y


def _mlp(xT, g_b, wuT, wdT, token_major_out):
    N = xT.shape[1]
    tm, tf = 1024, 1024
    if token_major_out:
        out_spec = pl.BlockSpec((tm, D_MODEL), lambda i, f: (i, 0))
        out_shape = jax.ShapeDtypeStruct((N, D_MODEL), F32)
    else:
        out_spec = pl.BlockSpec((D_MODEL, tm), lambda i, f: (0, i))
        out_shape = jax.ShapeDtypeStruct((D_MODEL, N), F32)
    return pl.pallas_call(
        functools.partial(_mlp_kernel, token_major_out=token_major_out),
        grid=(N // tm, D_FF // tf),
        in_specs=[
            pl.BlockSpec((D_MODEL, tm), lambda i, f: (0, i)),
            pl.BlockSpec((D_MODEL, LANE), lambda i, f: (0, 0)),
            pl.BlockSpec((tf, D_MODEL), lambda i, f: (f, 0)),
            pl.BlockSpec((D_MODEL, tf), lambda i, f: (0, f)),
        ],
        out_specs=out_spec,
        out_shape=out_shape,
        scratch_shapes=[pltpu.VMEM((D_MODEL, tm), BF16), pltpu.VMEM((D_MODEL, tm), F32)],
        compiler_params=_cparams(("parallel", "arbitrary")),
        name="relu2_mlp",
    )(xT, g_b, wuT, wdT)


def _to_chunks(rows, B, T):
    half_tokens = CMP_LEN // 2
    t = rows.reshape(B, T, N_KV_HEADS, HEAD_DIM).transpose(0, 2, 1, 3)
    return t.reshape(B * N_KV_HEADS, T // half_tokens, half_tokens * HEAD_DIM)


def kernel(x, norm_mix, w_in, q_norm, k_norm, sinks, cmp_pe_k, cmp_pe_v, w_ck1, w_ck2, w_cv1, w_cv2,
           w_branch, w_out, norm_mlp, w_up, w_down):
    B, T, D = x.shape
    N = B * T
    depth = w_in.shape[0]
    nq = T // Q_BLOCK
    ncmp = T // CMP_STRIDE
    nsel = T // SLC_LEN

    inv_freq = ROPE_THETA ** (-jnp.arange(0, HEAD_DIM, 2, dtype=F32) / HEAD_DIM)
    ang = jnp.arange(T, dtype=F32)[:, None] * inv_freq[None, :]
    cosT, sinT = jnp.cos(ang).T, jnp.sin(ang).T

    ci = np.arange(KEY_CHUNK)
    ltri = jnp.asarray(ci[None, :] < ci[:, None], BF16)
    emat = jnp.asarray((np.arange(T)[:, None] // SLC_LEN) == np.arange(nsel)[None, :], BF16)
    cstart = np.arange(ncmp) * CMP_STRIDE
    sstart = np.arange(nsel) * SLC_LEN
    n_cmp_valid = (T - CMP_LEN) // CMP_STRIDE + 1
    ov = ((cstart[None, :] < sstart[:, None] + SLC_LEN) & (cstart[None, :] + CMP_LEN > sstart[:, None])
          & (np.arange(ncmp)[None, :] < n_cmp_valid))
    ovT = jnp.asarray(ov, BF16)

    xT = x.reshape(N, D)

    sl = lambda s: w_in[:, :, s[0]:s[1]]
    wT_all = jnp.concatenate(
        [sl(S_QA), sl(S_QB), sl(S_QC),
         sl(S_KA), sl(S_KB), sl(S_KC), sl(S_KSL), sl(S_KWN),
         sl(S_VA), sl(S_VB), sl(S_VC), sl(S_VSL), sl(S_VWN),
         sl(S_IQ), sl(S_IK), sl(S_IW), jnp.zeros((depth, D, GATE_ROW0 - IDX_HEADS), w_in.dtype), sl(S_GC)],
        axis=2).transpose(0, 2, 1).astype(BF16)
    wgT_all = sl(S_GM).transpose(0, 2, 1).astype(BF16)
    wbT_all = w_branch.transpose(0, 1, 3, 2).astype(BF16)
    woT_all = w_out.transpose(0, 2, 1).astype(BF16)
    wuT_all = w_up.transpose(0, 2, 1).astype(BF16)
    wdT_all = w_down.transpose(0, 2, 1).astype(BF16)
    lane_b = lambda v: jnp.broadcast_to(v.astype(F32)[..., None], v.shape + (LANE,))
    gmix_all, gmlp_all = lane_b(norm_mix), lane_b(norm_mlp)
    gq_all = lane_b(jnp.tile(q_norm[:, :, None, :], (1, 1, N_Q_HEADS, 1)).reshape(depth, -1))
    gk_all = lane_b(jnp.tile(k_norm[:, jnp.array([0, 1, 2, 2, 2])][:, :, None, :],
                             (1, 1, N_KV_HEADS, 1)).reshape(depth, -1))
    sink_all = jnp.broadcast_to(jnp.repeat(sinks.astype(F32) * LOG2E, Q_BLOCK, axis=1)[:, None, :],
                                (depth, 8, N_Q_HEADS * Q_BLOCK))

    for l in range(depth):
        wgT, sink_lanes = wgT_all[l], sink_all[l]
        qT, k_rows, vT128, iqT, ik_rows, miscT = _project(
            xT, l == 0, gmix_all[l], wT_all[l], gq_all[l], gk_all[l], cosT, sinT, T)

        o_a = _dsa(iqT, miscT, ik_rows, qT, k_rows, vT128, ltri, B, T)
        o_b = _band(qT, 1, k_rows, 1, vT128, sink_lanes, SWA_WINDOW, True, BF16, B, T, "swa_attention")
        o_w = _band(qT, 2, k_rows, 4, vT128, sink_lanes, NSA_WINDOW, False, F32, B, T, "nsa_window_attention")

        kc_chunks = _to_chunks(k_rows[:, 2 * KV_WIDTH:3 * KV_WIDTH].reshape(B, T, KV_WIDTH), B, T)
        vc_rows = vT128[:, 2 * KV_WIDTH:3 * KV_WIDTH, :].reshape(B, nq, KV_WIDTH, LANE)
        vc_rows = vc_rows.transpose(0, 1, 3, 2).reshape(B, T, KV_WIDTH)
        vc_chunks = _to_chunks(vc_rows, B, T)
        k_cmp = _compress(kc_chunks, cmp_pe_k[l], w_ck1[l], w_ck2[l], "compress_k")
        v_cmp = _compress(vc_chunks, cmp_pe_v[l], w_cv1[l], w_cv2[l], "compress_v")
        kcmp_rows = k_cmp.reshape(B, N_KV_HEADS, ncmp, HEAD_DIM).transpose(0, 2, 1, 3)
        kcmp_rows = kcmp_rows.reshape(B, ncmp, KV_WIDTH).astype(BF16)
        vcmpT = v_cmp.reshape(B, N_KV_HEADS, ncmp, HEAD_DIM).transpose(0, 1, 3, 2)
        vcmpT = vcmpT.reshape(B, KV_WIDTH, ncmp).astype(BF16)

        o_c = _nsa(qT, kcmp_rows, vcmpT, k_rows, vT128, miscT, o_w, ovT, emat, B, T)

        xT = _merge(xT, l == 0, gmix_all[l], wgT, o_a, o_b, o_c, wbT_all[l], woT_all[l])
        xT = _mlp(xT, gmlp_all[l], wuT_all[l], wdT_all[l], token_major_out=(l == depth - 1))

    return xT.reshape(B, T, D)
```

```python
import functools

import numpy as np
import jax
import jax.numpy as jnp
from jax import lax
from jax.experimental import pallas as pl
from jax.experimental.pallas import tpu as pltpu

F32 = jnp.float32
BF16 = jnp.bfloat16

D_MODEL = 1024
HEAD_DIM = 64
HALF = HEAD_DIM // 2
N_Q_HEADS = 8
N_KV_HEADS = 2
GROUP = N_Q_HEADS // N_KV_HEADS
MIX_WIDTH = N_Q_HEADS * HEAD_DIM
KV_WIDTH = N_KV_HEADS * HEAD_DIM
N_MIXERS = 3
D_FF = 4 * D_MODEL
ROPE_THETA = 10000.0
EPS = 1e-6
NEG_INF = -1e30
MASK_VAL = -2e30
M_INIT = -1e30
BIG = 3e38
Q_BLOCK = 128
KEY_CHUNK = 512
BISECT_COUNT_STEPS = 14
BISECT_FIXED_STEPS = 2
ONES_ROWS = 16
IDX_HEADS = 4
IDX_DIM = 64
DSA_TOPK_MAX = 256
SWA_WINDOW = 128
CMP_LEN = 32
CMP_STRIDE = 16
CMP_HIDDEN = 256
SLC_LEN = 64
SLC_TOPN = 16
NSA_WINDOW = 512
FORCE_SCORE = 1e9
LOG2E = 1.4426950408889634
Q_SCALE = (HEAD_DIM ** -0.5) * LOG2E
LANE = 128
LANE_SHIFT = 7
KEY_STEPS = (2048, 1024, 512)
VMEM_LIMIT = 56 * 1024 * 1024

IN_SIZES = (
    MIX_WIDTH, KV_WIDTH, KV_WIDTH, IDX_HEADS * IDX_DIM, IDX_DIM, IDX_HEADS,
    MIX_WIDTH, KV_WIDTH, KV_WIDTH,
    MIX_WIDTH, KV_WIDTH, KV_WIDTH, KV_WIDTH, KV_WIDTH, KV_WIDTH, KV_WIDTH, 3 * N_Q_HEADS,
    N_MIXERS * D_MODEL,
)
_OFF = np.concatenate([[0], np.cumsum(IN_SIZES)]).astype(int)
(S_QA, S_KA, S_VA, S_IQ, S_IK, S_IW, S_QB, S_KB, S_VB,
 S_QC, S_KC, S_VC, S_KSL, S_VSL, S_KWN, S_VWN, S_GC, S_GM) = [
    (int(_OFF[i]), int(_OFF[i + 1])) for i in range(len(IN_SIZES))]

R_Q = 0
R_K = R_Q + N_MIXERS * MIX_WIDTH
R_V = R_K + 5 * KV_WIDTH
R_IQ = R_V + 5 * KV_WIDTH
R_IK = R_IQ + IDX_HEADS * IDX_DIM
R_MISC = R_IK + IDX_DIM
MISC_ROWS = 32
GATE_ROW0 = 8
R_END = R_MISC + MISC_ROWS
PROJ_TM = 1024


def _cparams(sem):
    return pltpu.CompilerParams(dimension_semantics=sem, vmem_limit_bytes=VMEM_LIMIT)


def _lane_tile(a, n):
    return a if n == 1 else jnp.concatenate([a] * n, axis=1)


def _proj_kernel(x_ref, g_ref, w_ref, gq_ref, gk_ref, cos_ref, sin_ref,
                 q_ref, k_ref, v_ref, iq_ref, ik_ref, misc_ref, *, token_major_in):
    x = x_ref[...].T if token_major_in else x_ref[...]
    tm = x.shape[1]
    rep = tm // LANE
    ss = jnp.sum(x * x, axis=0, keepdims=True)
    h = (x * lax.rsqrt(ss * (1.0 / D_MODEL) + EPS) * _lane_tile(g_ref[...], rep)).astype(BF16)
    cos = cos_ref[...]
    sin = sin_ref[...]

    def rope(z):
        z1, z2 = z[:HALF], z[HALF:]
        return jnp.concatenate([z1 * cos - z2 * sin, z2 * cos + z1 * sin], axis=0)

    def head_norm(z, gain):
        ssq = jnp.sum(z * z, axis=0, keepdims=True)
        return z * lax.rsqrt(ssq * (1.0 / HEAD_DIM) + EPS) * _lane_tile(gain, rep)

    for m in range(N_MIXERS):
        z = jnp.dot(w_ref[R_Q + m * MIX_WIDTH:R_Q + (m + 1) * MIX_WIDTH, :], h,
                    preferred_element_type=F32)
        for hh in range(N_Q_HEADS):
            r0 = m * MIX_WIDTH + hh * HEAD_DIM
            zh = head_norm(z[hh * HEAD_DIM:(hh + 1) * HEAD_DIM], gq_ref[r0:r0 + HEAD_DIM, :])
            q_ref[r0:r0 + HEAD_DIM, :] = (rope(zh) * Q_SCALE).astype(BF16)

    z = jnp.dot(w_ref[R_K:R_V, :], h, preferred_element_type=F32)
    for mk in range(5):
        heads = []
        for kvh in range(N_KV_HEADS):
            r0 = mk * KV_WIDTH + kvh * HEAD_DIM
            heads.append(rope(head_norm(z[r0:r0 + HEAD_DIM], gk_ref[r0:r0 + HEAD_DIM, :])))
        k_ref[:, mk * KV_WIDTH:(mk + 1) * KV_WIDTH] = jnp.concatenate(heads, axis=0).T.astype(BF16)

    z = jnp.dot(w_ref[R_V:R_IQ, :], h, preferred_element_type=F32)
    for j in range(rep):
        v_ref[j] = z[:, j * LANE:(j + 1) * LANE].astype(BF16)

    z = jnp.dot(w_ref[R_IQ:R_END, :], h, preferred_element_type=F32)
    for hh in range(IDX_HEADS):
        r0 = hh * IDX_DIM
        iq_ref[r0:r0 + IDX_DIM, :] = rope(z[r0:r0 + IDX_DIM])
    ik = rope(z[R_IK - R_IQ:R_MISC - R_IQ])
    ik_ref[...] = jnp.concatenate([ik, jnp.zeros((LANE - IDX_DIM, tm), F32)], axis=0).T
    misc_ref[...] = z[R_MISC - R_IQ:R_END - R_IQ]


def _x_spec(tm, token_major):
    if token_major:
        return pl.BlockSpec((tm, D_MODEL), lambda i: (i, 0))
    return pl.BlockSpec((D_MODEL, tm), lambda i: (0, i))


def _project(xs, token_major_in, g_b, wT, gq_b, gk_b, cosT, sinT, T):
    N = xs.shape[0] if token_major_in else xs.shape[1]
    xT = xs
    tm = PROJ_TM
    tpb = T // tm
    rep = tm // LANE
    col = lambda i: (0, i)
    const = lambda i: (0, 0)
    return pl.pallas_call(
        functools.partial(_proj_kernel, token_major_in=token_major_in),
        grid=(N // tm,),
        in_specs=[
            _x_spec(tm, token_major_in),
            pl.BlockSpec((D_MODEL, LANE), const),
            pl.BlockSpec((R_END, D_MODEL), const),
            pl.BlockSpec((N_MIXERS * MIX_WIDTH, LANE), const),
            pl.BlockSpec((5 * KV_WIDTH, LANE), const),
            pl.BlockSpec((HALF, tm), lambda i: (0, i % tpb)),
            pl.BlockSpec((HALF, tm), lambda i: (0, i % tpb)),
        ],
        out_specs=[
            pl.BlockSpec((N_MIXERS * MIX_WIDTH, tm), col),
            pl.BlockSpec((tm, 5 * KV_WIDTH), lambda i: (i, 0)),
            pl.BlockSpec((rep, 5 * KV_WIDTH, LANE), lambda i: (i, 0, 0)),
            pl.BlockSpec((IDX_HEADS * IDX_DIM, tm), col),
            pl.BlockSpec((tm, LANE), lambda i: (i, 0)),
            pl.BlockSpec((MISC_ROWS, tm), col),
        ],
        out_shape=[
            jax.ShapeDtypeStruct((N_MIXERS * MIX_WIDTH, N), BF16),
            jax.ShapeDtypeStruct((N, 5 * KV_WIDTH), BF16),
            jax.ShapeDtypeStruct((N // LANE, 5 * KV_WIDTH, LANE), BF16),
            jax.ShapeDtypeStruct((IDX_HEADS * IDX_DIM, N), F32),
            jax.ShapeDtypeStruct((N, LANE), F32),
            jax.ShapeDtypeStruct((MISC_ROWS, N), F32),
        ],
        compiler_params=_cparams(("parallel",)),
        name="in_projection",
    )(xT, g_b, wT, gq_b, gk_b, cosT, sinT)


def _padded_queries(q):
    zeros = jnp.zeros((HEAD_DIM, Q_BLOCK), q.dtype)
    out = []
    for kv in range(N_KV_HEADS):
        cols = []
        for g in range(GROUP):
            hq = kv * GROUP + g
            qh = q[hq * HEAD_DIM:(hq + 1) * HEAD_DIM, :]
            cols.append(jnp.concatenate([qh, zeros] if kv == 0 else [zeros, qh], axis=0))
        out.append(jnp.concatenate(cols, axis=1))
    return out


def _v_chunk(v_ref, blk0, nblk):
    return jnp.concatenate([v_ref[blk0 + j] for j in range(nblk)], axis=1)


def _walk_keys(nch, fn, carry):
    done = 0
    for size in KEY_STEPS:
        per = size // KEY_CHUNK
        n = (nch - done) // per
        base = done * KEY_CHUNK
        carry = lax.fori_loop(
            0, n, lambda c, x, base=base, size=size: fn(pl.multiple_of(base + c * size, KEY_CHUNK), size, x),
            carry)
        done = done + n * per
    return carry


def _masked_attention(nch, sel_fn, k_ref, v_ref, qall, s_ref, acc_ref):
    lanes = N_KV_HEADS * GROUP * Q_BLOCK
    walk = functools.partial(_walk_keys, nch)

    def qk_step(off, size, mx):
        s = jnp.dot(k_ref[pl.ds(off, size), :], qall, preferred_element_type=F32) + sel_fn(off, size)
        s_ref[pl.ds(off, size), :] = s
        return jnp.maximum(mx, jnp.max(s, axis=0, keepdims=True))

    mx = walk(qk_step, jnp.full((1, lanes), M_INIT, F32))
    acc_ref[...] = jnp.zeros(acc_ref.shape, F32)

    def pv_step(off, size, carry):
        p = jnp.exp2(s_ref[pl.ds(off, size), :] - mx).astype(BF16)
        vc = jnp.concatenate([_v_chunk(v_ref, lax.shift_right_logical(off, LANE_SHIFT), size // LANE),
                              jnp.ones((ONES_ROWS, size), BF16)], axis=0)
        acc_ref[...] += jnp.dot(vc, p, preferred_element_type=F32)
        return carry

    walk(pv_step, 0)
    l = acc_ref[KV_WIDTH:KV_WIDTH + 1, :]
    inv = jnp.where(l > 0.0, 1.0 / l, 0.0)
    half = GROUP * Q_BLOCK
    return [acc_ref[kv * HEAD_DIM:(kv + 1) * HEAD_DIM, kv * half:(kv + 1) * half]
            * inv[:, kv * half:(kv + 1) * half] for kv in range(N_KV_HEADS)]


def _dsa_kernel(iq_ref, misc_ref, ik_ref, q_ref, k_ref, v_ref, ltri_ref, o_ref,
                sc_ref, s_ref, acc_ref, *, k_top):
    C = KEY_CHUNK
    i = pl.program_id(1)
    nch = i // (C // Q_BLOCK) + 1
    qpos = i * Q_BLOCK + lax.broadcasted_iota(jnp.int32, (1, Q_BLOCK), 1)

    iq = iq_ref[...].astype(BF16)
    iq_cat = jnp.concatenate([iq[h * IDX_DIM:(h + 1) * IDX_DIM] for h in range(IDX_HEADS)], axis=1)
    w = misc_ref[0:IDX_HEADS, :] * ((IDX_HEADS ** -0.5) * (IDX_DIM ** -0.5))

    def fold(x):
        return x.reshape(x.shape[0] // 8, 8, Q_BLOCK)

    part = lambda v: jnp.full((8, Q_BLOCK), v, F32)

    def score_step(off, size, carry):
        lo, hi = carry
        ikc = ik_ref[pl.ds(off, size), 0:IDX_DIM].astype(BF16)
        d = jnp.dot(ikc, iq_cat, preferred_element_type=F32)
        s = jnp.maximum(d[:, 0:Q_BLOCK], 0.0) * w[0:1, :]
        for h in range(1, IDX_HEADS):
            s = s + jnp.maximum(d[:, h * Q_BLOCK:(h + 1) * Q_BLOCK], 0.0) * w[h:h + 1, :]
        causal = (off + lax.broadcasted_iota(jnp.int32, (size, Q_BLOCK), 0)) <= qpos
        sc_ref[pl.ds(off, size), :] = jnp.where(causal, s, NEG_INF)
        lo = jnp.minimum(lo, jnp.min(fold(jnp.where(causal, s, BIG)), axis=0))
        hi = jnp.maximum(hi, jnp.max(fold(jnp.where(causal, s, -BIG)), axis=0))
        return lo, hi

    lo0, hi0 = _walk_keys(nch, score_step, (part(BIG), part(-BIG)))
    lo0 = jnp.min(lo0, axis=0, keepdims=True)
    hi0 = jnp.max(hi0, axis=0, keepdims=True)

    kf = float(k_top)
    select = i * Q_BLOCK >= k_top

    def count_ge(mid):
        mid8 = jnp.broadcast_to(mid, (8, Q_BLOCK))[None]

        def body(off0, size, cnts):
            cnts = list(cnts)
            for j in range(size // 64):
                off = pl.multiple_of(off0 + j * 64, 64)
                s = sc_ref[pl.ds(off, 64), :].reshape(8, 8, Q_BLOCK)
                cnts[j % 4] = cnts[j % 4] + jnp.sum(jnp.where(s >= mid8, 1.0, 0.0), axis=0)
            return tuple(cnts)

        cnts = _walk_keys(nch, body, (part(0.0),) * 4)
        return jnp.sum((cnts[0] + cnts[1]) + (cnts[2] + cnts[3]), axis=0, keepdims=True)

    def count_step(_, st):
        lo, hi, clo, chi, moved = st
        mid = lo + (hi - lo) * 0.5
        cnt = count_ge(mid)
        up = cnt >= kf
        return (jnp.where(up, mid, lo), jnp.where(up, hi, mid), jnp.where(up, cnt, clo),
                jnp.where(up, chi, cnt), jnp.where(up, moved, 1.0))

    def stats(mid):
        mid8 = jnp.broadcast_to(mid, (8, Q_BLOCK))[None]

        def body(off0, size, carry):
            cnt, a, b = carry
            for j in range(size // 64):
                off = pl.multiple_of(off0 + j * 64, 64)
                s = sc_ref[pl.ds(off, 64), :].reshape(8, 8, Q_BLOCK)
                gt = s > mid8
                cnt = cnt + jnp.sum(jnp.where(gt, 1.0, 0.0), axis=0)
                a = jnp.minimum(a, jnp.min(jnp.where(gt, s, BIG), axis=0))
                b = jnp.maximum(b, jnp.max(jnp.where(gt, -BIG, s), axis=0))
            return cnt, a, b

        cnt, a, b = _walk_keys(nch, body, (part(0.0), part(BIG), part(-BIG)))
        return (jnp.sum(cnt, axis=0, keepdims=True), jnp.min(a, axis=0, keepdims=True),
                jnp.max(b, axis=0, keepdims=True))

    def step(st):
        lo, hi, clo, chi = st
        mid = jnp.maximum(lo + (hi - lo) * 0.5, lo)
        mid = jnp.where(mid >= hi, lo, mid)
        cnt, a, b = stats(mid)
        up = cnt >= kf
        return (jnp.where(up, a, lo), jnp.where(up, hi, b),
                jnp.where(up, cnt, clo), jnp.where(up, chi, cnt))

    def cond(st):
        return jnp.logical_and(select, jnp.max(st[1] - st[0]) > 0.0)

    n_causal = (qpos + 1).astype(F32)
    zero = jnp.zeros((1, Q_BLOCK), F32)
    lo_v, hi_v, clo, chi, moved = lax.fori_loop(
        0, jnp.where(select, BISECT_COUNT_STEPS, 0), count_step, (lo0, hi0, n_causal, zero, zero))

    def snap_step(off0, size, carry):
        lo, hi = carry
        for j in range(size // 64):
            off = pl.multiple_of(off0 + j * 64, 64)
            s = sc_ref[pl.ds(off, 64), :].reshape(8, 8, Q_BLOCK)
            lo = jnp.minimum(lo, jnp.min(jnp.where(s >= lo_v[None], s, BIG), axis=0))
            hi = jnp.maximum(hi, jnp.max(jnp.where(s < hi_v[None], s, -BIG), axis=0))
        return lo, hi

    lo1, hi1 = _walk_keys(jnp.where(select, nch, 0), snap_step, (part(BIG), part(-BIG)))
    lo1 = jnp.min(lo1, axis=0, keepdims=True)
    hi1 = jnp.where(moved > 0.0, jnp.max(hi1, axis=0, keepdims=True), hi0)
    st = lax.fori_loop(0, jnp.where(select, BISECT_FIXED_STEPS, 0), lambda _, s: step(s),
                       (lo1, hi1, clo, chi))
    lo, _, clo, chi = lax.while_loop(cond, lambda s: step(step(s)), st)
    thr = jnp.where(select, lo, 0.5 * NEG_INF)
    need = jnp.where(select, kf - chi, 0.0)
    ties = jnp.logical_and(select, jnp.max((clo - chi) - need) > 0.0)

    @pl.when(ties)
    def _():
        def mask_chunk(c, tie_carry):
            off = pl.multiple_of(c * C, C)
            s_idx = sc_ref[pl.ds(off, C), :]
            eqf = jnp.where(s_idx == thr, 1.0, 0.0)
            before = jnp.dot(ltri_ref[...], eqf.astype(BF16), preferred_element_type=F32) + tie_carry
            sc_ref[pl.ds(off, C), :] = jnp.where(s_idx > thr, 1.0, jnp.where(before < need, eqf, 0.0))
            return tie_carry + jnp.sum(eqf, axis=0, keepdims=True)

        lax.fori_loop(0, nch, mask_chunk, jnp.zeros((1, Q_BLOCK), F32))

    thr_sel = jnp.where(ties, 0.5, thr)
    qall = jnp.concatenate(_padded_queries(q_ref[...]), axis=1)

    def sel_fn(off, size):
        bias = jnp.where(sc_ref[pl.ds(off, size), :] >= thr_sel, 0.0, MASK_VAL)
        return _lane_tile(bias, N_KV_HEADS * GROUP)

    outs = _masked_attention(nch, sel_fn, k_ref, v_ref, qall, s_ref, acc_ref)
    for kv in range(N_KV_HEADS):
        o = outs[kv]
        for g in range(GROUP):
            hq = kv * GROUP + g
            o_ref[hq * HEAD_DIM:(hq + 1) * HEAD_DIM, :] = o[:, g * Q_BLOCK:(g + 1) * Q_BLOCK].astype(BF16)


def _dsa(iqT, miscT, ik_rows, qT, k_rows, vT128, ltri, B, T):
    nq = T // Q_BLOCK
    k_top = min(DSA_TOPK_MAX, T // 4)
    N = B * T
    return pl.pallas_call(
        functools.partial(_dsa_kernel, k_top=k_top),
        grid=(B, nq),
        in_specs=[
            pl.BlockSpec((IDX_HEADS * IDX_DIM, Q_BLOCK), lambda b, i: (0, b * nq + i)),
            pl.BlockSpec((MISC_ROWS, Q_BLOCK), lambda b, i: (0, b * nq + i)),
            pl.BlockSpec((T, LANE), lambda b, i: (b, 0)),
            pl.BlockSpec((MIX_WIDTH, Q_BLOCK), lambda b, i: (0, b * nq + i)),
            pl.BlockSpec((T, KV_WIDTH), lambda b, i: (b, 0)),
            pl.BlockSpec((nq, KV_WIDTH, LANE), lambda b, i: (b, 0, 0)),
            pl.BlockSpec((KEY_CHUNK, KEY_CHUNK), lambda b, i: (0, 0)),
        ],
        out_specs=pl.BlockSpec((MIX_WIDTH, Q_BLOCK), lambda b, i: (0, b * nq + i)),
        out_shape=jax.ShapeDtypeStruct((MIX_WIDTH, N), BF16),
        scratch_shapes=[
            pltpu.VMEM((T, Q_BLOCK), F32),
            pltpu.VMEM((T, N_KV_HEADS * GROUP * Q_BLOCK), F32),
            pltpu.VMEM((KV_WIDTH + ONES_ROWS, N_KV_HEADS * GROUP * Q_BLOCK), F32),
        ],
        compiler_params=_cparams(("parallel", "arbitrary")),
        name="dsa_attention",
    )(iqT, miscT, ik_rows, qT, k_rows, vT128, ltri)


def _band_kernel(q_ref, k_ref, v_ref, sink_ref, o_ref, *, n_prev, window, use_sink):
    i = pl.program_id(1)
    blk0 = jnp.maximum(i - n_prev, 0)
    nblk = n_prev + 1
    wb = nblk * Q_BLOCK
    kb = k_ref[pl.ds(pl.multiple_of(blk0 * Q_BLOCK, Q_BLOCK), wb), :]
    vb = jnp.concatenate([_v_chunk(v_ref, blk0, nblk), jnp.ones((ONES_ROWS, wb), BF16)], axis=0)
    qall = jnp.concatenate(_padded_queries(q_ref[...]), axis=1)
    rel = ((i - blk0) * Q_BLOCK + lax.broadcasted_iota(jnp.int32, (wb, Q_BLOCK), 1)
           - lax.broadcasted_iota(jnp.int32, (wb, Q_BLOCK), 0))
    bias = jnp.where((rel >= 0) & (rel < window), 0.0, MASK_VAL)
    s = jnp.dot(kb, qall, preferred_element_type=F32) + _lane_tile(bias, N_KV_HEADS * GROUP)
    m = jnp.max(s, axis=0, keepdims=True)
    if use_sink:
        sink = sink_ref[0:1, :]
        m = jnp.maximum(m, sink)
    o = jnp.dot(vb, jnp.exp2(s - m).astype(BF16), preferred_element_type=F32)
    l = o[KV_WIDTH:KV_WIDTH + 1, :]
    if use_sink:
        l = l + jnp.exp2(sink - m)
    inv = 1.0 / l
    half = GROUP * Q_BLOCK
    for kv in range(N_KV_HEADS):
        for g in range(GROUP):
            hq = kv * GROUP + g
            lanes = slice(kv * half + g * Q_BLOCK, kv * half + (g + 1) * Q_BLOCK)
            o_ref[hq * HEAD_DIM:(hq + 1) * HEAD_DIM, :] = (
                o[kv * HEAD_DIM:(kv + 1) * HEAD_DIM, lanes] * inv[:, lanes]).astype(o_ref.dtype)


def _band(qT, q_mixer, k_rows, kv_index, vT128, sink_lanes, window, use_sink, out_dtype, B, T, name):
    nq = T // Q_BLOCK
    N = B * T
    n_prev = (window + Q_BLOCK - 2) // Q_BLOCK
    return pl.pallas_call(
        functools.partial(_band_kernel, n_prev=n_prev, window=window, use_sink=use_sink),
        grid=(B, nq),
        in_specs=[
            pl.BlockSpec((MIX_WIDTH, Q_BLOCK), lambda b, i: (q_mixer, b * nq + i)),
            pl.BlockSpec((T, KV_WIDTH), lambda b, i: (b, kv_index)),
            pl.BlockSpec((nq, KV_WIDTH, LANE), lambda b, i: (b, kv_index, 0)),
            pl.BlockSpec((8, N_Q_HEADS * Q_BLOCK), lambda b, i: (0, 0)),
        ],
        out_specs=pl.BlockSpec((MIX_WIDTH, Q_BLOCK), lambda b, i: (0, b * nq + i)),
        out_shape=jax.ShapeDtypeStruct((MIX_WIDTH, N), out_dtype),
        compiler_params=_cparams(("parallel", "arbitrary")),
        name=name,
    )(qT, k_rows, vT128, sink_lanes)


def _cmp_kernel(xa_ref, xb_ref, pe_ref, w1_ref, w2_ref, o_ref):
    half = (CMP_LEN // 2) * HEAD_DIM
    xa = (xa_ref[...].astype(F32) + pe_ref[0:1, :]).astype(BF16)
    xb = (xb_ref[...].astype(F32) + pe_ref[1:2, :]).astype(BF16)
    hid = (jnp.dot(xa, w1_ref[0:half, :], preferred_element_type=F32)
           + jnp.dot(xb, w1_ref[half:2 * half, :], preferred_element_type=F32))
    hid = jnp.maximum(hid, 0.0).astype(BF16)
    o_ref[...] = jnp.dot(hid, w2_ref[...], preferred_element_type=F32)


def _compress(chunks, pe, w1, w2, name):
    G, nc, half = chunks.shape
    nxt = jnp.concatenate([chunks[:, 1:], jnp.zeros((G, 1, half), chunks.dtype)], axis=1)
    pe2 = jnp.zeros((8, half), F32).at[0:2].set(pe.reshape(2, half))
    return pl.pallas_call(
        _cmp_kernel,
        grid=(G,),
        in_specs=[
            pl.BlockSpec((None, nc, half), lambda g: (g, 0, 0)),
            pl.BlockSpec((None, nc, half), lambda g: (g, 0, 0)),
            pl.BlockSpec((8, half), lambda g: (0, 0)),
            pl.BlockSpec((2 * half, CMP_HIDDEN), lambda g: (0, 0)),
            pl.BlockSpec((CMP_HIDDEN, HEAD_DIM), lambda g: (0, 0)),
        ],
        out_specs=pl.BlockSpec((None, nc, HEAD_DIM), lambda g: (g, 0, 0)),
        out_shape=jax.ShapeDtypeStruct((G, nc, HEAD_DIM), F32),
        compiler_params=_cparams(("parallel",)),
        name=name,
    )(chunks, nxt, pe2, w1.astype(BF16), w2.astype(BF16))


def _nsa_kernel(q_ref, kc_ref, vc_ref, ks_ref, vs_ref, misc_ref, ow_ref, ov_ref, e_ref, o_ref,
                s_ref, acc_ref, *, n_top):
    C = KEY_CHUNK
    i = pl.program_id(1)
    nch = i // (C // Q_BLOCK) + 1
    ncmp = kc_ref.shape[0]
    nsel = ov_ref.shape[0]
    half = GROUP * Q_BLOCK
    t = i * Q_BLOCK + lax.broadcasted_iota(jnp.int32, (1, Q_BLOCK), 1)
    crow = lax.broadcasted_iota(jnp.int32, (ncmp, Q_BLOCK), 0)
    jrow = lax.broadcasted_iota(jnp.int32, (nsel, Q_BLOCK), 0)
    cur = t // SLC_LEN
    qall = jnp.concatenate(_padded_queries(q_ref[...]), axis=1)
    ov = ov_ref[...]

    cmp_bias = jnp.where(crow * CMP_STRIDE + (CMP_LEN - 1) <= t, 0.0, MASK_VAL)
    s = jnp.dot(kc_ref[...], qall, preferred_element_type=F32) + _lane_tile(cmp_bias, N_KV_HEADS * GROUP)
    m = jnp.maximum(jnp.max(s, axis=0, keepdims=True), M_INIT)
    p_all = jnp.exp2(s - m)
    l = jnp.sum(p_all, axis=0, keepdims=True)
    p_all = p_all * jnp.where(l > 0.0, 1.0 / l, 0.0)
    o_cmp_all = jnp.dot(vc_ref[...], p_all.astype(BF16), preferred_element_type=F32)

    o_cmps, selbs = [], []
    for kv in range(N_KV_HEADS):
        o_cmp = o_cmp_all[kv * HEAD_DIM:(kv + 1) * HEAD_DIM, kv * half:(kv + 1) * half]
        pg = p_all[:, kv * half:kv * half + Q_BLOCK]
        for g in range(1, GROUP):
            pg = pg + p_all[:, kv * half + g * Q_BLOCK:kv * half + (g + 1) * Q_BLOCK]
        p_hi = pg.astype(BF16)
        p_lo = (pg - p_hi.astype(F32)).astype(BF16)
        imp = (jnp.dot(ov, p_hi, preferred_element_type=F32)
               + jnp.dot(ov, p_lo, preferred_element_type=F32))
        forced = (jrow == 0) | (jrow == cur) | (jrow == cur - 1)
        imp = jnp.where(forced, FORCE_SCORE, imp)
        imp = jnp.where(jrow * SLC_LEN <= t, imp, NEG_INF)
        tiles = [imp[r0:r0 + 8] for r0 in range(0, nsel, 8)]
        ranks = [jnp.zeros((8, Q_BLOCK), F32) for _ in tiles]
        sub = lax.broadcasted_iota(jnp.int32, (8, Q_BLOCK), 0)
        for ii in range(nsel):
            r = jnp.broadcast_to(imp[ii:ii + 1], (8, Q_BLOCK))
            for jt, tile in enumerate(tiles):
                ge = jnp.where(r >= tile, 1.0, 0.0)
                gt = jnp.where(r > tile, 1.0, 0.0)
                if jt > ii // 8:
                    ahead = ge
                elif jt < ii // 8:
                    ahead = gt
                else:
                    ahead = jnp.where(sub > ii % 8, ge, gt)
                ranks[jt] = ranks[jt] + ahead
        rank = jnp.concatenate(ranks, axis=0)
        selbs.append(jnp.where(rank < float(n_top), 1.0, 0.0).astype(BF16))
        o_cmps.append(o_cmp)

    selb_all = jnp.concatenate(selbs, axis=1)

    def sel_fn(off, size):
        mk = jnp.dot(e_ref[pl.ds(off, size), :], selb_all, preferred_element_type=F32)
        causal = (off + lax.broadcasted_iota(jnp.int32, (size, Q_BLOCK), 0)) <= t
        parts = []
        for kv in range(N_KV_HEADS):
            keep = jnp.where(causal, mk[:, kv * Q_BLOCK:(kv + 1) * Q_BLOCK], 0.0) > 0.5
            parts.append(_lane_tile(jnp.where(keep, 0.0, MASK_VAL), GROUP))
        return jnp.concatenate(parts, axis=1)

    o_slcs = _masked_attention(nch, sel_fn, ks_ref, vs_ref, qall, s_ref, acc_ref)

    for kv in range(N_KV_HEADS):
        for g in range(GROUP):
            hq = kv * GROUP + g
            gr = GATE_ROW0 + hq * 3
            gate = [1.0 / (1.0 + jnp.exp(-misc_ref[gr + br:gr + br + 1, :])) for br in range(3)]
            out = (gate[0] * o_cmps[kv][:, g * Q_BLOCK:(g + 1) * Q_BLOCK]
                   + gate[1] * o_slcs[kv][:, g * Q_BLOCK:(g + 1) * Q_BLOCK]
                   + gate[2] * ow_ref[hq * HEAD_DIM:(hq + 1) * HEAD_DIM, :])
            o_ref[hq * HEAD_DIM:(hq + 1) * HEAD_DIM, :] = out.astype(BF16)


def _nsa(qT, kcmp_rows, vcmpT, k_rows, vT128, miscT, o_winT, ovT, emat, B, T):
    nq = T // Q_BLOCK
    N = B * T
    ncmp = T // CMP_STRIDE
    nsel = T // SLC_LEN
    n_top = min(SLC_TOPN, nsel)
    return pl.pallas_call(
        functools.partial(_nsa_kernel, n_top=n_top),
        grid=(B, nq),
        in_specs=[
            pl.BlockSpec((MIX_WIDTH, Q_BLOCK), lambda b, i: (2, b * nq + i)),
            pl.BlockSpec((None, ncmp, KV_WIDTH), lambda b, i: (b, 0, 0)),
            pl.BlockSpec((None, KV_WIDTH, ncmp), lambda b, i: (b, 0, 0)),
            pl.BlockSpec((T, KV_WIDTH), lambda b, i: (b, 3)),
            pl.BlockSpec((nq, KV_WIDTH, LANE), lambda b, i: (b, 3, 0)),
            pl.BlockSpec((MISC_ROWS, Q_BLOCK), lambda b, i: (0, b * nq + i)),
            pl.BlockSpec((MIX_WIDTH, Q_BLOCK), lambda b, i: (0, b * nq + i)),
            pl.BlockSpec((nsel, ncmp), lambda b, i: (0, 0)),
            pl.BlockSpec((T, nsel), lambda b, i: (0, 0)),
        ],
        out_specs=pl.BlockSpec((MIX_WIDTH, Q_BLOCK), lambda b, i: (0, b * nq + i)),
        out_shape=jax.ShapeDtypeStruct((MIX_WIDTH, N), BF16),
        scratch_shapes=[
            pltpu.VMEM((T, N_KV_HEADS * GROUP * Q_BLOCK), F32),
            pltpu.VMEM((KV_WIDTH + ONES_ROWS, N_KV_HEADS * GROUP * Q_BLOCK), F32),
        ],
        compiler_params=_cparams(("parallel", "arbitrary")),
        name="nsa_attention",
    )(qT, kcmp_rows, vcmpT, k_rows, vT128, miscT, o_winT, ovT, emat)


def _merge_kernel(x_ref, g_ref, wg_ref, oa_ref, ob_ref, oc_ref, wb_ref, wo_ref, out_ref, *, token_major_in):
    x = x_ref[...].T if token_major_in else x_ref[...]
    tm = x.shape[1]
    ss = jnp.sum(x * x, axis=0, keepdims=True)
    h = (x * lax.rsqrt(ss * (1.0 / D_MODEL) + EPS) * _lane_tile(g_ref[...], tm // LANE)).astype(BF16)
    acc = None
    for n, o_ref in enumerate((oa_ref, ob_ref, oc_ref)):
        gm = jnp.dot(wg_ref[n * D_MODEL:(n + 1) * D_MODEL, :], h, preferred_element_type=F32)
        y = jnp.dot(wb_ref[n], o_ref[...], preferred_element_type=F32)
        term = y * (1.0 / (1.0 + jnp.exp(-gm)))
        acc = term if acc is None else acc + term
    out_ref[...] = x + jnp.dot(wo_ref[...], acc.astype(BF16), preferred_element_type=F32)


def _merge(xT, token_major_in, g_b, wgT, oa, ob, oc, wbT, woT):
    N = xT.shape[0] if token_major_in else xT.shape[1]
    tm = 1024
    col = lambda i: (0, i)
    return pl.pallas_call(
        functools.partial(_merge_kernel, token_major_in=token_major_in),
        grid=(N // tm,),
        in_specs=[
            _x_spec(tm, token_major_in),
            pl.BlockSpec((D_MODEL, LANE), lambda i: (0, 0)),
            pl.BlockSpec((N_MIXERS * D_MODEL, D_MODEL), lambda i: (0, 0)),
            pl.BlockSpec((MIX_WIDTH, tm), col),
            pl.BlockSpec((MIX_WIDTH, tm), col),
            pl.BlockSpec((MIX_WIDTH, tm), col),
            pl.BlockSpec((N_MIXERS, D_MODEL, MIX_WIDTH), lambda i: (0, 0, 0)),
            pl.BlockSpec((D_MODEL, D_MODEL), lambda i: (0, 0)),
        ],
        out_specs=pl.BlockSpec((D_MODEL, tm), col),
        out_shape=jax.ShapeDtypeStruct((D_MODEL, N), F32),
        compiler_params=_cparams(("parallel",)),
        name="merge_out_projection",
    )(xT, g_b, wgT, oa, ob, oc, wbT, woT)


def _mlp_kernel(x_ref, g_ref, wu_ref, wd_ref, out_ref, h_ref, acc_ref, *, token_major_out):
    f = pl.program_id(1)
    tm = x_ref.shape[1]

    @pl.when(f == 0)
    def _():
        x = x_ref[...]
        ss = jnp.sum(x * x, axis=0, keepdims=True)
        h_ref[...] = (x * lax.rsqrt(ss * (1.0 / D_MODEL) + EPS)
                      * _lane_tile(g_ref[...], tm // LANE)).astype(BF16)
        acc_ref[...] = jnp.zeros(acc_ref.shape, F32)

    u = jnp.maximum(jnp.dot(wu_ref[...], h_ref[...], preferred_element_type=F32), 0.0)
    acc_ref[...] += jnp.dot(wd_ref[...], (u * u).astype(BF16), preferred_element_type=F32)

    @pl.when(f == pl.num_programs(1) - 1)
    def _():
        y = x_ref[...] + acc_ref[...]
        out_ref[...] = y.T if token_major_out else y


def _mlp(xT, g_b, wuT, wdT, token_major_out):
    N = xT.shape[1]
    tm, tf = 1024, 1024
    if token_major_out:
        out_spec = pl.BlockSpec((tm, D_MODEL), lambda i, f: (i, 0))
        out_shape = jax.ShapeDtypeStruct((N, D_MODEL), F32)
    else:
        out_spec = pl.BlockSpec((D_MODEL, tm), lambda i, f: (0, i))
        out_shape = jax.ShapeDtypeStruct((D_MODEL, N), F32)
    return pl.pallas_call(
        functools.partial(_mlp_kernel, token_major_out=token_major_out),
        grid=(N // tm, D_FF // tf),
        in_specs=[
            pl.BlockSpec((D_MODEL, tm), lambda i, f: (0, i)),
            pl.BlockSpec((D_MODEL, LANE), lambda i, f: (0, 0)),
            pl.BlockSpec((tf, D_MODEL), lambda i, f: (f, 0)),
            pl.BlockSpec((D_MODEL, tf), lambda i, f: (0, f)),
        ],
        out_specs=out_spec,
        out_shape=out_shape,
        scratch_shapes=[pltpu.VMEM((D_MODEL, tm), BF16), pltpu.VMEM((D_MODEL, tm), F32)],
        compiler_params=_cparams(("parallel", "arbitrary")),
        name="relu2_mlp",
    )(xT, g_b, wuT, wdT)


def _to_chunks(rows, B, T):
    half_tokens = CMP_LEN // 2
    t = rows.reshape(B, T, N_KV_HEADS, HEAD_DIM).transpose(0, 2, 1, 3)
    return t.reshape(B * N_KV_HEADS, T // half_tokens, half_tokens * HEAD_DIM)


def kernel(x, norm_mix, w_in, q_norm, k_norm, sinks, cmp_pe_k, cmp_pe_v, w_ck1, w_ck2, w_cv1, w_cv2,
           w_branch, w_out, norm_mlp, w_up, w_down):
    B, T, D = x.shape
    N = B * T
    depth = w_in.shape[0]
    nq = T // Q_BLOCK
    ncmp = T // CMP_STRIDE
    nsel = T // SLC_LEN

    inv_freq = ROPE_THETA ** (-jnp.arange(0, HEAD_DIM, 2, dtype=F32) / HEAD_DIM)
    ang = jnp.arange(T, dtype=F32)[:, None] * inv_freq[None, :]
    cosT, sinT = jnp.cos(ang).T, jnp.sin(ang).T

    ci = np.arange(KEY_CHUNK)
    ltri = jnp.asarray(ci[None, :] < ci[:, None], BF16)
    emat = jnp.asarray((np.arange(T)[:, None] // SLC_LEN) == np.arange(nsel)[None, :], BF16)
    cstart = np.arange(ncmp) * CMP_STRIDE
    sstart = np.arange(nsel) * SLC_LEN
    n_cmp_valid = (T - CMP_LEN) // CMP_STRIDE + 1
    ov = ((cstart[None, :] < sstart[:, None] + SLC_LEN) & (cstart[None, :] + CMP_LEN > sstart[:, None])
          & (np.arange(ncmp)[None, :] < n_cmp_valid))
    ovT = jnp.asarray(ov, BF16)

    xT = x.reshape(N, D)

    sl = lambda s: w_in[:, :, s[0]:s[1]]
    wT_all = jnp.concatenate(
        [sl(S_QA), sl(S_QB), sl(S_QC),
         sl(S_KA), sl(S_KB), sl(S_KC), sl(S_KSL), sl(S_KWN),
         sl(S_VA), sl(S_VB), sl(S_VC), sl(S_VSL), sl(S_VWN),
         sl(S_IQ), sl(S_IK), sl(S_IW), jnp.zeros((depth, D, GATE_ROW0 - IDX_HEADS), w_in.dtype), sl(S_GC)],
        axis=2).transpose(0, 2, 1).astype(BF16)
    wgT_all = sl(S_GM).transpose(0, 2, 1).astype(BF16)
    wbT_all = w_branch.transpose(0, 1, 3, 2).astype(BF16)
    woT_all = w_out.transpose(0, 2, 1).astype(BF16)
    wuT_all = w_up.transpose(0, 2, 1).astype(BF16)
    wdT_all = w_down.transpose(0, 2, 1).astype(BF16)
    lane_b = lambda v: jnp.broadcast_to(v.astype(F32)[..., None], v.shape + (LANE,))
    gmix_all, gmlp_all = lane_b(norm_mix), lane_b(norm_mlp)
    gq_all = lane_b(jnp.tile(q_norm[:, :, None, :], (1, 1, N_Q_HEADS, 1)).reshape(depth, -1))
    gk_all = lane_b(jnp.tile(k_norm[:, jnp.array([0, 1, 2, 2, 2])][:, :, None, :],
                             (1, 1, N_KV_HEADS, 1)).reshape(depth, -1))
    sink_all = jnp.broadcast_to(jnp.repeat(sinks.astype(F32) * LOG2E, Q_BLOCK, axis=1)[:, None, :],
                                (depth, 8, N_Q_HEADS * Q_BLOCK))

    for l in range(depth):
        wgT, sink_lanes = wgT_all[l], sink_all[l]
        qT, k_rows, vT128, iqT, ik_rows, miscT = _project(
            xT, l == 0, gmix_all[l], wT_all[l], gq_all[l], gk_all[l], cosT, sinT, T)

        o_a = _dsa(iqT, miscT, ik_rows, qT, k_rows, vT128, ltri, B, T)
        o_b = _band(qT, 1, k_rows, 1, vT128, sink_lanes, SWA_WINDOW, True, BF16, B, T, "swa_attention")
        o_w = _band(qT, 2, k_rows, 4, vT128, sink_lanes, NSA_WINDOW, False, F32, B, T, "nsa_window_attention")

        kc_chunks = _to_chunks(k_rows[:, 2 * KV_WIDTH:3 * KV_WIDTH].reshape(B, T, KV_WIDTH), B, T)
        vc_rows = vT128[:, 2 * KV_WIDTH:3 * KV_WIDTH, :].reshape(B, nq, KV_WIDTH, LANE)
        vc_rows = vc_rows.transpose(0, 1, 3, 2).reshape(B, T, KV_WIDTH)
        vc_chunks = _to_chunks(vc_rows, B, T)
        k_cmp = _compress(kc_chunks, cmp_pe_k[l], w_ck1[l], w_ck2[l], "compress_k")
        v_cmp = _compress(vc_chunks, cmp_pe_v[l], w_cv1[l], w_cv2[l], "compress_v")
        kcmp_rows = k_cmp.reshape(B, N_KV_HEADS, ncmp, HEAD_DIM).transpose(0, 2, 1, 3)
        kcmp_rows = kcmp_rows.reshape(B, ncmp, KV_WIDTH).astype(BF16)
        vcmpT = v_cmp.reshape(B, N_KV_HEADS, ncmp, HEAD_DIM).transpose(0, 1, 3, 2)
        vcmpT = vcmpT.reshape(B, KV_WIDTH, ncmp).astype(BF16)

        o_c = _nsa(qT, kcmp_rows, vcmpT, k_rows, vT128, miscT, o_w, ovT, emat, B, T)

        xT = _merge(xT, l == 0, gmix_all[l], wgT, o_a, o_b, o_c, wbT_all[l], woT_all[l])
        xT = _mlp(xT, gmlp_all[l], wuT_all[l], wdT_all[l], token_major_out=(l == depth - 1))

    return xT.reshape(B, T, D)
```
